```python
import jax
import jax.numpy as jnp
from jax import lax
import numpy as np

D_MODEL = 1024
BATCH = 4
SEQ = 4096
DEPTH = 2
DEC_BATCH = 32
DEC_SEQ = 8
PAST_LEN = 16384
PAGE_SIZE = 128

RW_HEADS = 16
RW_HD = 64
RW_DIM = RW_HEADS * RW_HD
RW_W_LORA = 64
RW_A_LORA = 64
RW_G_LORA = 128
RW_IN = 3 * RW_DIM + RW_W_LORA + RW_A_LORA + RW_G_LORA
RW_SPLITS = (RW_DIM, 2 * RW_DIM, 3 * RW_DIM, 3 * RW_DIM + RW_W_LORA, 3 * RW_DIM + RW_W_LORA + RW_A_LORA)
RW_GN_EPS = 64e-5
SSM_DIM = 2 * D_MODEL
SSM_HD = 64
SSM_HEADS = SSM_DIM // SSM_HD
SSM_GROUPS = 4
SSM_RPG = SSM_HEADS // SSM_GROUPS
SSM_STATE = 128
SSM_CONV = 4
SSM_CHUNK = 128
SSM_XBC = SSM_DIM + 2 * SSM_GROUPS * SSM_STATE
SSM_IN = SSM_DIM + SSM_XBC + SSM_HEADS
NSA_HEADS = 16
NSA_HD = 64
NSA_KV_HEADS = 4
NSA_RPG = NSA_HEADS // NSA_KV_HEADS
NSA_DIM = NSA_HEADS * NSA_HD
NSA_KV_DIM = NSA_KV_HEADS * NSA_HD
CMP_STRIDE = 16
CMP_BLOCK = 2 * CMP_STRIDE
CMP_HIDDEN = 128
SEL_BLOCK = 64
N_SELECT = 16
WINDOW = 512
Q_BLOCK = 64
SEL_FORCE = 1e6
NSA_IN = NSA_DIM + 6 * NSA_KV_DIM + 3 * NSA_HEADS
NSA_SPLITS = tuple(NSA_DIM + i * NSA_KV_DIM for i in range(7))
N_BRANCH = 3
IN_SPLITS = (RW_IN, RW_IN + SSM_IN, RW_IN + SSM_IN + NSA_IN)
IN_DIM = RW_IN + SSM_IN + NSA_IN + N_BRANCH * D_MODEL
FFN_HIDDEN = ((8 * D_MODEL // 3 + 255) // 256) * 256
EPS = 1e-6

kernel_name = 'hybrid_rwkv7_mamba2_nsa_adaln_step'


def rmsnorm(x, g):
    xf = x.astype(jnp.float32)
    y = xf * lax.rsqrt(jnp.mean(xf * xf, axis=-1, keepdims=True) + EPS)
    return (y * g.astype(jnp.float32)).astype(x.dtype)


def masked_softmax(s, mask):
    s = jnp.where(mask, s.astype(jnp.float32), -jnp.inf)
    m = jnp.max(s, axis=-1, keepdims=True)
    m = jnp.where(jnp.isfinite(m), m, 0.0)
    e = jnp.exp(s - m)
    return e / jnp.maximum(jnp.sum(e, axis=-1, keepdims=True), jnp.finfo(jnp.float32).tiny)


def rwkv7_mix(proj, shift_state, wkv_state, p):
    B, T, _ = proj.shape
    f32 = jnp.float32
    prev = jnp.concatenate([shift_state[:, None].astype(proj.dtype), proj[:, :-1]], axis=1)
    xs = (proj + p['rw_mu'] * (prev - proj)).astype(f32)
    r, k, v, wl, al, gl = jnp.split(xs, RW_SPLITS, axis=-1)
    w = -jax.nn.softplus(-(p['rw_w0'] + jnp.tanh(wl) @ p['rw_w2'])) - 0.5
    decay = jnp.exp(-jnp.exp(w))
    a = jax.nn.sigmoid(p['rw_a0'] + al @ p['rw_a2'])
    g = jax.nn.sigmoid(gl) @ p['rw_g2']
    kk = k * p['rw_k_k']
    k = k * (1.0 + (a - 1.0) * p['rw_k_a'])
    hd = lambda t: t.reshape(B, T, RW_HEADS, RW_HD)
    r, k, v, kk, a, decay = hd(r), hd(k), hd(v), hd(kk), hd(a), hd(decay)
    kk = kk * lax.rsqrt(jnp.sum(kk * kk, axis=-1, keepdims=True) + 1e-12)

    def step(S, inp):
        r_t, w_t, k_t, v_t, kk_t, a_t = inp
        sa = jnp.einsum('bhvk,bhk->bhv', S, kk_t)
        S = S * w_t[:, :, None, :] - sa[..., None] * (kk_t * a_t)[:, :, None, :] + v_t[..., None] * k_t[:, :, None, :]
        return S, jnp.einsum('bhvk,bhk->bhv', S, r_t)

    seqs = tuple(jnp.swapaxes(t, 0, 1) for t in (r, decay, k, v, kk, a))
    S_fin, o = lax.scan(step, wkv_state.astype(f32), seqs)
    o = jnp.swapaxes(o, 0, 1)
    mean = jnp.mean(o, axis=-1, keepdims=True)
    var = jnp.mean(jnp.square(o - mean), axis=-1, keepdims=True)
    o = ((o - mean) * lax.rsqrt(var + RW_GN_EPS)).reshape(B, T, RW_DIM) * p['rw_lnx_g'] + p['rw_lnx_b']
    bonus = jnp.sum(r * k * p['rw_r_k'], axis=-1, keepdims=True) * v
    o = (o + bonus.reshape(B, T, RW_DIM)) * g
    return o.astype(proj.dtype), proj[:, -1], S_fin.astype(proj.dtype)


def ssd_chunked(x, dt, A, Bm, Cm, s0):
    Bsz, T = x.shape[:2]
    L = SSM_CHUNK if T % SSM_CHUNK == 0 else T
    nc = T // L
    ch = lambda t: t.reshape((Bsz, nc, L) + t.shape[2:])
    xdt = ch(x * dt[..., None])
    a_cs = jnp.cumsum(ch(dt * A), axis=2)
    Bc, Cc = ch(Bm), ch(Cm)
    seg = a_cs[:, :, :, None] - a_cs[:, :, None, :]
    causal = jnp.tril(jnp.ones((L, L), dtype=bool))[:, :, None, None]
    decay_ls = jnp.exp(jnp.where(causal, seg, -jnp.inf))
    cb = jnp.einsum('bclgn,bcsgn->bclsg', Cc, Bc)
    y_diag = jnp.einsum('bclsgr,bcsgrp->bclgrp', cb[..., None] * decay_ls, xdt)
    decay_s = jnp.exp(a_cs[:, :, -1:] - a_cs)
    states = jnp.einsum('bclgn,bclgr,bclgrp->bcgrpn', Bc, decay_s, xdt)
    chunk_decay = jnp.exp(a_cs[:, :, -1])

    def step(s, inp):
        st, dec = inp
        return s * dec[..., None, None] + st, s

    s_fin, s_in = lax.scan(step, s0, (jnp.swapaxes(states, 0, 1), jnp.swapaxes(chunk_decay, 0, 1)))
    s_in = jnp.swapaxes(s_in, 0, 1)
    y_off = jnp.einsum('bclgn,bcgrpn,bclgr->bclgrp', Cc, s_in, jnp.exp(a_cs))
    return (y_diag + y_off).reshape(x.shape), s_fin


def mamba2_mix(proj, conv_state, ssm_state, p):
    B, T, _ = proj.shape
    f32 = jnp.float32
    z, xbc, dt = jnp.split(proj, (SSM_DIM, SSM_DIM + SSM_XBC), axis=-1)
    xpad = jnp.concatenate([conv_state.astype(xbc.dtype), xbc], axis=1)
    conv = p['ssm_conv_b'] + sum(xpad[:, i:i + T] * p['ssm_conv_w'][i] for i in range(SSM_CONV))
    xbc_c = jax.nn.silu(conv)
    xs, Bm, Cm = jnp.split(xbc_c, (SSM_DIM, SSM_DIM + SSM_GROUPS * SSM_STATE), axis=-1)
    x = xs.reshape(B, T, SSM_GROUPS, SSM_RPG, SSM_HD).astype(f32)
    Bm = Bm.reshape(B, T, SSM_GROUPS, SSM_STATE).astype(f32)
    Cm = Cm.reshape(B, T, SSM_GROUPS, SSM_STATE).astype(f32)
    dt = jax.nn.softplus((dt + p['ssm_dt_bias']).astype(f32)).reshape(B, T, SSM_GROUPS, SSM_RPG)
    A = -jnp.exp(p['ssm_a_log'].astype(f32)).reshape(SSM_GROUPS, SSM_RPG)
    s0 = ssm_state.astype(f32).reshape(B, SSM_GROUPS, SSM_RPG, SSM_HD, SSM_STATE)
    y, s_fin = ssd_chunked(x, dt, A, Bm, Cm, s0)
    y = y + p['ssm_d'].astype(f32).reshape(SSM_GROUPS, SSM_RPG)[:, :, None] * x
    y = y.reshape(B, T, SSM_DIM) * jax.nn.silu(z.astype(f32))
    y = rmsnorm(y, p['ssm_norm_g'])
    return (y.astype(proj.dtype), xpad[:, -(SSM_CONV - 1):],
            s_fin.reshape(B, SSM_HEADS, SSM_HD, SSM_STATE).astype(proj.dtype))


def nsa_project(proj, p):
    B, T, _ = proj.shape
    q, kc, vc, ks, vs, kw, vw, gates = jnp.split(proj, NSA_SPLITS, axis=-1)
    kvh = lambda t: t.reshape(B, T, NSA_KV_HEADS, NSA_HD)
    q = rmsnorm(q.reshape(B, T, NSA_HEADS, NSA_HD), p['nsa_q_norm']).reshape(B, T, NSA_KV_HEADS, NSA_RPG, NSA_HD)
    cmp_rows = jnp.stack([kvh(kc), kvh(vc)], axis=2)
    sel_rows = jnp.stack([rmsnorm(kvh(ks), p['nsa_k_norm'][1]), kvh(vs)], axis=2)
    win_rows = jnp.stack([rmsnorm(kvh(kw), p['nsa_k_norm'][2]), kvh(vw)], axis=2)
    g = jax.nn.sigmoid(gates.astype(jnp.float32)).reshape(B, T, 3, NSA_KV_HEADS, NSA_RPG, 1)
    return q, cmp_rows, sel_rows, win_rows, g


def half_proj(rows, w1):
    B, T = rows.shape[:2]
    h = rows.reshape(B, T // CMP_STRIDE, CMP_STRIDE, 2, NSA_KV_HEADS, NSA_HD)
    top = jnp.einsum('bnsckd,csdh->bnckh', h, w1[:, :CMP_STRIDE])
    bot = jnp.einsum('bnsckd,csdh->bnckh', h, w1[:, CMP_STRIDE:])
    return top, bot


def compress_finish(top, bot, p):
    w1 = p['nsa_cmp_w1']
    const = jnp.einsum('csd,csdh->ch', p['nsa_cmp_pos'], w1) + p['nsa_cmp_b1']
    hid = jax.nn.gelu(top[:, :-1] + bot[:, 1:] + const[:, None, :])
    kv = jnp.einsum('bnckh,chd->bnckd', hid, p['nsa_cmp_w2'])
    return rmsnorm(kv[:, :, 0], p['nsa_k_norm'][0]), kv[:, :, 1]


def cmp_to_sel(nc, ns):
    i = jnp.arange(nc)[:, None] * CMP_STRIDE
    j = jnp.arange(ns)[None, :] * SEL_BLOCK
    ov = jnp.minimum(i + CMP_BLOCK, j + SEL_BLOCK) - jnp.maximum(i, j)
    return jnp.clip(ov, 0, None).astype(jnp.float32) / CMP_BLOCK


def fetch_rows(kv, pos):
    b = jnp.arange(kv.shape[0])[:, None, None, None]
    g = jnp.arange(NSA_KV_HEADS)[None, :, None, None]
    return kv[b, jnp.clip(pos, 0, kv.shape[1] - 1), :, g, :]


def fetch_paged(pool, l, page_table, new_rows, pos):
    b = jnp.arange(page_table.shape[0])[:, None, None, None]
    g = jnp.arange(NSA_KV_HEADS)[None, :, None, None]
    pp = jnp.clip(pos, 0, PAST_LEN - 1)
    phys = page_table[b, pp // PAGE_SIZE]
    past = pool[l, phys, pp % PAGE_SIZE, :, g, :]
    new = fetch_rows(new_rows, pos - PAST_LEN)
    return jnp.where((pos < PAST_LEN)[..., None, None], past.astype(new.dtype), new)


def nsa_cmp_sel(q, t, kc, vc, fetch, ns):
    B, Tq = q.shape[:2]
    scale = NSA_HD ** -0.5
    nc = kc.shape[1]
    s = jnp.einsum('bqgrd,bngd->bgrqn', q, kc) * scale
    cmp_end = jnp.arange(nc) * CMP_STRIDE + CMP_BLOCK - 1
    p_cmp = masked_softmax(s, cmp_end[None, :] <= t[:, None])
    o_cmp = jnp.einsum('bgrqn,bngd->bqgrd', p_cmp.astype(vc.dtype), vc)
    imp = jnp.einsum('bgrqn,ns->bgqs', p_cmp, cmp_to_sel(nc, ns))
    blk = jnp.arange(ns)[None, :]
    cur = t[:, None] // SEL_BLOCK
    forced = (blk == 0) | (blk == cur) | (blk == cur - 1)
    valid = blk * SEL_BLOCK <= t[:, None]
    score = jnp.where(forced, SEL_FORCE, jnp.where(valid, imp, -SEL_FORCE))
    n_top = min(N_SELECT, ns)
    _, idx = lax.top_k(score, n_top)
    pos = (idx[..., None] * SEL_BLOCK + jnp.arange(SEL_BLOCK)).reshape(B, NSA_KV_HEADS, Tq, n_top * SEL_BLOCK)
    kvs = fetch(pos)
    s2 = jnp.einsum('bqgrd,bgqkd->bgrqk', q, kvs[..., 0, :]) * scale
    p_sel = masked_softmax(s2, (pos <= t[:, None])[:, :, None])
    o_sel = jnp.einsum('bgrqk,bgqkd->bqgrd', p_sel.astype(kvs.dtype), kvs[..., 1, :])
    return o_cmp, o_sel


def nsa_window(q, t, kw, vw, kpos):
    s = jnp.einsum('bqgrd,bngd->bgrqn', q, kw) * (NSA_HD ** -0.5)
    mask = (kpos[None, :] <= t[:, None]) & (kpos[None, :] > t[:, None] - WINDOW) & (kpos[None, :] >= 0)
    p = masked_softmax(s, mask)
    return jnp.einsum('bgrqn,bngd->bqgrd', p.astype(vw.dtype), vw)


def nsa_merge(o, g):
    B, T = o.shape[:2]
    return jnp.sum(g.astype(o.dtype) * o, axis=2).reshape(B, T, NSA_DIM)


def nsa_prompt(proj, p):
    q, cmp_rows, sel_rows, win_rows, g = nsa_project(proj, p)
    B, T = proj.shape[:2]
    top, bot = half_proj(cmp_rows, p['nsa_cmp_w1'])
    kc, vc = compress_finish(top, bot, p)
    ns = -(-T // SEL_BLOCK)
    fetch = lambda pos: fetch_rows(sel_rows, pos)
    win_pad = jnp.pad(win_rows, ((0, 0), (WINDOW, 0), (0, 0), (0, 0), (0, 0)))
    nqb = T // Q_BLOCK
    qb = jnp.swapaxes(q.reshape(B, nqb, Q_BLOCK, NSA_KV_HEADS, NSA_RPG, NSA_HD), 0, 1)

    def body(args):
        q_blk, i = args
        t = i * Q_BLOCK + jnp.arange(Q_BLOCK)
        o_cmp, o_sel = nsa_cmp_sel(q_blk, t, kc, vc, fetch, ns)
        kv_w = lax.dynamic_slice_in_dim(win_pad, i * Q_BLOCK, WINDOW + Q_BLOCK, axis=1)
        kpos = i * Q_BLOCK - WINDOW + jnp.arange(WINDOW + Q_BLOCK)
        o_win = nsa_window(q_blk, t, kv_w[:, :, 0], kv_w[:, :, 1], kpos)
        return jnp.stack([o_cmp, o_sel, o_win], axis=2)

    o = lax.map(body, (qb, jnp.arange(nqb)))
    o = jnp.swapaxes(o, 0, 1).reshape(B, T, 3, NSA_KV_HEADS, NSA_RPG, NSA_HD)
    return nsa_merge(o, g), (cmp_rows, sel_rows, win_pad[:, -WINDOW:])


def nsa_sample(proj, p, l, cache_cmp_kv, cache_sel_kv, win_buf, page_table):
    q, cmp_rows, sel_rows, win_rows, g = nsa_project(proj, p)
    B, T = proj.shape[:2]
    past_cmp = cache_cmp_kv[l, page_table].reshape(B, PAST_LEN, 2, NSA_KV_HEADS, NSA_HD)
    new_cmp = jnp.pad(cmp_rows, ((0, 0), (0, (-T) % CMP_STRIDE), (0, 0), (0, 0), (0, 0)))
    tp, bp = half_proj(past_cmp.astype(cmp_rows.dtype), p['nsa_cmp_w1'])
    tn, bn = half_proj(new_cmp, p['nsa_cmp_w1'])
    kc, vc = compress_finish(jnp.concatenate([tp, tn], axis=1), jnp.concatenate([bp, bn], axis=1), p)
    t = PAST_LEN + jnp.arange(T)
    ns = -(-(PAST_LEN + T) // SEL_BLOCK)
    fetch = lambda pos: fetch_paged(cache_sel_kv, l, page_table, sel_rows, pos)
    o_cmp, o_sel = nsa_cmp_sel(q, t, kc, vc, fetch, ns)
    kv_w = jnp.concatenate([win_buf.astype(win_rows.dtype), win_rows], axis=1)
    kpos = PAST_LEN - WINDOW + jnp.arange(WINDOW + T)
    o_win = nsa_window(q, t, kv_w[:, :, 0], kv_w[:, :, 1], kpos)
    o = jnp.stack([o_cmp, o_sel, o_win], axis=2)
    return nsa_merge(o, g), (cmp_rows, sel_rows, kv_w[:, -WINDOW:])


def trunk_layer(x, c, p, rw_shift, rw_wkv, conv_state, ssm_state, nsa_fn):
    B, T, _ = x.shape
    mod = (c @ p['w_ada'] + p['b_ada'])[:, None, :]
    sh1, sc1, gt1, sh2, sc2, gt2 = jnp.split(mod, 6, axis=-1)
    h = rmsnorm(x, p['ln1']) * (1.0 + sc1) + sh1
    proj = h @ p['w_in']
    p_rw, p_ssm, p_nsa, p_gate = jnp.split(proj, IN_SPLITS, axis=-1)
    o_rw, rw_shift, rw_wkv = rwkv7_mix(p_rw, rw_shift, rw_wkv, p)
    o_ssm, conv_state, ssm_state = mamba2_mix(p_ssm, conv_state, ssm_state, p)
    o_nsa, (cmp_rows, sel_rows, win_buf) = nsa_fn(p_nsa, p)
    gate = jax.nn.sigmoid(p_gate.astype(jnp.float32)).astype(x.dtype).reshape(B, T, N_BRANCH, D_MODEL)
    merged = (gate[:, :, 0] * (o_rw @ p['w_br_rw']) + gate[:, :, 1] * (o_ssm @ p['w_br_ssm'])
              + gate[:, :, 2] * (o_nsa @ p['w_br_nsa']))
    x = x + gt1 * (merged @ p['w_out'])
    h2 = rmsnorm(x, p['ln2']) * (1.0 + sc2) + sh2
    up, gf = jnp.split(h2 @ p['w_ffn_in'], 2, axis=-1)
    x = x + gt2 * ((jax.nn.silu(gf) * up) @ p['w_ffn_out'])
    return x, (rw_shift, rw_wkv, conv_state, ssm_state, cmp_rows, sel_rows, win_buf)


def setup_inputs(seed: int = 0) -> dict:
    key = jax.random.key(seed)
    keys = iter(jax.random.split(key, 64))
    f32 = jnp.float32

    def nrm(shape, scale):
        return jax.random.normal(next(keys), shape, f32) * scale

    def gain(shape):
        return 1.0 + nrm(shape, 0.05)

    L = DEPTH
    n_pages = PAST_LEN // PAGE_SIZE
    n_used = DEC_BATCH * n_pages
    n_pool = n_used + n_used // 4
    page_table = jax.random.permutation(next(keys), n_pool)[:n_used].reshape(DEC_BATCH, n_pages).astype(jnp.int32)
    dt0 = jnp.exp(jax.random.uniform(next(keys), (L, SSM_HEADS), f32, np.log(1e-3), np.log(1e-1)))
    return {
        'x_prompt': nrm((BATCH, SEQ, D_MODEL), 1.0),
        'x_sample': nrm((DEC_BATCH, DEC_SEQ, D_MODEL), 1.0),
        'cache_cmp_kv': nrm((L, n_pool, PAGE_SIZE, 2, NSA_KV_HEADS, NSA_HD), 1.0),
        'cache_sel_kv': nrm((L, n_pool, PAGE_SIZE, 2, NSA_KV_HEADS, NSA_HD), 1.0),
        'cache_win_kv': nrm((L, DEC_BATCH, WINDOW, 2, NSA_KV_HEADS, NSA_HD), 1.0),
        'state_rwkv_shift': nrm((L, DEC_BATCH, RW_IN), 1.0),
        'state_rwkv_wkv': nrm((L, DEC_BATCH, RW_HEADS, RW_HD, RW_HD), 0.5),
        'state_ssm_conv': nrm((L, DEC_BATCH, SSM_CONV - 1, SSM_XBC), 1.0),
        'state_ssm': nrm((L, DEC_BATCH, SSM_HEADS, SSM_HD, SSM_STATE), 0.2),
        'page_table': page_table,
        'c_prompt': nrm((BATCH, D_MODEL), 1.0),
        'c_sample': nrm((DEC_BATCH, D_MODEL), 1.0),
        'w_ada': nrm((L, D_MODEL, 6 * D_MODEL), 0.2 * D_MODEL ** -0.5),
        'b_ada': nrm((L, 6 * D_MODEL), 0.02),
        'ln1': gain((L, D_MODEL)),
        'ln2': gain((L, D_MODEL)),
        'w_in': nrm((L, D_MODEL, IN_DIM), D_MODEL ** -0.5),
        'rw_mu': jax.random.uniform(next(keys), (L, RW_IN), f32),
        'rw_w0': jax.random.uniform(next(keys), (L, RW_DIM), f32, -6.0, 1.0),
        'rw_w2': nrm((L, RW_W_LORA, RW_DIM), 0.1 * RW_W_LORA ** -0.5),
        'rw_a0': nrm((L, RW_DIM), 0.1),
        'rw_a2': nrm((L, RW_A_LORA, RW_DIM), 0.1 * RW_A_LORA ** -0.5),
        'rw_g2': nrm((L, RW_G_LORA, RW_DIM), RW_G_LORA ** -0.5),
        'rw_k_k': 0.85 + nrm((L, RW_DIM), 0.05),
        'rw_k_a': gain((L, RW_DIM)),
        'rw_r_k': nrm((L, RW_HEADS, RW_HD), 0.1),
        'rw_lnx_g': gain((L, RW_DIM)),
        'rw_lnx_b': nrm((L, RW_DIM), 0.02),
        'ssm_conv_w': nrm((L, SSM_CONV, SSM_XBC), 0.5),
        'ssm_conv_b': nrm((L, SSM_XBC), 0.02),
        'ssm_dt_bias': jnp.log(jnp.expm1(dt0)),
        'ssm_a_log': jnp.log(jax.random.uniform(next(keys), (L, SSM_HEADS), f32, 1.0, 16.0)),
        'ssm_d': gain((L, SSM_HEADS)),
        'ssm_norm_g': gain((L, SSM_DIM)),
        'nsa_q_norm': gain((L, NSA_HD)),
        'nsa_k_norm': gain((L, 3, NSA_HD)),
        'nsa_cmp_pos': nrm((L, 2, CMP_BLOCK, NSA_HD), 0.1),
        'nsa_cmp_w1': nrm((L, 2, CMP_BLOCK, NSA_HD, CMP_HIDDEN), (CMP_BLOCK * NSA_HD) ** -0.5),
        'nsa_cmp_b1': nrm((L, 2, CMP_HIDDEN), 0.02),
        'nsa_cmp_w2': nrm((L, 2, CMP_HIDDEN, NSA_HD), CMP_HIDDEN ** -0.5),
        'w_br_rw': nrm((L, RW_DIM, D_MODEL), RW_DIM ** -0.5),
        'w_br_ssm': nrm((L, SSM_DIM, D_MODEL), SSM_DIM ** -0.5),
        'w_br_nsa': nrm((L, NSA_DIM, D_MODEL), NSA_DIM ** -0.5),
        'w_out': nrm((L, D_MODEL, D_MODEL), D_MODEL ** -0.5),
        'w_ffn_in': nrm((L, D_MODEL, 2 * FFN_HIDDEN), D_MODEL ** -0.5),
        'w_ffn_out': nrm((L, FFN_HIDDEN, D_MODEL), FFN_HIDDEN ** -0.5),
    }


def reference(x_prompt, x_sample, cache_cmp_kv, cache_sel_kv, cache_win_kv, state_rwkv_shift, state_rwkv_wkv,
              state_ssm_conv, state_ssm, page_table, c_prompt, c_sample, w_ada, b_ada, ln1, ln2, w_in,
              rw_mu, rw_w0, rw_w2, rw_a0, rw_a2, rw_g2, rw_k_k, rw_k_a, rw_r_k, rw_lnx_g, rw_lnx_b,
              ssm_conv_w, ssm_conv_b, ssm_dt_bias, ssm_a_log, ssm_d, ssm_norm_g,
              nsa_q_norm, nsa_k_norm, nsa_cmp_pos, nsa_cmp_w1, nsa_cmp_b1, nsa_cmp_w2,
              w_br_rw, w_br_ssm, w_br_nsa, w_out, w_ffn_in, w_ffn_out):
    params = dict(w_ada=w_ada, b_ada=b_ada, ln1=ln1, ln2=ln2, w_in=w_in,
                  rw_mu=rw_mu, rw_w0=rw_w0, rw_w2=rw_w2, rw_a0=rw_a0, rw_a2=rw_a2, rw_g2=rw_g2,
                  rw_k_k=rw_k_k, rw_k_a=rw_k_a, rw_r_k=rw_r_k, rw_lnx_g=rw_lnx_g, rw_lnx_b=rw_lnx_b,
                  ssm_conv_w=ssm_conv_w, ssm_conv_b=ssm_conv_b, ssm_dt_bias=ssm_dt_bias, ssm_a_log=ssm_a_log,
                  ssm_d=ssm_d, ssm_norm_g=ssm_norm_g, nsa_q_norm=nsa_q_norm, nsa_k_norm=nsa_k_norm,
                  nsa_cmp_pos=nsa_cmp_pos, nsa_cmp_w1=nsa_cmp_w1, nsa_cmp_b1=nsa_cmp_b1, nsa_cmp_w2=nsa_cmp_w2,
                  w_br_rw=w_br_rw, w_br_ssm=w_br_ssm, w_br_nsa=w_br_nsa, w_out=w_out,
                  w_ffn_in=w_ffn_in, w_ffn_out=w_ffn_out)
    bp = x_prompt.shape[0]
    dtp = x_prompt.dtype
    zero_shift = jnp.zeros((bp, RW_IN), dtp)
    zero_wkv = jnp.zeros((bp, RW_HEADS, RW_HD, RW_HD), dtp)
    zero_conv = jnp.zeros((bp, SSM_CONV - 1, SSM_XBC), dtp)
    zero_ssm = jnp.zeros((bp, SSM_HEADS, SSM_HD, SSM_STATE), dtp)
    xp, xs = x_prompt, x_sample
    st_p, st_s = [], []
    for l in range(DEPTH):
        p = {name: arr[l] for name, arr in params.items()}
        xp, sp_l = trunk_layer(xp, c_prompt, p, zero_shift, zero_wkv, zero_conv, zero_ssm, nsa_prompt)
        xs, ss_l = trunk_layer(xs, c_sample, p, state_rwkv_shift[l], state_rwkv_wkv[l], state_ssm_conv[l], state_ssm[l],
                               lambda proj, p_: nsa_sample(proj, p_, l, cache_cmp_kv, cache_sel_kv, cache_win_kv[l], page_table))
        st_p.append(sp_l)
        st_s.append(ss_l)
    sp = [jnp.stack([s[i] for s in st_p]) for i in range(7)]
    ss = [jnp.stack([s[i] for s in st_s]) for i in range(7)]
    return (xp, xs, sp[4], ss[4], sp[5], ss[5], sp[6], ss[6], sp[0], ss[0], sp[1], ss[1], sp[2], ss[2], sp[3], ss[3])
```

```python
import functools

import jax
import jax.numpy as jnp
from jax import lax
from jax.experimental import pallas as pl
from jax.experimental.pallas import tpu as pltpu

F32 = jnp.float32
BF16 = jnp.bfloat16

D_MODEL = 1024
PAST_LEN = 16384
PAGE_SIZE = 128
RW_HEADS = 16
RW_HD = 64
RW_DIM = RW_HEADS * RW_HD
RW_W_LORA = 64
RW_A_LORA = 64
RW_G_LORA = 128
RW_IN = 3 * RW_DIM + RW_W_LORA + RW_A_LORA + RW_G_LORA
RW_SPLITS = (RW_DIM, 2 * RW_DIM, 3 * RW_DIM, 3 * RW_DIM + RW_W_LORA, 3 * RW_DIM + RW_W_LORA + RW_A_LORA)
RW_GN_EPS = 64e-5
SSM_DIM = 2 * D_MODEL
SSM_HD = 64
SSM_HEADS = SSM_DIM // SSM_HD
SSM_GROUPS = 4
SSM_RPG = SSM_HEADS // SSM_GROUPS
SSM_STATE = 128
SSM_CONV = 4
SSM_CHUNK = 128
SSM_XBC = SSM_DIM + 2 * SSM_GROUPS * SSM_STATE
SSM_IN = SSM_DIM + SSM_XBC + SSM_HEADS
NSA_HEADS = 16
NSA_HD = 64
NSA_KV_HEADS = 4
NSA_RPG = NSA_HEADS // NSA_KV_HEADS
NSA_DIM = NSA_HEADS * NSA_HD
NSA_KV_DIM = NSA_KV_HEADS * NSA_HD
CMP_STRIDE = 16
CMP_BLOCK = 2 * CMP_STRIDE
CMP_HIDDEN = 128
SEL_BLOCK = 64
N_SELECT = 16
WINDOW = 512
Q_BLOCK = 64
SEL_FORCE = 1e6
NSA_IN = NSA_DIM + 6 * NSA_KV_DIM + 3 * NSA_HEADS
N_BRANCH = 3
FFN_HIDDEN = ((8 * D_MODEL // 3 + 255) // 256) * 256
EPS = 1e-6

LANE = 128
VMEM_LIMIT = 56 * 1024 * 1024


def _cparams(*sem):
    return pltpu.CompilerParams(dimension_semantics=sem, vmem_limit_bytes=VMEM_LIMIT)


def _modnorm(x, g, sc, sh):
    y = x * lax.rsqrt(jnp.mean(x * x, axis=-1, keepdims=True) + EPS)
    return (y * g) * (1.0 + sc) + sh


def _mod_spec(per_batch, tm, tiles_per_batch):
    if per_batch:
        return pl.BlockSpec((None, 1, D_MODEL), lambda i: (i // tiles_per_batch, 0, 0))
    return pl.BlockSpec((tm, D_MODEL), lambda i: (i, 0))


def _full_spec(shape):
    return pl.BlockSpec(shape, lambda i: (0,) * len(shape))


def _ada_kernel(c_ref, w_ref, b_ref, o_ref):
    o_ref[...] = jnp.dot(c_ref[...].astype(BF16), w_ref[...], preferred_element_type=F32) + b_ref[...]


def _ada(c, w_bf, b):
    n, tn = c.shape[0], 1536
    return pl.pallas_call(
        _ada_kernel,
        grid=(6 * D_MODEL // tn,),
        in_specs=[pl.BlockSpec((n, D_MODEL), lambda j: (0, 0)),
                  pl.BlockSpec((D_MODEL, tn), lambda j: (0, j)),
                  pl.BlockSpec((1, tn), lambda j: (0, j))],
        out_specs=pl.BlockSpec((n, tn), lambda j: (0, j)),
        out_shape=jax.ShapeDtypeStruct((n, 6 * D_MODEL), F32),
        compiler_params=_cparams("arbitrary"),
        name="ada_mod",
    )(c, w_bf, b.reshape(1, -1))


def _norm_proj_kernel(x_ref, sc_ref, sh_ref, g_ref, w_ref, *o_refs, splits):
    h = _modnorm(x_ref[...], g_ref[...], sc_ref[...], sh_ref[...]).astype(BF16)
    for o_ref, (a, b) in zip(o_refs, splits):
        o_ref[...] = jnp.dot(h, w_ref[:, a:b], preferred_element_type=F32)


def _norm_proj(x2, sc, sh, g, w_bf, widths, per_batch, tokens_per_batch, name):
    n = x2.shape[0]
    tm = min(512, n)
    splits, a = [], 0
    for wd in widths:
        splits.append((a, a + wd))
        a += wd
    assert a == w_bf.shape[1] and n % tm == 0
    tpb = max(tokens_per_batch // tm, 1)
    return pl.pallas_call(
        functools.partial(_norm_proj_kernel, splits=tuple(splits)),
        grid=(n // tm,),
        in_specs=[pl.BlockSpec((tm, D_MODEL), lambda i: (i, 0)),
                  _mod_spec(per_batch, tm, tpb), _mod_spec(per_batch, tm, tpb),
                  _full_spec((1, D_MODEL)), _full_spec(w_bf.shape)],
        out_specs=[pl.BlockSpec((tm, wd), lambda i: (i, 0)) for wd in widths],
        out_shape=[jax.ShapeDtypeStruct((n, wd), F32) for wd in widths],
        compiler_params=_cparams("arbitrary"),
        name=name,
    )(x2, sc, sh, g.reshape(1, -1), w_bf)


def _merge_kernel(x_ref, gt_ref, orw_ref, ossm_ref, onsa_ref, gate_ref, wrw_ref, wssm_ref, wnsa_ref, wout_ref,
                  o_ref):
    gate = jax.nn.sigmoid(gate_ref[...])
    br = (jnp.dot(orw_ref[...].astype(BF16), wrw_ref[...], preferred_element_type=F32),
          jnp.dot(ossm_ref[...].astype(BF16), wssm_ref[...], preferred_element_type=F32),
          jnp.dot(onsa_ref[...].astype(BF16), wnsa_ref[...], preferred_element_type=F32))
    merged = sum(gate[:, i * D_MODEL:(i + 1) * D_MODEL] * br[i] for i in range(N_BRANCH))
    y = jnp.dot(merged.astype(BF16), wout_ref[...], preferred_element_type=F32)
    o_ref[...] = x_ref[...] + gt_ref[...] * y


def _merge(x2, gt, o_rw, o_ssm, o_nsa, gate, w_rw, w_ssm, w_nsa, w_out, per_batch, tokens_per_batch):
    n = x2.shape[0]
    tm = min(512, n)
    tpb = max(tokens_per_batch // tm, 1)
    row = lambda wd: pl.BlockSpec((tm, wd), lambda i: (i, 0))
    return pl.pallas_call(
        _merge_kernel,
        grid=(n // tm,),
        in_specs=[row(D_MODEL), _mod_spec(per_batch, tm, tpb), row(RW_DIM), row(SSM_DIM), row(NSA_DIM),
                  row(N_BRANCH * D_MODEL), _full_spec(w_rw.shape), _full_spec(w_ssm.shape),
                  _full_spec(w_nsa.shape), _full_spec(w_out.shape)],
        out_specs=row(D_MODEL),
        out_shape=jax.ShapeDtypeStruct((n, D_MODEL), F32),
        compiler_params=_cparams("arbitrary"),
        name="branch_merge",
    )(x2, gt, o_rw, o_ssm, o_nsa, gate, w_rw, w_ssm, w_nsa, w_out)


FFN_CHUNK = 256


def _ffn_kernel(x_ref, sc_ref, sh_ref, gt_ref, g_ref, win_ref, wout_ref, o_ref):
    x = x_ref[...]
    h = _modnorm(x, g_ref[...], sc_ref[...], sh_ref[...]).astype(BF16)
    acc = jnp.zeros(x.shape, F32)
    for c in range(FFN_HIDDEN // FFN_CHUNK):
        a = c * FFN_CHUNK
        up = jnp.dot(h, win_ref[:, a:a + FFN_CHUNK], preferred_element_type=F32)
        gf = jnp.dot(h, win_ref[:, FFN_HIDDEN + a:FFN_HIDDEN + a + FFN_CHUNK], preferred_element_type=F32)
        act = (jax.nn.silu(gf) * up).astype(BF16)
        acc = acc + jnp.dot(act, wout_ref[a:a + FFN_CHUNK, :], preferred_element_type=F32)
    o_ref[...] = x + gt_ref[...] * acc


def _ffn(x2, sc, sh, gt, g, w_in, w_out, per_batch, tokens_per_batch):
    n = x2.shape[0]
    tm = min(512, n)
    tpb = max(tokens_per_batch // tm, 1)
    row = pl.BlockSpec((tm, D_MODEL), lambda i: (i, 0))
    mod = _mod_spec(per_batch, tm, tpb)
    return pl.pallas_call(
        _ffn_kernel,
        grid=(n // tm,),
        in_specs=[row, mod, mod, mod, _full_spec((1, D_MODEL)), _full_spec(w_in.shape), _full_spec(w_out.shape)],
        out_specs=row,
        out_shape=jax.ShapeDtypeStruct((n, D_MODEL), F32),
        compiler_params=_cparams("arbitrary"),
        name="ffn",
    )(x2, sc, sh, gt, g.reshape(1, -1), w_in, w_out)


def _rmsnorm(x, g):
    y = x * lax.rsqrt(jnp.mean(x * x, axis=-1, keepdims=True) + EPS)
    return y * g


def _masked_softmax(s, mask):
    s = jnp.where(mask, s, -jnp.inf)
    m = jnp.max(s, axis=-1, keepdims=True)
    m = jnp.where(jnp.isfinite(m), m, 0.0)
    e = jnp.exp(s - m)
    return e / jnp.maximum(jnp.sum(e, axis=-1, keepdims=True), jnp.finfo(F32).tiny)


def _rwkv7_mix(proj, shift_state, wkv_state, p):
    B, T, _ = proj.shape
    prev = jnp.concatenate([shift_state[:, None], proj[:, :-1]], axis=1)
    xs = proj + p['rw_mu'] * (prev - proj)
    r, k, v, wl, al, gl = jnp.split(xs, RW_SPLITS, axis=-1)
    w = -jax.nn.softplus(-(p['rw_w0'] + jnp.tanh(wl) @ p['rw_w2'])) - 0.5
    decay = jnp.exp(-jnp.exp(w))
    a = jax.nn.sigmoid(p['rw_a0'] + al @ p['rw_a2'])
    g = jax.nn.sigmoid(gl) @ p['rw_g2']
    kk = k * p['rw_k_k']
    k = k * (1.0 + (a - 1.0) * p['rw_k_a'])
    hd = lambda t: t.reshape(B, T, RW_HEADS, RW_HD)
    r, k, v, kk, a, decay = hd(r), hd(k), hd(v), hd(kk), hd(a), hd(decay)
    kk = kk * lax.rsqrt(jnp.sum(kk * kk, axis=-1, keepdims=True) + 1e-12)

    def step(S, inp):
        r_t, w_t, k_t, v_t, kk_t, a_t = inp
        sa = jnp.einsum('bhvk,bhk->bhv', S, kk_t)
        S = S * w_t[:, :, None, :] - sa[..., None] * (kk_t * a_t)[:, :, None, :] + v_t[..., None] * k_t[:, :, None, :]
        return S, jnp.einsum('bhvk,bhk->bhv', S, r_t)

    seqs = tuple(jnp.swapaxes(t, 0, 1) for t in (r, decay, k, v, kk, a))
    S_fin, o = lax.scan(step, wkv_state, seqs)
    o = jnp.swapaxes(o, 0, 1)
    mean = jnp.mean(o, axis=-1, keepdims=True)
    var = jnp.mean(jnp.square(o - mean), axis=-1, keepdims=True)
    o = ((o - mean) * lax.rsqrt(var + RW_GN_EPS)).reshape(B, T, RW_DIM) * p['rw_lnx_g'] + p['rw_lnx_b']
    bonus = jnp.sum(r * k * p['rw_r_k'], axis=-1, keepdims=True) * v
    o = (o + bonus.reshape(B, T, RW_DIM)) * g
    return o, proj[:, -1], S_fin


def _ssd_chunked(x, dt, A, Bm, Cm, s0):
    Bsz, T = x.shape[:2]
    L = SSM_CHUNK if T % SSM_CHUNK == 0 else T
    nc = T // L
    ch = lambda t: t.reshape((Bsz, nc, L) + t.shape[2:])
    xdt = ch(x * dt[..., None])
    a_cs = jnp.cumsum(ch(dt * A), axis=2)
    Bc, Cc = ch(Bm), ch(Cm)
    seg = a_cs[:, :, :, None] - a_cs[:, :, None, :]
    causal = jnp.tril(jnp.ones((L, L), dtype=bool))[:, :, None, None]
    decay_ls = jnp.exp(jnp.where(causal, seg, -jnp.inf))
    cb = jnp.einsum('bclgn,bcsgn->bclsg', Cc, Bc)
    y_diag = jnp.einsum('bclsgr,bcsgrp->bclgrp', cb[..., None] * decay_ls, xdt)
    decay_s = jnp.exp(a_cs[:, :, -1:] - a_cs)
    states = jnp.einsum('bclgn,bclgr,bclgrp->bcgrpn', Bc, decay_s, xdt)
    chunk_decay = jnp.exp(a_cs[:, :, -1])

    def step(s, inp):
        st, dec = inp
        return s * dec[..., None, None] + st, s

    s_fin, s_in = lax.scan(step, s0, (jnp.swapaxes(states, 0, 1), jnp.swapaxes(chunk_decay, 0, 1)))
    s_in = jnp.swapaxes(s_in, 0, 1)
    y_off = jnp.einsum('bclgn,bcgrpn,bclgr->bclgrp', Cc, s_in, jnp.exp(a_cs))
    return (y_diag + y_off).reshape(x.shape), s_fin


def _mamba2_mix(z, xbc, dt, conv_state, ssm_state, p):
    B, T, _ = z.shape
    xpad = jnp.concatenate([conv_state, xbc], axis=1)
    conv = p['ssm_conv_b'] + sum(xpad[:, i:i + T] * p['ssm_conv_w'][i] for i in range(SSM_CONV))
    xbc_c = jax.nn.silu(conv)
    xs, Bm, Cm = jnp.split(xbc_c, (SSM_DIM, SSM_DIM + SSM_GROUPS * SSM_STATE), axis=-1)
    x = xs.reshape(B, T, SSM_GROUPS, SSM_RPG, SSM_HD)
    Bm = Bm.reshape(B, T, SSM_GROUPS, SSM_STATE)
    Cm = Cm.reshape(B, T, SSM_GROUPS, SSM_STATE)
    dt = jax.nn.softplus(dt + p['ssm_dt_bias']).reshape(B, T, SSM_GROUPS, SSM_RPG)
    A = -jnp.exp(p['ssm_a_log']).reshape(SSM_GROUPS, SSM_RPG)
    s0 = ssm_state.reshape(B, SSM_GROUPS, SSM_RPG, SSM_HD, SSM_STATE)
    y, s_fin = _ssd_chunked(x, dt, A, Bm, Cm, s0)
    y = y + p['ssm_d'].reshape(SSM_GROUPS, SSM_RPG)[:, :, None] * x
    y = y.reshape(B, T, SSM_DIM) * jax.nn.silu(z)
    y = _rmsnorm(y, p['ssm_norm_g'])
    return y, xpad[:, -(SSM_CONV - 1):], s_fin.reshape(B, SSM_HEADS, SSM_HD, SSM_STATE)


def _nsa_project(q, kv, gates, p):
    B, T, _ = q.shape
    kc, vc, ks, vs, kw, vw = jnp.split(kv, 6, axis=-1)
    kvh = lambda t: t.reshape(B, T, NSA_KV_HEADS, NSA_HD)
    q = _rmsnorm(q.reshape(B, T, NSA_HEADS, NSA_HD), p['nsa_q_norm']).reshape(B, T, NSA_KV_HEADS, NSA_RPG, NSA_HD)
    cmp_rows = jnp.stack([kvh(kc), kvh(vc)], axis=2)
    sel_rows = jnp.stack([_rmsnorm(kvh(ks), p['nsa_k_norm'][1]), kvh(vs)], axis=2)
    win_rows = jnp.stack([_rmsnorm(kvh(kw), p['nsa_k_norm'][2]), kvh(vw)], axis=2)
    g = jax.nn.sigmoid(gates).reshape(B, T, 3, NSA_KV_HEADS, NSA_RPG, 1)
    return q, cmp_rows, sel_rows, win_rows, g


def _half_proj(rows, w1):
    B, T = rows.shape[:2]
    h = rows.reshape(B, T // CMP_STRIDE, CMP_STRIDE, 2, NSA_KV_HEADS, NSA_HD)
    top = jnp.einsum('bnsckd,csdh->bnckh', h, w1[:, :CMP_STRIDE])
    bot = jnp.einsum('bnsckd,csdh->bnckh', h, w1[:, CMP_STRIDE:])
    return top, bot


def _compress_finish(top, bot, p):
    w1 = p['nsa_cmp_w1']
    const = jnp.einsum('csd,csdh->ch', p['nsa_cmp_pos'], w1) + p['nsa_cmp_b1']
    hid = jax.nn.gelu(top[:, :-1] + bot[:, 1:] + const[:, None, :])
    kv = jnp.einsum('bnckh,chd->bnckd', hid, p['nsa_cmp_w2'])
    return _rmsnorm(kv[:, :, 0], p['nsa_k_norm'][0]), kv[:, :, 1]


def _cmp_to_sel(nc, ns):
    i = jnp.arange(nc)[:, None] * CMP_STRIDE
    j = jnp.arange(ns)[None, :] * SEL_BLOCK
    ov = jnp.minimum(i + CMP_BLOCK, j + SEL_BLOCK) - jnp.maximum(i, j)
    return jnp.clip(ov, 0, None).astype(F32) / CMP_BLOCK


def _fetch_rows(kv, pos):
    b = jnp.arange(kv.shape[0])[:, None, None, None]
    g = jnp.arange(NSA_KV_HEADS)[None, :, None, None]
    return kv[b, jnp.clip(pos, 0, kv.shape[1] - 1), :, g, :]


def _fetch_paged(pool, l, page_table, new_rows, pos):
    b = jnp.arange(page_table.shape[0])[:, None, None, None]
    g = jnp.arange(NSA_KV_HEADS)[None, :, None, None]
    pp = jnp.clip(pos, 0, PAST_LEN - 1)
    phys = page_table[b, pp // PAGE_SIZE]
    past = pool[l, phys, pp % PAGE_SIZE, :, g, :]
    new = _fetch_rows(new_rows, pos - PAST_LEN)
    return jnp.where((pos < PAST_LEN)[..., None, None], past, new)


def _nsa_cmp_sel(q, t, kc, vc, fetch, ns):
    B, Tq = q.shape[:2]
    scale = NSA_HD ** -0.5
    nc = kc.shape[1]
    s = jnp.einsum('bqgrd,bngd->bgrqn', q, kc) * scale
    cmp_end = jnp.arange(nc) * CMP_STRIDE + CMP_BLOCK - 1
    p_cmp = _masked_softmax(s, cmp_end[None, :] <= t[:, None])
    o_cmp = jnp.einsum('bgrqn,bngd->bqgrd', p_cmp, vc)
    imp = jnp.einsum('bgrqn,ns->bgqs', p_cmp, _cmp_to_sel(nc, ns))
    blk = jnp.arange(ns)[None, :]
    cur = t[:, None] // SEL_BLOCK
    forced = (blk == 0) | (blk == cur) | (blk == cur - 1)
    valid = blk * SEL_BLOCK <= t[:, None]
    score = jnp.where(forced, SEL_FORCE, jnp.where(valid, imp, -SEL_FORCE))
    n_top = min(N_SELECT, ns)
    _, idx = lax.top_k(score, n_top)
    pos = (idx[..., None] * SEL_BLOCK + jnp.arange(SEL_BLOCK)).reshape(B, NSA_KV_HEADS, Tq, n_top * SEL_BLOCK)
    kvs = fetch(pos)
    s2 = jnp.einsum('bqgrd,bgqkd->bgrqk', q, kvs[..., 0, :]) * scale
    p_sel = _masked_softmax(s2, (pos <= t[:, None])[:, :, None])
    o_sel = jnp.einsum('bgrqk,bgqkd->bqgrd', p_sel, kvs[..., 1, :])
    return o_cmp, o_sel


def _nsa_window(q, t, kw, vw, kpos):
    s = jnp.einsum('bqgrd,bngd->bgrqn', q, kw) * (NSA_HD ** -0.5)
    mask = (kpos[None, :] <= t[:, None]) & (kpos[None, :] > t[:, None] - WINDOW) & (kpos[None, :] >= 0)
    p = _masked_softmax(s, mask)
    return jnp.einsum('bgrqn,bngd->bqgrd', p, vw)


def _nsa_merge(o, g):
    B, T = o.shape[:2]
    return jnp.sum(g * o, axis=2).reshape(B, T, NSA_DIM)


def _nsa_prompt(q, kv, gates, p):
    q, cmp_rows, sel_rows, win_rows, g = _nsa_project(q, kv, gates, p)
    B, T = q.shape[:2]
    top, bot = _half_proj(cmp_rows, p['nsa_cmp_w1'])
    kc, vc = _compress_finish(top, bot, p)
    ns = -(-T // SEL_BLOCK)
    fetch = lambda pos: _fetch_rows(sel_rows, pos)
    win_pad = jnp.pad(win_rows, ((0, 0), (WINDOW, 0), (0, 0), (0, 0), (0, 0)))
    nqb = T // Q_BLOCK
    qb = jnp.swapaxes(q.reshape(B, nqb, Q_BLOCK, NSA_KV_HEADS, NSA_RPG, NSA_HD), 0, 1)

    def body(args):
        q_blk, i = args
        t = i * Q_BLOCK + jnp.arange(Q_BLOCK)
        o_cmp, o_sel = _nsa_cmp_sel(q_blk, t, kc, vc, fetch, ns)
        kv_w = lax.dynamic_slice_in_dim(win_pad, i * Q_BLOCK, WINDOW + Q_BLOCK, axis=1)
        kpos = i * Q_BLOCK - WINDOW + jnp.arange(WINDOW + Q_BLOCK)
        o_win = _nsa_window(q_blk, t, kv_w[:, :, 0], kv_w[:, :, 1], kpos)
        return jnp.stack([o_cmp, o_sel, o_win], axis=2)

    o = lax.map(body, (qb, jnp.arange(nqb)))
    o = jnp.swapaxes(o, 0, 1).reshape(B, T, 3, NSA_KV_HEADS, NSA_RPG, NSA_HD)
    return _nsa_merge(o, g), (cmp_rows, sel_rows, win_pad[:, -WINDOW:])


def _nsa_sample(q, kv, gates, p, l, cache_cmp_kv, cache_sel_kv, win_buf, page_table):
    q, cmp_rows, sel_rows, win_rows, g = _nsa_project(q, kv, gates, p)
    B, T = q.shape[:2]
    past_cmp = cache_cmp_kv[l, page_table].reshape(B, PAST_LEN, 2, NSA_KV_HEADS, NSA_HD)
    new_cmp = jnp.pad(cmp_rows, ((0, 0), (0, (-T) % CMP_STRIDE), (0, 0), (0, 0), (0, 0)))
    tp, bp = _half_proj(past_cmp, p['nsa_cmp_w1'])
    tn, bn = _half_proj(new_cmp, p['nsa_cmp_w1'])
    kc, vc = _compress_finish(jnp.concatenate([tp, tn], axis=1), jnp.concatenate([bp, bn], axis=1), p)
    t = PAST_LEN + jnp.arange(T)
    ns = -(-(PAST_LEN + T) // SEL_BLOCK)
    fetch = lambda pos: _fetch_paged(cache_sel_kv, l, page_table, sel_rows, pos)
    o_cmp, o_sel = _nsa_cmp_sel(q, t, kc, vc, fetch, ns)
    kv_w = jnp.concatenate([win_buf, win_rows], axis=1)
    kpos = PAST_LEN - WINDOW + jnp.arange(WINDOW + T)
    o_win = _nsa_window(q, t, kv_w[:, :, 0], kv_w[:, :, 1], kpos)
    o = jnp.stack([o_cmp, o_sel, o_win], axis=2)
    return _nsa_merge(o, g), (cmp_rows, sel_rows, kv_w[:, -WINDOW:])


def _prep_weights(p):
    w_in = p['w_in']
    o_ssm = RW_IN
    o_nsa = RW_IN + SSM_IN
    o_gate = o_nsa + NSA_IN
    pad = lambda w, n: jnp.pad(w, ((0, 0), (0, n - w.shape[1])))
    w_rw = w_in[:, :RW_IN]
    w_ssm = jnp.concatenate([w_in[:, o_ssm:o_ssm + SSM_DIM + SSM_XBC],
                             pad(w_in[:, o_ssm + SSM_DIM + SSM_XBC:o_nsa], LANE)], axis=1)
    w_nsa = jnp.concatenate([w_in[:, o_nsa:o_nsa + NSA_DIM + 6 * NSA_KV_DIM],
                             pad(w_in[:, o_nsa + NSA_DIM + 6 * NSA_KV_DIM:o_gate], LANE),
                             w_in[:, o_gate:]], axis=1)
    bf = lambda w: w.astype(BF16)
    return dict(w_ada=bf(p['w_ada']), w_rw=bf(w_rw), w_ssm=bf(w_ssm), w_nsa=bf(w_nsa),
                w_br_rw=bf(p['w_br_rw']), w_br_ssm=bf(p['w_br_ssm']), w_br_nsa=bf(p['w_br_nsa']),
                w_out=bf(p['w_out']), w_ffn_in=bf(p['w_ffn_in']), w_ffn_out=bf(p['w_ffn_out']))


def _trunk_layer(x, c, p, wb, rw_shift, rw_wkv, conv_state, ssm_state, nsa_fn, per_batch):
    B, T, _ = x.shape
    n = B * T
    mod = _ada(c, wb['w_ada'], p['b_ada'])
    if per_batch:
        mods = [m.reshape(B, 1, D_MODEL) for m in jnp.split(mod, 6, axis=-1)]
    else:
        mods = [jnp.repeat(m, T, axis=0) for m in jnp.split(mod, 6, axis=-1)]
    sh1, sc1, gt1, sh2, sc2, gt2 = mods
    x2 = x.reshape(n, D_MODEL)
    (p_rw,) = _norm_proj(x2, sc1, sh1, p['ln1'], wb['w_rw'], (RW_IN,), per_batch, T, "proj_rw")
    z, xbc, dt = _norm_proj(x2, sc1, sh1, p['ln1'], wb['w_ssm'], (SSM_DIM, SSM_XBC, LANE), per_batch, T, "proj_ssm")
    q, kv, gates, p_gate = _norm_proj(x2, sc1, sh1, p['ln1'], wb['w_nsa'],
                                      (NSA_DIM, 6 * NSA_KV_DIM, LANE, N_BRANCH * D_MODEL), per_batch, T, "proj_nsa")
    r3 = lambda t: t.reshape(B, T, t.shape[-1])
    o_rw, rw_shift, rw_wkv = _rwkv7_mix(r3(p_rw), rw_shift, rw_wkv, p)
    o_ssm, conv_state, ssm_state = _mamba2_mix(r3(z), r3(xbc), r3(dt)[..., :SSM_HEADS], conv_state, ssm_state, p)
    o_nsa, (cmp_rows, sel_rows, win_buf) = nsa_fn(r3(q), r3(kv), r3(gates)[..., :3 * NSA_HEADS], p)
    x2 = _merge(x2, gt1, o_rw.reshape(n, -1), o_ssm.reshape(n, -1), o_nsa.reshape(n, -1), p_gate,
                wb['w_br_rw'], wb['w_br_ssm'], wb['w_br_nsa'], wb['w_out'], per_batch, T)
    x2 = _ffn(x2, sc2, sh2, gt2, p['ln2'], wb['w_ffn_in'], wb['w_ffn_out'], per_batch, T)
    return x2.reshape(B, T, D_MODEL), (rw_shift, rw_wkv, conv_state, ssm_state, cmp_rows, sel_rows, win_buf)


def kernel(x_prompt, x_sample, cache_cmp_kv, cache_sel_kv, cache_win_kv, state_rwkv_shift, state_rwkv_wkv,
           state_ssm_conv, state_ssm, page_table, c_prompt, c_sample, w_ada, b_ada, ln1, ln2, w_in,
           rw_mu, rw_w0, rw_w2, rw_a0, rw_a2, rw_g2, rw_k_k, rw_k_a, rw_r_k, rw_lnx_g, rw_lnx_b,
           ssm_conv_w, ssm_conv_b, ssm_dt_bias, ssm_a_log, ssm_d, ssm_norm_g,
           nsa_q_norm, nsa_k_norm, nsa_cmp_pos, nsa_cmp_w1, nsa_cmp_b1, nsa_cmp_w2,
           w_br_rw, w_br_ssm, w_br_nsa, w_out, w_ffn_in, w_ffn_out):
    params = dict(w_ada=w_ada, b_ada=b_ada, ln1=ln1, ln2=ln2, w_in=w_in,
                  rw_mu=rw_mu, rw_w0=rw_w0, rw_w2=rw_w2, rw_a0=rw_a0, rw_a2=rw_a2, rw_g2=rw_g2,
                  rw_k_k=rw_k_k, rw_k_a=rw_k_a, rw_r_k=rw_r_k, rw_lnx_g=rw_lnx_g, rw_lnx_b=rw_lnx_b,
                  ssm_conv_w=ssm_conv_w, ssm_conv_b=ssm_conv_b, ssm_dt_bias=ssm_dt_bias, ssm_a_log=ssm_a_log,
                  ssm_d=ssm_d, ssm_norm_g=ssm_norm_g, nsa_q_norm=nsa_q_norm, nsa_k_norm=nsa_k_norm,
                  nsa_cmp_pos=nsa_cmp_pos, nsa_cmp_w1=nsa_cmp_w1, nsa_cmp_b1=nsa_cmp_b1, nsa_cmp_w2=nsa_cmp_w2,
                  w_br_rw=w_br_rw, w_br_ssm=w_br_ssm, w_br_nsa=w_br_nsa, w_out=w_out,
                  w_ffn_in=w_ffn_in, w_ffn_out=w_ffn_out)
    bp = x_prompt.shape[0]
    depth = w_in.shape[0]
    zero_shift = jnp.zeros((bp, RW_IN), F32)
    zero_wkv = jnp.zeros((bp, RW_HEADS, RW_HD, RW_HD), F32)
    zero_conv = jnp.zeros((bp, SSM_CONV - 1, SSM_XBC), F32)
    zero_ssm = jnp.zeros((bp, SSM_HEADS, SSM_HD, SSM_STATE), F32)
    xp, xs = x_prompt, x_sample
    st_p, st_s = [], []
    for l in range(depth):
        p = {name: arr[l] for name, arr in params.items()}
        wb = _prep_weights(p)
        xp, sp_l = _trunk_layer(xp, c_prompt, p, wb, zero_shift, zero_wkv, zero_conv, zero_ssm, _nsa_prompt, True)
        nsa_s = lambda q, kv, g, p_, l=l: _nsa_sample(q, kv, g, p_, l, cache_cmp_kv, cache_sel_kv, cache_win_kv[l],
                                                      page_table)
        xs, ss_l = _trunk_layer(xs, c_sample, p, wb, state_rwkv_shift[l], state_rwkv_wkv[l], state_ssm_conv[l],
                                state_ssm[l], nsa_s, False)
        st_p.append(sp_l)
        st_s.append(ss_l)
    sp = [jnp.stack([s[i] for s in st_p]) for i in range(7)]
    ss = [jnp.stack([s[i] for s in st_s]) for i in range(7)]
    return (xp, xs, sp[4], ss[4], sp[5], ss[5], sp[6], ss[6], sp[0], ss[0], sp[1], ss[1], sp[2], ss[2], sp[3], ss[3])
```

```python
import functools

import jax
import jax.numpy as jnp
from jax import lax
from jax.experimental import pallas as pl
from jax.experimental.pallas import tpu as pltpu

F32 = jnp.float32
BF16 = jnp.bfloat16

D_MODEL = 1024
PAST_LEN = 16384
PAGE_SIZE = 128
RW_HEADS = 16
RW_HD = 64
RW_DIM = RW_HEADS * RW_HD
RW_W_LORA = 64
RW_A_LORA = 64
RW_G_LORA = 128
RW_IN = 3 * RW_DIM + RW_W_LORA + RW_A_LORA + RW_G_LORA
RW_SPLITS = (RW_DIM, 2 * RW_DIM, 3 * RW_DIM, 3 * RW_DIM + RW_W_LORA, 3 * RW_DIM + RW_W_LORA + RW_A_LORA)
RW_GN_EPS = 64e-5
SSM_DIM = 2 * D_MODEL
SSM_HD = 64
SSM_HEADS = SSM_DIM // SSM_HD
SSM_GROUPS = 4
SSM_RPG = SSM_HEADS // SSM_GROUPS
SSM_STATE = 128
SSM_CONV = 4
SSM_CHUNK = 128
SSM_XBC = SSM_DIM + 2 * SSM_GROUPS * SSM_STATE
SSM_IN = SSM_DIM + SSM_XBC + SSM_HEADS
NSA_HEADS = 16
NSA_HD = 64
NSA_KV_HEADS = 4
NSA_RPG = NSA_HEADS // NSA_KV_HEADS
NSA_DIM = NSA_HEADS * NSA_HD
NSA_KV_DIM = NSA_KV_HEADS * NSA_HD
CMP_STRIDE = 16
CMP_BLOCK = 2 * CMP_STRIDE
CMP_HIDDEN = 128
SEL_BLOCK = 64
N_SELECT = 16
WINDOW = 512
Q_BLOCK = 64
SEL_FORCE = 1e6
NSA_IN = NSA_DIM + 6 * NSA_KV_DIM + 3 * NSA_HEADS
N_BRANCH = 3
FFN_HIDDEN = ((8 * D_MODEL // 3 + 255) // 256) * 256
EPS = 1e-6

LANE = 128
VMEM_LIMIT = 56 * 1024 * 1024


def _cparams(*sem):
    return pltpu.CompilerParams(dimension_semantics=sem, vmem_limit_bytes=VMEM_LIMIT)


def _modnorm(x, g, sc, sh):
    y = x * lax.rsqrt(jnp.mean(x * x, axis=-1, keepdims=True) + EPS)
    return (y * g) * (1.0 + sc) + sh


def _mod_spec(per_batch, tm, tiles_per_batch):
    if per_batch:
        return pl.BlockSpec((None, 1, D_MODEL), lambda i: (i // tiles_per_batch, 0, 0))
    return pl.BlockSpec((tm, D_MODEL), lambda i: (i, 0))


def _full_spec(shape):
    return pl.BlockSpec(shape, lambda i: (0,) * len(shape))


def _ada_kernel(c_ref, w_ref, b_ref, o_ref):
    o_ref[...] = jnp.dot(c_ref[...].astype(BF16), w_ref[...], preferred_element_type=F32) + b_ref[...]


def _ada(c, w_bf, b):
    n, tn = c.shape[0], 1536
    return pl.pallas_call(
        _ada_kernel,
        grid=(6 * D_MODEL // tn,),
        in_specs=[pl.BlockSpec((n, D_MODEL), lambda j: (0, 0)),
                  pl.BlockSpec((D_MODEL, tn), lambda j: (0, j)),
                  pl.BlockSpec((1, tn), lambda j: (0, j))],
        out_specs=pl.BlockSpec((n, tn), lambda j: (0, j)),
        out_shape=jax.ShapeDtypeStruct((n, 6 * D_MODEL), F32),
        compiler_params=_cparams("arbitrary"),
        name="ada_mod",
    )(c, w_bf, b.reshape(1, -1))


def _norm_proj_kernel(x_ref, sc_ref, sh_ref, g_ref, w_ref, *o_refs, splits):
    h = _modnorm(x_ref[...], g_ref[...], sc_ref[...], sh_ref[...]).astype(BF16)
    for o_ref, (a, b) in zip(o_refs, splits):
        o_ref[...] = jnp.dot(h, w_ref[:, a:b], preferred_element_type=F32)


def _norm_proj(x2, sc, sh, g, w_bf, widths, per_batch, tokens_per_batch, name):
    n = x2.shape[0]
    tm = min(512, n)
    splits, a = [], 0
    for wd in widths:
        splits.append((a, a + wd))
        a += wd
    assert a == w_bf.shape[1] and n % tm == 0
    tpb = max(tokens_per_batch // tm, 1)
    return pl.pallas_call(
        functools.partial(_norm_proj_kernel, splits=tuple(splits)),
        grid=(n // tm,),
        in_specs=[pl.BlockSpec((tm, D_MODEL), lambda i: (i, 0)),
                  _mod_spec(per_batch, tm, tpb), _mod_spec(per_batch, tm, tpb),
                  _full_spec((1, D_MODEL)), _full_spec(w_bf.shape)],
        out_specs=[pl.BlockSpec((tm, wd), lambda i: (i, 0)) for wd in widths],
        out_shape=[jax.ShapeDtypeStruct((n, wd), F32) for wd in widths],
        compiler_params=_cparams("arbitrary"),
        name=name,
    )(x2, sc, sh, g.reshape(1, -1), w_bf)


def _merge_kernel(x_ref, gt_ref, orw_ref, ossm_ref, onsa_ref, gate_ref, wrw_ref, wssm_ref, wnsa_ref, wout_ref,
                  o_ref):
    gate = jax.nn.sigmoid(gate_ref[...])
    br = (jnp.dot(orw_ref[...].astype(BF16), wrw_ref[...], preferred_element_type=F32),
          jnp.dot(ossm_ref[...].astype(BF16), wssm_ref[...], preferred_element_type=F32),
          jnp.dot(onsa_ref[...].astype(BF16), wnsa_ref[...], preferred_element_type=F32))
    merged = sum(gate[:, i * D_MODEL:(i + 1) * D_MODEL] * br[i] for i in range(N_BRANCH))
    y = jnp.dot(merged.astype(BF16), wout_ref[...], preferred_element_type=F32)
    o_ref[...] = x_ref[...] + gt_ref[...] * y


def _merge(x2, gt, o_rw, o_ssm, o_nsa, gate, w_rw, w_ssm, w_nsa, w_out, per_batch, tokens_per_batch):
    n = x2.shape[0]
    tm = min(512, n)
    tpb = max(tokens_per_batch // tm, 1)
    row = lambda wd: pl.BlockSpec((tm, wd), lambda i: (i, 0))
    return pl.pallas_call(
        _merge_kernel,
        grid=(n // tm,),
        in_specs=[row(D_MODEL), _mod_spec(per_batch, tm, tpb), row(RW_DIM), row(SSM_DIM), row(NSA_DIM),
                  row(N_BRANCH * D_MODEL), _full_spec(w_rw.shape), _full_spec(w_ssm.shape),
                  _full_spec(w_nsa.shape), _full_spec(w_out.shape)],
        out_specs=row(D_MODEL),
        out_shape=jax.ShapeDtypeStruct((n, D_MODEL), F32),
        compiler_params=_cparams("arbitrary"),
        name="branch_merge",
    )(x2, gt, o_rw, o_ssm, o_nsa, gate, w_rw, w_ssm, w_nsa, w_out)


FFN_CHUNK = 256


def _ffn_kernel(x_ref, sc_ref, sh_ref, gt_ref, g_ref, win_ref, wout_ref, o_ref):
    x = x_ref[...]
    h = _modnorm(x, g_ref[...], sc_ref[...], sh_ref[...]).astype(BF16)
    acc = jnp.zeros(x.shape, F32)
    for c in range(FFN_HIDDEN // FFN_CHUNK):
        a = c * FFN_CHUNK
        up = jnp.dot(h, win_ref[:, a:a + FFN_CHUNK], preferred_element_type=F32)
        gf = jnp.dot(h, win_ref[:, FFN_HIDDEN + a:FFN_HIDDEN + a + FFN_CHUNK], preferred_element_type=F32)
        act = (jax.nn.silu(gf) * up).astype(BF16)
        acc = acc + jnp.dot(act, wout_ref[a:a + FFN_CHUNK, :], preferred_element_type=F32)
    o_ref[...] = x + gt_ref[...] * acc


def _ffn(x2, sc, sh, gt, g, w_in, w_out, per_batch, tokens_per_batch):
    n = x2.shape[0]
    tm = min(512, n)
    tpb = max(tokens_per_batch // tm, 1)
    row = pl.BlockSpec((tm, D_MODEL), lambda i: (i, 0))
    mod = _mod_spec(per_batch, tm, tpb)
    return pl.pallas_call(
        _ffn_kernel,
        grid=(n // tm,),
        in_specs=[row, mod, mod, mod, _full_spec((1, D_MODEL)), _full_spec(w_in.shape), _full_spec(w_out.shape)],
        out_specs=row,
        out_shape=jax.ShapeDtypeStruct((n, D_MODEL), F32),
        compiler_params=_cparams("arbitrary"),
        name="ffn",
    )(x2, sc, sh, gt, g.reshape(1, -1), w_in, w_out)


HIGHEST = lax.Precision.HIGHEST
NT_DIMS = (((1,), (1,)), ((), ()))
NEG_BIG = -1e30
F32_TINY = float(jnp.finfo(jnp.float32).tiny)


def _seg_rmsnorm(x, seg, seg_t, gain):
    ss = jnp.dot(x * x, seg, precision=HIGHEST, preferred_element_type=F32)
    inv = lax.rsqrt(ss * (1.0 / NSA_HD) + EPS)
    return x * jnp.dot(inv, seg_t, precision=HIGHEST, preferred_element_type=F32) * gain


def _nsa_prep_kernel(q_ref, sel_ref, win_ref, seg_ref, segt_ref, qg_ref, kg_ref,
                     qh_ref, ksh_ref, vsh_ref, kwh_ref, vwh_ref, selo_ref, wino_ref):
    seg, segt = seg_ref[...], segt_ref[...]
    qn = _seg_rmsnorm(q_ref[...], seg, segt, qg_ref[...]) * (NSA_HD ** -0.5)
    for h in range(NSA_HEADS):
        qh_ref[h] = qn[:, h * NSA_HD:(h + 1) * NSA_HD]
    kg = kg_ref[...]
    for src, dst, kh_ref, vh_ref, row in ((sel_ref, selo_ref, ksh_ref, vsh_ref, 1), (win_ref, wino_ref, kwh_ref, vwh_ref, 2)):
        x = src[...]
        kn = _seg_rmsnorm(x[:, :NSA_KV_DIM], seg[:NSA_KV_DIM], segt[:, :NSA_KV_DIM], kg[row:row + 1])
        v = x[:, NSA_KV_DIM:]
        dst[:, :NSA_KV_DIM] = kn
        dst[:, NSA_KV_DIM:] = v
        for g in range(NSA_KV_HEADS):
            kh_ref[g] = kn[:, g * NSA_HD:(g + 1) * NSA_HD].astype(BF16)
            vh_ref[g] = v[:, g * NSA_HD:(g + 1) * NSA_HD].astype(BF16)


def _seg_mats():
    c = jnp.arange(NSA_DIM)[:, None] // NSA_HD
    seg = (c == jnp.arange(LANE)[None, :]).astype(F32)
    return seg, seg.T


def _nsa_prep(q2, sel2, win2, q_norm, k_norm, B, T):
    tm = min(512, T)
    tpb = T // tm
    seg, segt = _seg_mats()
    row = lambda wd: pl.BlockSpec((tm, wd), lambda i: (i, 0))
    hm = lambda nh: pl.BlockSpec((None, nh, tm, NSA_HD), lambda i: (i // tpb, 0, i % tpb, 0))
    kvs = jax.ShapeDtypeStruct((B, NSA_KV_HEADS, T, NSA_HD), BF16)
    rows = jax.ShapeDtypeStruct((B * T, 2 * NSA_KV_DIM), F32)
    return pl.pallas_call(
        _nsa_prep_kernel,
        grid=(B * T // tm,),
        in_specs=[row(NSA_DIM), row(2 * NSA_KV_DIM), row(2 * NSA_KV_DIM), _full_spec(seg.shape), _full_spec(segt.shape),
                  _full_spec((1, NSA_DIM)), _full_spec((3, NSA_KV_DIM))],
        out_specs=[hm(NSA_HEADS), hm(NSA_KV_HEADS), hm(NSA_KV_HEADS), hm(NSA_KV_HEADS), hm(NSA_KV_HEADS),
                   row(2 * NSA_KV_DIM), row(2 * NSA_KV_DIM)],
        out_shape=[jax.ShapeDtypeStruct((B, NSA_HEADS, T, NSA_HD), F32), kvs, kvs, kvs, kvs, rows, rows],
        compiler_params=_cparams("arbitrary"),
        name="nsa_prep",
    )(q2, sel2, win2, seg, segt, jnp.tile(q_norm, NSA_HEADS).reshape(1, -1), jnp.tile(k_norm, (1, NSA_KV_HEADS)))


def _nsa_cmp_kernel(xk_ref, xv_ref, w1_ref, pos_ref, b1_ref, w2_ref, kn_ref, kc_ref, vc_ref, *, n_valid):
    half = CMP_STRIDE * NSA_HD
    for c, x_ref in ((0, xk_ref), (1, xv_ref)):
        x = x_ref[...]
        const = jnp.dot(pos_ref[c].astype(BF16), w1_ref[c], preferred_element_type=F32)[0:1] + b1_ref[c:c + 1]
        top = jnp.dot(x, w1_ref[c, :half, :], preferred_element_type=F32)
        bot = jnp.dot(x, w1_ref[c, half:, :], preferred_element_type=F32)
        bot_next = jnp.concatenate([bot[1:], jnp.zeros((1, CMP_HIDDEN), F32)], axis=0)
        hid = jax.nn.gelu(top + bot_next + const)
        kv = jnp.dot(hid.astype(BF16), w2_ref[c], preferred_element_type=F32)
        if c == 0:
            kv = kv * lax.rsqrt(jnp.mean(kv * kv, axis=-1, keepdims=True) + EPS) * kn_ref[...]
            kc_ref[...] = kv
        else:
            vc_ref[...] = kv.astype(BF16)


def _nsa_compress(xh, w1_bf, pos, b1, w2_bf, k_norm0):
    B, _, nh, _ = xh.shape
    blk = lambda off: pl.BlockSpec((None, None, nh, CMP_STRIDE * NSA_HD), lambda b, g: (b, off + g, 0, 0))
    full = lambda shape: pl.BlockSpec(shape, lambda b, g: (0,) * len(shape))
    out = pl.BlockSpec((None, None, nh, NSA_HD), lambda b, g: (b, g, 0, 0))
    return pl.pallas_call(
        functools.partial(_nsa_cmp_kernel, n_valid=nh - 1),
        grid=(B, NSA_KV_HEADS),
        in_specs=[blk(0), blk(NSA_KV_HEADS), full(w1_bf.shape), full((2, 8, CMP_BLOCK * NSA_HD)), full((2, CMP_HIDDEN)),
                  full(w2_bf.shape), full((1, NSA_HD))],
        out_specs=[out, out],
        out_shape=[jax.ShapeDtypeStruct((B, NSA_KV_HEADS, nh, NSA_HD), F32),
                   jax.ShapeDtypeStruct((B, NSA_KV_HEADS, nh, NSA_HD), BF16)],
        compiler_params=_cparams("arbitrary", "arbitrary"),
        name="nsa_compress",
    )(xh, xh, w1_bf, jnp.broadcast_to(pos.reshape(2, 1, -1), (2, 8, CMP_BLOCK * NSA_HD)), b1, w2_bf, k_norm0.reshape(1, -1))


NSA_TQ = 128
NSA_TK = 512


def _softmax_direct(s, mask):
    sm = jnp.where(mask[None], s, NEG_BIG)
    m = jnp.max(sm, axis=-1, keepdims=True)
    e = jnp.where(mask[None], jnp.exp(sm - m), 0.0)
    return e / jnp.maximum(jnp.sum(e, axis=-1, keepdims=True), F32_TINY)


def _top_blocks(score, n_top):
    ns = score.shape[-1]
    lane = lax.broadcasted_iota(jnp.int32, score.shape, 1)
    sel = jnp.zeros(score.shape, F32)
    for _ in range(n_top):
        m = jnp.max(score, axis=-1, keepdims=True)
        first = jnp.min(jnp.where(score == m, lane, ns), axis=-1, keepdims=True)
        hit = lane == first
        sel = jnp.where(hit, 1.0, sel)
        score = jnp.where(hit, -jnp.inf, score)
    return sel


def _nsa_prompt_kernel(q_ref, kc_ref, vc_ref, c2s_ref, ks_ref, vs_ref, kw_ref, vw_ref, g_ref, o_ref, selexp_ref):
    R, TQ, TK = NSA_RPG, NSA_TQ, NSA_TK
    T = ks_ref.shape[0]
    nc = kc_ref.shape[0]
    i = pl.program_id(2)
    q0 = i * TQ
    q = q_ref[...].reshape(R * TQ, NSA_HD)
    qb = q.astype(BF16)
    t_row = q0 + lax.broadcasted_iota(jnp.int32, (TQ, 1), 0)

    s = lax.dot_general(q, kc_ref[...], NT_DIMS, precision=HIGHEST, preferred_element_type=F32).reshape(R, TQ, nc)
    cmp_end = lax.broadcasted_iota(jnp.int32, (1, nc), 1) * CMP_STRIDE + (CMP_BLOCK - 1)
    p = _softmax_direct(s, cmp_end <= t_row)
    o_cmp = jnp.dot(p.reshape(R * TQ, nc).astype(BF16), vc_ref[...], preferred_element_type=F32).reshape(R, TQ, NSA_HD)
    imp = jnp.dot(jnp.sum(p, axis=0), c2s_ref[...], precision=HIGHEST, preferred_element_type=F32)
    ns = imp.shape[-1]
    blk = lax.broadcasted_iota(jnp.int32, (1, ns), 1)
    cur = t_row // SEL_BLOCK
    forced = (blk == 0) | (blk == cur) | (blk == cur - 1)
    score = jnp.where(forced, SEL_FORCE, jnp.where(blk * SEL_BLOCK <= t_row, imp, -SEL_FORCE))
    sel = _top_blocks(score, min(N_SELECT, ns)).astype(BF16)
    per_tile = TK // SEL_BLOCK
    key_blk = lax.broadcasted_iota(jnp.int32, (ns, TK), 1) // SEL_BLOCK
    row_blk = lax.broadcasted_iota(jnp.int32, (ns, TK), 0)
    for j in range(T // TK):
        expand = (key_blk + j * per_tile == row_blk).astype(BF16)
        selexp_ref[j] = jnp.dot(sel, expand, preferred_element_type=F32)

    def body(j, carry):
        m, l, acc = carry
        start = pl.multiple_of(j * TK, TK)
        k = ks_ref[pl.ds(start, TK), :]
        v = vs_ref[pl.ds(start, TK), :]
        sj = lax.dot_general(qb, k, NT_DIMS, preferred_element_type=F32).reshape(R, TQ, TK)
        kpos = start + lax.broadcasted_iota(jnp.int32, (1, TK), 1)
        mask = (selexp_ref[j] > 0.5) & (kpos <= t_row)
        sm = jnp.where(mask[None], sj, NEG_BIG)
        m_new = jnp.maximum(m, jnp.max(sm, axis=-1, keepdims=True))
        alpha = jnp.exp(m - m_new)
        pj = jnp.where(mask[None], jnp.exp(sm - m_new), 0.0)
        l = alpha * l + jnp.sum(pj, axis=-1, keepdims=True)
        pv = jnp.dot(pj.reshape(R * TQ, TK).astype(BF16), v, preferred_element_type=F32).reshape(R, TQ, NSA_HD)
        return m_new, l, alpha * acc + pv

    carry0 = (jnp.full((R, TQ, 1), NEG_BIG, F32), jnp.zeros((R, TQ, 1), F32), jnp.zeros((R, TQ, NSA_HD), F32))
    _, l, acc = lax.fori_loop(0, (q0 + TQ + TK - 1) // TK, body, carry0)
    o_sel = acc / jnp.maximum(l, F32_TINY)

    wk = WINDOW + TQ
    ws = pl.multiple_of(jnp.maximum(q0 - WINDOW, 0), TQ)
    kw = kw_ref[pl.ds(ws, wk), :]
    vw = vw_ref[pl.ds(ws, wk), :]
    sw = lax.dot_general(qb, kw, NT_DIMS, preferred_element_type=F32).reshape(R, TQ, wk)
    kpos = ws + lax.broadcasted_iota(jnp.int32, (1, wk), 1)
    pw = _softmax_direct(sw, (kpos <= t_row) & (kpos > t_row - WINDOW))
    o_win = jnp.dot(pw.reshape(R * TQ, wk).astype(BF16), vw, preferred_element_type=F32).reshape(R, TQ, NSA_HD)

    gs = jax.nn.sigmoid(g_ref[...])
    for r in range(R):
        o_ref[:, r * NSA_HD:(r + 1) * NSA_HD] = (gs[:, r:r + 1] * o_cmp[r] + gs[:, R + r:R + r + 1] * o_sel[r]
                                                 + gs[:, 2 * R + r:2 * R + r + 1] * o_win[r])


def _cmp_to_sel_padded(nh, ns):
    i = jnp.arange(nh)[:, None] * CMP_STRIDE
    j = jnp.arange(ns)[None, :] * SEL_BLOCK
    ov = jnp.minimum(i + CMP_BLOCK, j + SEL_BLOCK) - jnp.maximum(i, j)
    return jnp.clip(ov, 0, None).astype(F32) / CMP_BLOCK


def _nsa_prompt_attn(qh, kc, vc, ksh, vsh, kwh, vwh, gates4):
    B, _, T, _ = qh.shape
    nh = kc.shape[2]
    ns = T // SEL_BLOCK
    c2s = _cmp_to_sel_padded(nh, ns)
    kv = lambda rows: pl.BlockSpec((None, None, rows, NSA_HD), lambda b, g, i: (b, g, 0, 0))
    return pl.pallas_call(
        _nsa_prompt_kernel,
        grid=(B, NSA_KV_HEADS, T // NSA_TQ),
        in_specs=[pl.BlockSpec((None, NSA_RPG, NSA_TQ, NSA_HD), lambda b, g, i: (b, g, i, 0)),
                  kv(nh), kv(nh), pl.BlockSpec(c2s.shape, lambda b, g, i: (0, 0)),
                  kv(T), kv(T), kv(T), kv(T),
                  pl.BlockSpec((None, None, NSA_TQ, 16), lambda b, g, i: (b, g, i, 0))],
        out_specs=pl.BlockSpec((None, NSA_TQ, NSA_RPG * NSA_HD), lambda b, g, i: (b, i, g)),
        out_shape=jax.ShapeDtypeStruct((B, T, NSA_DIM), F32),
        scratch_shapes=[pltpu.VMEM((T // NSA_TK, NSA_TQ, NSA_TK), F32)],
        compiler_params=_cparams("arbitrary", "arbitrary", "arbitrary"),
        name="nsa_prompt_attn",
    )(qh, kc, vc, c2s, ksh, vsh, kwh, vwh, gates4)


def _nsa_prompt_pallas(q2, cmp2, sel2, win2, gates2, p, wb, B, T):
    qh, ksh, vsh, kwh, vwh, sel_rows, win_rows = _nsa_prep(q2, sel2, win2, p['nsa_q_norm'], p['nsa_k_norm'], B, T)
    nh = T // CMP_STRIDE
    xh = cmp2.astype(BF16).reshape(B, nh, CMP_STRIDE, 2 * NSA_KV_HEADS, NSA_HD)
    xh = xh.transpose(0, 3, 1, 2, 4).reshape(B, 2 * NSA_KV_HEADS, nh, CMP_STRIDE * NSA_HD)
    kc, vc = _nsa_compress(xh, wb['cmp_w1'], p['nsa_cmp_pos'], p['nsa_cmp_b1'], wb['cmp_w2'], p['nsa_k_norm'][0])
    gates4 = gates2[:, :NSA_KV_HEADS * 16].reshape(B, T, NSA_KV_HEADS, 16).transpose(0, 2, 1, 3)
    o = _nsa_prompt_attn(qh, kc, vc, ksh, vsh, kwh, vwh, gates4)
    shp = (B, T, 2, NSA_KV_HEADS, NSA_HD)
    return o, (cmp2.reshape(shp), sel_rows.reshape(shp), win_rows.reshape(shp)[:, -WINDOW:])


HALF_COLS = CMP_STRIDE * 2 * NSA_KV_DIM
PAGE_HALVES = PAGE_SIZE // CMP_STRIDE
N_PAIR = 2 * NSA_KV_HEADS // 2
CMP_PAGES = 32
SEL_PAGES = 8


def _half_proj_body(x, w_ref, o_ref):
    for pr in range(N_PAIR):
        lhs = jnp.concatenate([x[:, s * 2 * NSA_KV_DIM + pr * LANE:s * 2 * NSA_KV_DIM + (pr + 1) * LANE]
                               for s in range(CMP_STRIDE)], axis=1)
        o_ref[:, pr * 4 * CMP_HIDDEN:(pr + 1) * 4 * CMP_HIDDEN] = jnp.dot(lhs, w_ref[pr], preferred_element_type=F32)


def _half_proj_paged_kernel(pt_ref, *refs):
    pages, (w_ref, o_ref) = refs[:CMP_PAGES], refs[CMP_PAGES:]
    x = jnp.concatenate([pg[...].astype(BF16) for pg in pages], axis=0)
    _half_proj_body(x, w_ref, o_ref)


def _half_proj_rows_kernel(x_ref, w_ref, o_ref):
    _half_proj_body(x_ref[...].astype(BF16), w_ref, o_ref)


def _pair_weights(w1):
    top, bot = w1[:, :CMP_STRIDE], w1[:, CMP_STRIDE:]
    tb = jnp.concatenate([top, bot], axis=-1)
    z = jnp.zeros_like(tb)
    par0 = jnp.concatenate([tb, z], axis=-1)
    par1 = jnp.concatenate([z, tb], axis=-1)
    w = jnp.stack([par0, par1], axis=2)
    w = w.reshape(2, CMP_STRIDE * 2 * NSA_HD, 4 * CMP_HIDDEN)
    return jnp.stack([w[0], w[0], w[1], w[1]], axis=0).astype(BF16)


def _half_proj_paged(pool4, layer, page_table, w_pair):
    B, n_pages = page_table.shape
    m = CMP_PAGES * PAGE_HALVES
    page_spec = lambda n: pl.BlockSpec((None, None, PAGE_HALVES, HALF_COLS),
                                       lambda b, j, pt: (layer, pt[b, j * CMP_PAGES + n], 0, 0))
    grid_spec = pltpu.PrefetchScalarGridSpec(
        num_scalar_prefetch=1, grid=(B, n_pages // CMP_PAGES),
        in_specs=[page_spec(n) for n in range(CMP_PAGES)] + [pl.BlockSpec(w_pair.shape, lambda b, j, pt: (0, 0, 0))],
        out_specs=pl.BlockSpec((None, m, 8 * 2 * CMP_HIDDEN), lambda b, j, pt: (b, j, 0)))
    return pl.pallas_call(
        _half_proj_paged_kernel, grid_spec=grid_spec,
        out_shape=jax.ShapeDtypeStruct((B, n_pages * PAGE_HALVES, 8 * 2 * CMP_HIDDEN), F32),
        compiler_params=_cparams("arbitrary", "arbitrary"),
        name="nsa_half_proj_paged",
    )(page_table, *([pool4] * CMP_PAGES), w_pair)


def _half_proj_rows(x, w_pair):
    m = x.shape[0]
    return pl.pallas_call(
        _half_proj_rows_kernel, grid=(1,),
        in_specs=[_full_spec(x.shape), _full_spec(w_pair.shape)],
        out_specs=_full_spec((m, 8 * 2 * CMP_HIDDEN)),
        out_shape=jax.ShapeDtypeStruct((m, 8 * 2 * CMP_HIDDEN), F32),
        compiler_params=_cparams("arbitrary"),
        name="nsa_half_proj_new",
    )(x, w_pair)


def _cmp_finish_kernel(tk_ref, tv_ref, nk_ref, nv_ref, w1_ref, pos_ref, b1_ref, w2_ref, kn_ref, kc_ref, vc_ref):
    for c, t_ref, n_ref in ((0, tk_ref, nk_ref), (1, tv_ref, nv_ref)):
        const = jnp.dot(pos_ref[c].astype(BF16), w1_ref[c], preferred_element_type=F32)[0:1] + b1_ref[c:c + 1]
        tb = t_ref[...]
        top = tb[:, :CMP_HIDDEN]
        bot_next = jnp.concatenate([tb[1:, CMP_HIDDEN:], n_ref[0:1, CMP_HIDDEN:]], axis=0)
        hid = jax.nn.gelu(top + bot_next + const)
        kv = jnp.dot(hid.astype(BF16), w2_ref[c], preferred_element_type=F32)
        if c == 0:
            kc_ref[...] = kv * lax.rsqrt(jnp.mean(kv * kv, axis=-1, keepdims=True) + EPS) * kn_ref[...]
        else:
            vc_ref[...] = kv.astype(BF16)


def _cmp_finish(tb_past, tb_new, w1_bf, pos, b1, w2_bf, k_norm0):
    B, nh, _ = tb_past.shape
    wd = 2 * CMP_HIDDEN
    full = lambda shape: pl.BlockSpec(shape, lambda b, g: (0,) * len(shape))
    past = lambda off: pl.BlockSpec((None, nh, wd), lambda b, g: (b, 0, off + g))
    new = lambda off: pl.BlockSpec((None, 8, wd), lambda b, g: (b, 0, off + g))
    out = pl.BlockSpec((None, None, nh, NSA_HD), lambda b, g: (b, g, 0, 0))
    return pl.pallas_call(
        _cmp_finish_kernel, grid=(B, NSA_KV_HEADS),
        in_specs=[past(0), past(NSA_KV_HEADS), new(0), new(NSA_KV_HEADS), full(w1_bf.shape),
                  full((2, 8, CMP_BLOCK * NSA_HD)), full((2, CMP_HIDDEN)), full(w2_bf.shape), full((1, NSA_HD))],
        out_specs=[out, out],
        out_shape=[jax.ShapeDtypeStruct((B, NSA_KV_HEADS, nh, NSA_HD), F32),
                   jax.ShapeDtypeStruct((B, NSA_KV_HEADS, nh, NSA_HD), BF16)],
        compiler_params=_cparams("arbitrary", "arbitrary"),
        name="nsa_cmp_finish",
    )(tb_past, tb_past, tb_new, tb_new, w1_bf, jnp.broadcast_to(pos.reshape(2, 1, -1), (2, 8, CMP_BLOCK * NSA_HD)),
      b1, w2_bf, k_norm0.reshape(1, -1))


def _nsa_sample_kernel(pt_ref, *refs, T, ns_pad):
    pages = refs[:SEL_PAGES]
    (q_ref, kc_ref, vc_ref, c2s_ref, newsel_ref, wincache_ref, newwin_ref, gate_ref, o_ref,
     qbd_scr, sel_scr, m_scr, l_scr, acc_scr, ocmp_scr, owin_scr) = refs[SEL_PAGES:]
    H, G, R, D = NSA_HEADS, NSA_KV_HEADS, NSA_RPG, NSA_HD
    rows = H * T
    j = pl.program_id(1)
    n_steps = pl.num_programs(1)
    tk = SEL_PAGES * PAGE_SIZE
    row_g = lax.broadcasted_iota(jnp.int32, (rows, 1), 0) // (R * T)
    row_t = PAST_LEN + lax.broadcasted_iota(jnp.int32, (rows, 1), 0) % T

    def place(x):
        return jnp.concatenate([jnp.where(row_g == gg, x, 0.0) for gg in range(G)], axis=1)

    def own(x):
        return sum(jnp.where(row_g == gg, x[:, gg * D:(gg + 1) * D], 0.0) for gg in range(G))

    @pl.when(j == 0)
    def _():
        q = q_ref[...].reshape(rows, D)
        qbd = place(q).astype(BF16)
        qbd_scr[...] = qbd
        nc = kc_ref.shape[1]
        t_q = PAST_LEN + lax.broadcasted_iota(jnp.int32, (T, 1), 0)
        cmp_end = lax.broadcasted_iota(jnp.int32, (1, nc), 1) * CMP_STRIDE + (CMP_BLOCK - 1)
        blk = lax.broadcasted_iota(jnp.int32, (1, ns_pad), 1)
        cur = t_q // SEL_BLOCK
        forced = (blk == 0) | (blk == cur) | (blk == cur - 1)
        valid = blk * SEL_BLOCK <= t_q
        for g in range(G):
            qg = q[g * R * T:(g + 1) * R * T]
            s = lax.dot_general(qg, kc_ref[g], NT_DIMS, precision=HIGHEST, preferred_element_type=F32).reshape(R, T, nc)
            p = _softmax_direct(s, cmp_end <= t_q)
            ocmp_scr[g * R * T:(g + 1) * R * T, :] = jnp.dot(p.reshape(R * T, nc).astype(BF16), vc_ref[g],
                                                             preferred_element_type=F32)
            imp = jnp.dot(jnp.sum(p, axis=0), c2s_ref[...], precision=HIGHEST, preferred_element_type=F32)
            score = jnp.where(forced, SEL_FORCE, jnp.where(valid, imp, -SEL_FORCE))
            sel = _top_blocks(score, N_SELECT)
            sel_scr[g * R * T:(g + 1) * R * T, :] = jnp.broadcast_to(sel[None], (R, T, ns_pad)).reshape(R * T, ns_pad)
        wc = wincache_ref[...].astype(BF16)
        nw = newwin_ref[...].astype(BF16)
        sw = jnp.concatenate([lax.dot_general(qbd, wc[:, :G * D], NT_DIMS, preferred_element_type=F32),
                              lax.dot_general(qbd, nw[:, :G * D], NT_DIMS, preferred_element_type=F32)], axis=1)
        kpos = jnp.concatenate([PAST_LEN - WINDOW + lax.broadcasted_iota(jnp.int32, (1, WINDOW), 1),
                                PAST_LEN + lax.broadcasted_iota(jnp.int32, (1, T), 1)], axis=1)
        pw = _softmax_direct(sw[None], (kpos <= row_t) & (kpos > row_t - WINDOW))[0].astype(BF16)
        owin_scr[...] = own(jnp.dot(pw[:, :WINDOW], wc[:, G * D:], preferred_element_type=F32)
                            + jnp.dot(pw[:, WINDOW:], nw[:, G * D:], preferred_element_type=F32))
        m_scr[...] = jnp.full(m_scr.shape, NEG_BIG, F32)
        l_scr[...] = jnp.zeros(l_scr.shape, F32)
        acc_scr[...] = jnp.zeros(acc_scr.shape, F32)

    def flash(kv, mask):
        s = lax.dot_general(qbd_scr[...], kv[:, :G * D], NT_DIMS, preferred_element_type=F32)
        sm = jnp.where(mask, s, NEG_BIG)
        m_old = m_scr[...]
        m_new = jnp.maximum(m_old, jnp.max(sm, axis=-1, keepdims=True))
        alpha = jnp.exp(m_old - m_new)
        p = jnp.where(mask, jnp.exp(sm - m_new), 0.0)
        l_scr[...] = alpha * l_scr[...] + jnp.sum(p, axis=-1, keepdims=True)
        acc_scr[...] = alpha * acc_scr[...] + jnp.dot(p.astype(BF16), kv[:, G * D:], preferred_element_type=F32)
        m_scr[...] = m_new

    kv = jnp.concatenate([pg[...].astype(BF16) for pg in pages], axis=0)
    per_step = tk // SEL_BLOCK
    key_blk = lax.broadcasted_iota(jnp.int32, (ns_pad, tk), 1) // SEL_BLOCK + j * per_step
    expand = (key_blk == lax.broadcasted_iota(jnp.int32, (ns_pad, tk), 0)).astype(BF16)
    flash(kv, jnp.dot(sel_scr[...].astype(BF16), expand, preferred_element_type=F32) > 0.5)

    @pl.when(j == n_steps - 1)
    def _():
        new_blk = PAST_LEN // SEL_BLOCK
        kpos = PAST_LEN + lax.broadcasted_iota(jnp.int32, (1, T), 1)
        flash(newsel_ref[...].astype(BF16), (sel_scr[:, new_blk:new_blk + 1] > 0.5) & (kpos <= row_t))
        o_sel = own(acc_scr[...]) / jnp.maximum(l_scr[...], F32_TINY)
        gs = jax.nn.sigmoid(gate_ref[...])
        o = gs[:, 0:1] * ocmp_scr[...] + gs[:, 1:2] * o_sel + gs[:, 2:3] * owin_scr[...]
        for h in range(H):
            o_ref[:, h * D:(h + 1) * D] = o[h * T:(h + 1) * T]


def _nsa_sample_attn(qh, kc, vc, pool_sel, layer, page_table, new_sel, win_cache, new_win, gate_rows):
    B, H, T, D = qh.shape
    n_pages = page_table.shape[1]
    nc = kc.shape[2]
    ns = -(-(PAST_LEN + T) // SEL_BLOCK)
    ns_pad = -(-ns // LANE) * LANE
    c2s = _cmp_to_sel_padded(nc, ns_pad)
    rows = H * T
    cst = lambda shape: pl.BlockSpec(shape, lambda b, j, pt: (0,) * len(shape))
    per_b = lambda shape: pl.BlockSpec((None,) + shape, lambda b, j, pt: (b,) + (0,) * len(shape))
    page_spec = lambda n: pl.BlockSpec((None, None, PAGE_SIZE, 2 * NSA_KV_DIM),
                                       lambda b, j, pt: (layer, pt[b, j * SEL_PAGES + n], 0, 0))
    grid_spec = pltpu.PrefetchScalarGridSpec(
        num_scalar_prefetch=1, grid=(B, n_pages // SEL_PAGES),
        in_specs=[page_spec(n) for n in range(SEL_PAGES)] + [
            per_b((H, T, D)), per_b((NSA_KV_HEADS, nc, D)), per_b((NSA_KV_HEADS, nc, D)), cst(c2s.shape),
            per_b((T, 2 * NSA_KV_DIM)), per_b((WINDOW, 2 * NSA_KV_DIM)), per_b((T, 2 * NSA_KV_DIM)), per_b((rows, 3))],
        out_specs=per_b((T, NSA_DIM)),
        scratch_shapes=[pltpu.VMEM((rows, NSA_KV_DIM), BF16), pltpu.VMEM((rows, ns_pad), F32),
                        pltpu.VMEM((rows, 1), F32), pltpu.VMEM((rows, 1), F32), pltpu.VMEM((rows, NSA_KV_DIM), F32),
                        pltpu.VMEM((rows, D), F32), pltpu.VMEM((rows, D), F32)])
    return pl.pallas_call(
        functools.partial(_nsa_sample_kernel, T=T, ns_pad=ns_pad), grid_spec=grid_spec,
        out_shape=jax.ShapeDtypeStruct((B, T, NSA_DIM), F32),
        compiler_params=_cparams("arbitrary", "arbitrary"),
        name="nsa_sample_attn",
    )(page_table, *([pool_sel] * SEL_PAGES), qh, kc, vc, c2s, new_sel, win_cache, new_win, gate_rows)


def _nsa_sample_pallas(q2, cmp2, sel2, win2, gates2, p, wb, B, T, layer, cache_cmp_kv, cache_sel_kv, win_buf, page_table):
    qh, _, _, _, _, sel_rows, win_rows = _nsa_prep(q2, sel2, win2, p['nsa_q_norm'], p['nsa_k_norm'], B, T)
    depth, n_pool = cache_cmp_kv.shape[:2]
    pool_cmp = cache_cmp_kv.reshape(depth, n_pool, PAGE_HALVES, HALF_COLS)
    tb_past = _half_proj_paged(pool_cmp, layer, page_table, wb['cmp_pair'])
    new_half = jnp.pad(cmp2.reshape(B, 1, T * 2 * NSA_KV_DIM), ((0, 0), (0, 7), (0, HALF_COLS - T * 2 * NSA_KV_DIM)))
    tb_new = _half_proj_rows(new_half.reshape(B * 8, HALF_COLS), wb['cmp_pair']).reshape(B, 8, -1)
    kc, vc = _cmp_finish(tb_past, tb_new, wb['cmp_w1'], p['nsa_cmp_pos'], p['nsa_cmp_b1'], wb['cmp_w2'], p['nsa_k_norm'][0])
    pool_sel = cache_sel_kv.reshape(depth, n_pool, PAGE_SIZE, 2 * NSA_KV_DIM)
    win_cache = win_buf.reshape(B, WINDOW, 2 * NSA_KV_DIM)
    g = gates2[:, :NSA_KV_HEADS * 16].reshape(B, T, NSA_KV_HEADS, 4, NSA_RPG)[:, :, :, :3]
    gate_rows = g.transpose(0, 2, 4, 1, 3).reshape(B, NSA_HEADS * T, 3)
    sel3, win3 = sel_rows.reshape(B, T, -1), win_rows.reshape(B, T, -1)
    o = _nsa_sample_attn(qh, kc, vc, pool_sel, layer, page_table, sel3, win_cache, win3, gate_rows)
    shp = (B, T, 2, NSA_KV_HEADS, NSA_HD)
    win_out = jnp.concatenate([win_buf, win_rows.reshape(shp)], axis=1)[:, -WINDOW:]
    return o, (cmp2.reshape(shp), sel_rows.reshape(shp), win_out)


RW_CHUNK = 64


def _heads(x):
    return jnp.stack([x[:, h * RW_HD:(h + 1) * RW_HD] for h in range(x.shape[1] // RW_HD)], axis=0)


def _unheads(x):
    return jnp.concatenate([x[h] for h in range(x.shape[0])], axis=1)


def _bmm(a, b):
    return jnp.einsum('hlm,hmn->hln', a.astype(BF16), b.astype(BF16), preferred_element_type=F32)


def _bmm_nt(a, b):
    return jnp.einsum('hlk,hmk->hlm', a.astype(BF16), b.astype(BF16), preferred_element_type=F32)


def _bmm_tn(a, b):
    return jnp.einsum('hlv,hlk->hvk', a.astype(BF16), b.astype(BF16), preferred_element_type=F32)


def _rwkv_kernel(x_ref, shift_ref, s0_ref, mu_ref, w0_ref, a0_ref, w2_ref, a2_ref, g2_ref, kk_ref, ka_ref, rk_ref,
                 lng_ref, lnb_ref, o_ref, sfin_ref, prev_scr, s_scr):
    L = x_ref.shape[0]
    c = pl.program_id(1)

    @pl.when(c == 0)
    def _():
        prev_scr[...] = shift_ref[...]
        s_scr[...] = s0_ref[...]

    x = x_ref[...]
    prev = jnp.concatenate([prev_scr[...], x[:L - 1]], axis=0)
    prev_scr[...] = x[L - 1:]
    xs = x + mu_ref[...] * (prev - x)
    r, k, v = xs[:, :RW_DIM], xs[:, RW_DIM:2 * RW_DIM], xs[:, 2 * RW_DIM:3 * RW_DIM]
    wl = xs[:, RW_SPLITS[2]:RW_SPLITS[3]]
    al = xs[:, RW_SPLITS[3]:RW_SPLITS[4]]
    gl = xs[:, RW_SPLITS[4]:]
    w = -jax.nn.softplus(-(w0_ref[...] + jnp.dot(jnp.tanh(wl).astype(BF16), w2_ref[...], preferred_element_type=F32))) - 0.5
    logd = -jnp.exp(w)
    a = jax.nn.sigmoid(a0_ref[...] + jnp.dot(al.astype(BF16), a2_ref[...], preferred_element_type=F32))
    g = jnp.dot(jax.nn.sigmoid(gl).astype(BF16), g2_ref[...], preferred_element_type=F32)
    kk = k * kk_ref[...]
    k = k * (1.0 + (a - 1.0) * ka_ref[...])
    row = lax.broadcasted_iota(jnp.int32, (L, L), 0)
    col = lax.broadcasted_iota(jnp.int32, (L, L), 1)
    cs = jnp.dot((col <= row).astype(F32), logd, precision=HIGHEST, preferred_element_type=F32)
    g_incl, g_prev, g_inv = jnp.exp(cs), jnp.exp(cs - logd), jnp.exp(-cs)

    kk_h = _heads(kk)
    kk_h = kk_h * lax.rsqrt(jnp.sum(kk_h * kk_h, axis=-1, keepdims=True) + 1e-12)
    r_h, k_h, v_h = _heads(r), _heads(k), _heads(v)
    inv_h = _heads(g_inv)
    at = kk_h * _heads(g_prev)
    rt = r_h * _heads(g_incl)
    bt = -(kk_h * _heads(a)) * inv_h
    kt = k_h * inv_h
    ar = jnp.concatenate([at, rt], axis=1)
    bk = jnp.concatenate([bt, kt], axis=1)
    gram = _bmm_nt(ar, bk)
    strict, incl = (col < row)[None], (col <= row)[None]
    a_ab = jnp.where(strict, gram[:, :L, :L], 0.0)
    a_ak = jnp.where(strict, gram[:, :L, L:], 0.0)
    a_rbk = jnp.concatenate([jnp.where(incl, gram[:, L:, :L], 0.0), jnp.where(incl, gram[:, L:, L:], 0.0)], axis=2)
    tinv = jnp.where((col == row)[None], 1.0, a_ab)
    pw = a_ab
    n_sq, span = 0, 2
    while span < L:
        n_sq, span = n_sq + 1, span * 2
    for _ in range(n_sq):
        pw = _bmm(pw, pw)
        tinv = tinv + _bmm(tinv, pw)

    s0 = s_scr[...]
    ars = _bmm_nt(ar, s0)
    u = _bmm(tinv, ars[:, :L] + _bmm(a_ak, v_h))
    uv = jnp.concatenate([u, v_h], axis=1)
    o = ars[:, L:] + _bmm(a_rbk, uv)
    s_new = (s0 + _bmm_tn(uv, bk)) * _heads(g_incl[L - 1:])
    s_scr[...] = s_new

    @pl.when(c == pl.num_programs(1) - 1)
    def _():
        sfin_ref[...] = s_new

    mean = jnp.mean(o, axis=-1, keepdims=True)
    var = jnp.mean(jnp.square(o - mean), axis=-1, keepdims=True)
    on = _unheads((o - mean) * lax.rsqrt(var + RW_GN_EPS)) * lng_ref[...] + lnb_ref[...]
    bonus = _unheads(jnp.sum(r_h * k_h * _heads(rk_ref[...]), axis=-1, keepdims=True) * v_h)
    o_ref[...] = (on + bonus) * g


def _rwkv7_pallas(p_rw, shift_state, wkv_state, p, wb):
    B, T, _ = p_rw.shape
    L = min(RW_CHUNK, T)
    assert T % L == 0
    vec = lambda name: p[name].reshape(1, -1)
    full = lambda shape: pl.BlockSpec(shape, lambda b, c: (0,) * len(shape))
    st = pl.BlockSpec((None, RW_HEADS, RW_HD, RW_HD), lambda b, c: (b, 0, 0, 0))
    o, s_fin = pl.pallas_call(
        _rwkv_kernel,
        grid=(B, T // L),
        in_specs=[pl.BlockSpec((None, L, RW_IN), lambda b, c: (b, c, 0)),
                  pl.BlockSpec((None, 1, RW_IN), lambda b, c: (b, 0, 0)), st,
                  full((1, RW_IN)), full((1, RW_DIM)), full((1, RW_DIM)),
                  full((RW_W_LORA, RW_DIM)), full((RW_A_LORA, RW_DIM)), full((RW_G_LORA, RW_DIM)),
                  full((1, RW_DIM)), full((1, RW_DIM)), full((1, RW_DIM)), full((1, RW_DIM)), full((1, RW_DIM))],
        out_specs=[pl.BlockSpec((None, L, RW_DIM), lambda b, c: (b, c, 0)), st],
        out_shape=[jax.ShapeDtypeStruct((B, T, RW_DIM), F32),
                   jax.ShapeDtypeStruct((B, RW_HEADS, RW_HD, RW_HD), F32)],
        scratch_shapes=[pltpu.VMEM((1, RW_IN), F32), pltpu.VMEM((RW_HEADS, RW_HD, RW_HD), F32)],
        compiler_params=_cparams("arbitrary", "arbitrary"),
        name="rwkv7_mix",
    )(p_rw, shift_state.reshape(B, 1, RW_IN), wkv_state, vec('rw_mu'), vec('rw_w0'), vec('rw_a0'),
      wb['rw_w2'], wb['rw_a2'], wb['rw_g2'], vec('rw_k_k'), vec('rw_k_a'), vec('rw_r_k'), vec('rw_lnx_g'), vec('rw_lnx_b'))
    return o, p_rw[:, -1], s_fin


def _rmsnorm(x, g):
    y = x * lax.rsqrt(jnp.mean(x * x, axis=-1, keepdims=True) + EPS)
    return y * g


def _masked_softmax(s, mask):
    s = jnp.where(mask, s, -jnp.inf)
    m = jnp.max(s, axis=-1, keepdims=True)
    m = jnp.where(jnp.isfinite(m), m, 0.0)
    e = jnp.exp(s - m)
    return e / jnp.maximum(jnp.sum(e, axis=-1, keepdims=True), jnp.finfo(F32).tiny)


def _rwkv7_mix(proj, shift_state, wkv_state, p):
    B, T, _ = proj.shape
    prev = jnp.concatenate([shift_state[:, None], proj[:, :-1]], axis=1)
    xs = proj + p['rw_mu'] * (prev - proj)
    r, k, v, wl, al, gl = jnp.split(xs, RW_SPLITS, axis=-1)
    w = -jax.nn.softplus(-(p['rw_w0'] + jnp.tanh(wl) @ p['rw_w2'])) - 0.5
    decay = jnp.exp(-jnp.exp(w))
    a = jax.nn.sigmoid(p['rw_a0'] + al @ p['rw_a2'])
    g = jax.nn.sigmoid(gl) @ p['rw_g2']
    kk = k * p['rw_k_k']
    k = k * (1.0 + (a - 1.0) * p['rw_k_a'])
    hd = lambda t: t.reshape(B, T, RW_HEADS, RW_HD)
    r, k, v, kk, a, decay = hd(r), hd(k), hd(v), hd(kk), hd(a), hd(decay)
    kk = kk * lax.rsqrt(jnp.sum(kk * kk, axis=-1, keepdims=True) + 1e-12)

    def step(S, inp):
        r_t, w_t, k_t, v_t, kk_t, a_t = inp
        sa = jnp.einsum('bhvk,bhk->bhv', S, kk_t)
        S = S * w_t[:, :, None, :] - sa[..., None] * (kk_t * a_t)[:, :, None, :] + v_t[..., None] * k_t[:, :, None, :]
        return S, jnp.einsum('bhvk,bhk->bhv', S, r_t)

    seqs = tuple(jnp.swapaxes(t, 0, 1) for t in (r, decay, k, v, kk, a))
    S_fin, o = lax.scan(step, wkv_state, seqs)
    o = jnp.swapaxes(o, 0, 1)
    mean = jnp.mean(o, axis=-1, keepdims=True)
    var = jnp.mean(jnp.square(o - mean), axis=-1, keepdims=True)
    o = ((o - mean) * lax.rsqrt(var + RW_GN_EPS)).reshape(B, T, RW_DIM) * p['rw_lnx_g'] + p['rw_lnx_b']
    bonus = jnp.sum(r * k * p['rw_r_k'], axis=-1, keepdims=True) * v
    o = (o + bonus.reshape(B, T, RW_DIM)) * g
    return o, proj[:, -1], S_fin


def _ssd_chunked(x, dt, A, Bm, Cm, s0):
    Bsz, T = x.shape[:2]
    L = SSM_CHUNK if T % SSM_CHUNK == 0 else T
    nc = T // L
    ch = lambda t: t.reshape((Bsz, nc, L) + t.shape[2:])
    xdt = ch(x * dt[..., None])
    a_cs = jnp.cumsum(ch(dt * A), axis=2)
    Bc, Cc = ch(Bm), ch(Cm)
    seg = a_cs[:, :, :, None] - a_cs[:, :, None, :]
    causal = jnp.tril(jnp.ones((L, L), dtype=bool))[:, :, None, None]
    decay_ls = jnp.exp(jnp.where(causal, seg, -jnp.inf))
    cb = jnp.einsum('bclgn,bcsgn->bclsg', Cc, Bc)
    y_diag = jnp.einsum('bclsgr,bcsgrp->bclgrp', cb[..., None] * decay_ls, xdt)
    decay_s = jnp.exp(a_cs[:, :, -1:] - a_cs)
    states = jnp.einsum('bclgn,bclgr,bclgrp->bcgrpn', Bc, decay_s, xdt)
    chunk_decay = jnp.exp(a_cs[:, :, -1])

    def step(s, inp):
        st, dec = inp
        return s * dec[..., None, None] + st, s

    s_fin, s_in = lax.scan(step, s0, (jnp.swapaxes(states, 0, 1), jnp.swapaxes(chunk_decay, 0, 1)))
    s_in = jnp.swapaxes(s_in, 0, 1)
    y_off = jnp.einsum('bclgn,bcgrpn,bclgr->bclgrp', Cc, s_in, jnp.exp(a_cs))
    return (y_diag + y_off).reshape(x.shape), s_fin


def _mamba2_mix(z, xbc, dt, conv_state, ssm_state, p):
    B, T, _ = z.shape
    xpad = jnp.concatenate([conv_state, xbc], axis=1)
    conv = p['ssm_conv_b'] + sum(xpad[:, i:i + T] * p['ssm_conv_w'][i] for i in range(SSM_CONV))
    xbc_c = jax.nn.silu(conv)
    xs, Bm, Cm = jnp.split(xbc_c, (SSM_DIM, SSM_DIM + SSM_GROUPS * SSM_STATE), axis=-1)
    x = xs.reshape(B, T, SSM_GROUPS, SSM_RPG, SSM_HD)
    Bm = Bm.reshape(B, T, SSM_GROUPS, SSM_STATE)
    Cm = Cm.reshape(B, T, SSM_GROUPS, SSM_STATE)
    dt = jax.nn.softplus(dt + p['ssm_dt_bias']).reshape(B, T, SSM_GROUPS, SSM_RPG)
    A = -jnp.exp(p['ssm_a_log']).reshape(SSM_GROUPS, SSM_RPG)
    s0 = ssm_state.reshape(B, SSM_GROUPS, SSM_RPG, SSM_HD, SSM_STATE)
    y, s_fin = _ssd_chunked(x, dt, A, Bm, Cm, s0)
    y = y + p['ssm_d'].reshape(SSM_GROUPS, SSM_RPG)[:, :, None] * x
    y = y.reshape(B, T, SSM_DIM) * jax.nn.silu(z)
    y = _rmsnorm(y, p['ssm_norm_g'])
    return y, xpad[:, -(SSM_CONV - 1):], s_fin.reshape(B, SSM_HEADS, SSM_HD, SSM_STATE)


def _nsa_project(q, kv, gates, p):
    B, T, _ = q.shape
    kc, vc, ks, vs, kw, vw = jnp.split(kv, 6, axis=-1)
    kvh = lambda t: t.reshape(B, T, NSA_KV_HEADS, NSA_HD)
    q = _rmsnorm(q.reshape(B, T, NSA_HEADS, NSA_HD), p['nsa_q_norm']).reshape(B, T, NSA_KV_HEADS, NSA_RPG, NSA_HD)
    cmp_rows = jnp.stack([kvh(kc), kvh(vc)], axis=2)
    sel_rows = jnp.stack([_rmsnorm(kvh(ks), p['nsa_k_norm'][1]), kvh(vs)], axis=2)
    win_rows = jnp.stack([_rmsnorm(kvh(kw), p['nsa_k_norm'][2]), kvh(vw)], axis=2)
    g = jax.nn.sigmoid(gates).reshape(B, T, 3, NSA_KV_HEADS, NSA_RPG, 1)
    return q, cmp_rows, sel_rows, win_rows, g


def _half_proj(rows, w1):
    B, T = rows.shape[:2]
    h = rows.reshape(B, T // CMP_STRIDE, CMP_STRIDE, 2, NSA_KV_HEADS, NSA_HD)
    top = jnp.einsum('bnsckd,csdh->bnckh', h, w1[:, :CMP_STRIDE])
    bot = jnp.einsum('bnsckd,csdh->bnckh', h, w1[:, CMP_STRIDE:])
    return top, bot


def _compress_finish(top, bot, p):
    w1 = p['nsa_cmp_w1']
    const = jnp.einsum('csd,csdh->ch', p['nsa_cmp_pos'], w1) + p['nsa_cmp_b1']
    hid = jax.nn.gelu(top[:, :-1] + bot[:, 1:] + const[:, None, :])
    kv = jnp.einsum('bnckh,chd->bnckd', hid, p['nsa_cmp_w2'])
    return _rmsnorm(kv[:, :, 0], p['nsa_k_norm'][0]), kv[:, :, 1]


def _cmp_to_sel(nc, ns):
    i = jnp.arange(nc)[:, None] * CMP_STRIDE
    j = jnp.arange(ns)[None, :] * SEL_BLOCK
    ov = jnp.minimum(i + CMP_BLOCK, j + SEL_BLOCK) - jnp.maximum(i, j)
    return jnp.clip(ov, 0, None).astype(F32) / CMP_BLOCK


def _fetch_rows(kv, pos):
    b = jnp.arange(kv.shape[0])[:, None, None, None]
    g = jnp.arange(NSA_KV_HEADS)[None, :, None, None]
    return kv[b, jnp.clip(pos, 0, kv.shape[1] - 1), :, g, :]


def _fetch_paged(pool, l, page_table, new_rows, pos):
    b = jnp.arange(page_table.shape[0])[:, None, None, None]
    g = jnp.arange(NSA_KV_HEADS)[None, :, None, None]
    pp = jnp.clip(pos, 0, PAST_LEN - 1)
    phys = page_table[b, pp // PAGE_SIZE]
    past = pool[l, phys, pp % PAGE_SIZE, :, g, :]
    new = _fetch_rows(new_rows, pos - PAST_LEN)
    return jnp.where((pos < PAST_LEN)[..., None, None], past, new)


def _nsa_cmp_sel(q, t, kc, vc, fetch, ns):
    B, Tq = q.shape[:2]
    scale = NSA_HD ** -0.5
    nc = kc.shape[1]
    s = jnp.einsum('bqgrd,bngd->bgrqn', q, kc) * scale
    cmp_end = jnp.arange(nc) * CMP_STRIDE + CMP_BLOCK - 1
    p_cmp = _masked_softmax(s, cmp_end[None, :] <= t[:, None])
    o_cmp = jnp.einsum('bgrqn,bngd->bqgrd', p_cmp, vc)
    imp = jnp.einsum('bgrqn,ns->bgqs', p_cmp, _cmp_to_sel(nc, ns))
    blk = jnp.arange(ns)[None, :]
    cur = t[:, None] // SEL_BLOCK
    forced = (blk == 0) | (blk == cur) | (blk == cur - 1)
    valid = blk * SEL_BLOCK <= t[:, None]
    score = jnp.where(forced, SEL_FORCE, jnp.where(valid, imp, -SEL_FORCE))
    n_top = min(N_SELECT, ns)
    _, idx = lax.top_k(score, n_top)
    pos = (idx[..., None] * SEL_BLOCK + jnp.arange(SEL_BLOCK)).reshape(B, NSA_KV_HEADS, Tq, n_top * SEL_BLOCK)
    kvs = fetch(pos)
    s2 = jnp.einsum('bqgrd,bgqkd->bgrqk', q, kvs[..., 0, :]) * scale
    p_sel = _masked_softmax(s2, (pos <= t[:, None])[:, :, None])
    o_sel = jnp.einsum('bgrqk,bgqkd->bqgrd', p_sel, kvs[..., 1, :])
    return o_cmp, o_sel


def _nsa_window(q, t, kw, vw, kpos):
    s = jnp.einsum('bqgrd,bngd->bgrqn', q, kw) * (NSA_HD ** -0.5)
    mask = (kpos[None, :] <= t[:, None]) & (kpos[None, :] > t[:, None] - WINDOW) & (kpos[None, :] >= 0)
    p = _masked_softmax(s, mask)
    return jnp.einsum('bgrqn,bngd->bqgrd', p, vw)


def _nsa_merge(o, g):
    B, T = o.shape[:2]
    return jnp.sum(g * o, axis=2).reshape(B, T, NSA_DIM)


def _nsa_prompt(q, kv, gates, p):
    q, cmp_rows, sel_rows, win_rows, g = _nsa_project(q, kv, gates, p)
    B, T = q.shape[:2]
    top, bot = _half_proj(cmp_rows, p['nsa_cmp_w1'])
    kc, vc = _compress_finish(top, bot, p)
    ns = -(-T // SEL_BLOCK)
    fetch = lambda pos: _fetch_rows(sel_rows, pos)
    win_pad = jnp.pad(win_rows, ((0, 0), (WINDOW, 0), (0, 0), (0, 0), (0, 0)))
    nqb = T // Q_BLOCK
    qb = jnp.swapaxes(q.reshape(B, nqb, Q_BLOCK, NSA_KV_HEADS, NSA_RPG, NSA_HD), 0, 1)

    def body(args):
        q_blk, i = args
        t = i * Q_BLOCK + jnp.arange(Q_BLOCK)
        o_cmp, o_sel = _nsa_cmp_sel(q_blk, t, kc, vc, fetch, ns)
        kv_w = lax.dynamic_slice_in_dim(win_pad, i * Q_BLOCK, WINDOW + Q_BLOCK, axis=1)
        kpos = i * Q_BLOCK - WINDOW + jnp.arange(WINDOW + Q_BLOCK)
        o_win = _nsa_window(q_blk, t, kv_w[:, :, 0], kv_w[:, :, 1], kpos)
        return jnp.stack([o_cmp, o_sel, o_win], axis=2)

    o = lax.map(body, (qb, jnp.arange(nqb)))
    o = jnp.swapaxes(o, 0, 1).reshape(B, T, 3, NSA_KV_HEADS, NSA_RPG, NSA_HD)
    return _nsa_merge(o, g), (cmp_rows, sel_rows, win_pad[:, -WINDOW:])


def _nsa_sample(q, kv, gates, p, l, cache_cmp_kv, cache_sel_kv, win_buf, page_table):
    q, cmp_rows, sel_rows, win_rows, g = _nsa_project(q, kv, gates, p)
    B, T = q.shape[:2]
    past_cmp = cache_cmp_kv[l, page_table].reshape(B, PAST_LEN, 2, NSA_KV_HEADS, NSA_HD)
    new_cmp = jnp.pad(cmp_rows, ((0, 0), (0, (-T) % CMP_STRIDE), (0, 0), (0, 0), (0, 0)))
    tp, bp = _half_proj(past_cmp, p['nsa_cmp_w1'])
    tn, bn = _half_proj(new_cmp, p['nsa_cmp_w1'])
    kc, vc = _compress_finish(jnp.concatenate([tp, tn], axis=1), jnp.concatenate([bp, bn], axis=1), p)
    t = PAST_LEN + jnp.arange(T)
    ns = -(-(PAST_LEN + T) // SEL_BLOCK)
    fetch = lambda pos: _fetch_paged(cache_sel_kv, l, page_table, sel_rows, pos)
    o_cmp, o_sel = _nsa_cmp_sel(q, t, kc, vc, fetch, ns)
    kv_w = jnp.concatenate([win_buf, win_rows], axis=1)
    kpos = PAST_LEN - WINDOW + jnp.arange(WINDOW + T)
    o_win = _nsa_window(q, t, kv_w[:, :, 0], kv_w[:, :, 1], kpos)
    o = jnp.stack([o_cmp, o_sel, o_win], axis=2)
    return _nsa_merge(o, g), (cmp_rows, sel_rows, kv_w[:, -WINDOW:])


def _prep_weights(p):
    w_in = p['w_in']
    o_ssm = RW_IN
    o_nsa = RW_IN + SSM_IN
    o_gate = o_nsa + NSA_IN
    pad = lambda w, n: jnp.pad(w, ((0, 0), (0, n - w.shape[1])))
    w_rw = w_in[:, :RW_IN]
    w_ssm = jnp.concatenate([w_in[:, o_ssm:o_ssm + SSM_DIM + SSM_XBC],
                             pad(w_in[:, o_ssm + SSM_DIM + SSM_XBC:o_nsa], LANE)], axis=1)
    o_g = o_nsa + NSA_DIM + 6 * NSA_KV_DIM
    wg = w_in[:, o_g:o_gate].reshape(D_MODEL, 3, NSA_KV_HEADS, NSA_RPG).transpose(0, 2, 1, 3)
    wg = jnp.pad(wg.reshape(D_MODEL, NSA_KV_HEADS, 3 * NSA_RPG), ((0, 0), (0, 0), (0, 16 - 3 * NSA_RPG)))
    w_nsa = jnp.concatenate([w_in[:, o_nsa:o_g], pad(wg.reshape(D_MODEL, NSA_KV_HEADS * 16), LANE),
                             w_in[:, o_gate:]], axis=1)
    bf = lambda w: w.astype(BF16)
    return dict(w_ada=bf(p['w_ada']), w_rw=bf(w_rw), w_ssm=bf(w_ssm), w_nsa=bf(w_nsa),
                w_br_rw=bf(p['w_br_rw']), w_br_ssm=bf(p['w_br_ssm']), w_br_nsa=bf(p['w_br_nsa']),
                w_out=bf(p['w_out']), w_ffn_in=bf(p['w_ffn_in']), w_ffn_out=bf(p['w_ffn_out']),
                rw_w2=bf(p['rw_w2']), rw_a2=bf(p['rw_a2']), rw_g2=bf(p['rw_g2']),
                cmp_w1=bf(p['nsa_cmp_w1'].reshape(2, CMP_BLOCK * NSA_HD, CMP_HIDDEN)), cmp_w2=bf(p['nsa_cmp_w2']),
                cmp_pair=_pair_weights(p['nsa_cmp_w1']))


def _trunk_layer(x, c, p, wb, rw_shift, rw_wkv, conv_state, ssm_state, nsa_fn, per_batch):
    B, T, _ = x.shape
    n = B * T
    mod = _ada(c, wb['w_ada'], p['b_ada'])
    if per_batch:
        mods = [m.reshape(B, 1, D_MODEL) for m in jnp.split(mod, 6, axis=-1)]
    else:
        mods = [jnp.repeat(m, T, axis=0) for m in jnp.split(mod, 6, axis=-1)]
    sh1, sc1, gt1, sh2, sc2, gt2 = mods
    x2 = x.reshape(n, D_MODEL)
    (p_rw,) = _norm_proj(x2, sc1, sh1, p['ln1'], wb['w_rw'], (RW_IN,), per_batch, T, "proj_rw")
    z, xbc, dt = _norm_proj(x2, sc1, sh1, p['ln1'], wb['w_ssm'], (SSM_DIM, SSM_XBC, LANE), per_batch, T, "proj_ssm")
    kvw = 2 * NSA_KV_DIM
    q, cmp2, sel2, win2, gates, p_gate = _norm_proj(x2, sc1, sh1, p['ln1'], wb['w_nsa'],
                                                    (NSA_DIM, kvw, kvw, kvw, LANE, N_BRANCH * D_MODEL), per_batch, T,
                                                    "proj_nsa")
    r3 = lambda t: t.reshape(B, T, t.shape[-1])
    o_rw, rw_shift, rw_wkv = _rwkv7_pallas(r3(p_rw), rw_shift, rw_wkv, p, wb)
    o_ssm, conv_state, ssm_state = _mamba2_mix(r3(z), r3(xbc), r3(dt)[..., :SSM_HEADS], conv_state, ssm_state, p)
    o_nsa, (cmp_rows, sel_rows, win_buf) = nsa_fn(q, cmp2, sel2, win2, gates, p, wb, B, T)
    x2 = _merge(x2, gt1, o_rw.reshape(n, -1), o_ssm.reshape(n, -1), o_nsa.reshape(n, -1), p_gate,
                wb['w_br_rw'], wb['w_br_ssm'], wb['w_br_nsa'], wb['w_out'], per_batch, T)
    x2 = _ffn(x2, sc2, sh2, gt2, p['ln2'], wb['w_ffn_in'], wb['w_ffn_out'], per_batch, T)
    return x2.reshape(B, T, D_MODEL), (rw_shift, rw_wkv, conv_state, ssm_state, cmp_rows, sel_rows, win_buf)


def kernel(x_prompt, x_sample, cache_cmp_kv, cache_sel_kv, cache_win_kv, state_rwkv_shift, state_rwkv_wkv,
           state_ssm_conv, state_ssm, page_table, c_prompt, c_sample, w_ada, b_ada, ln1, ln2, w_in,
           rw_mu, rw_w0, rw_w2, rw_a0, rw_a2, rw_g2, rw_k_k, rw_k_a, rw_r_k, rw_lnx_g, rw_lnx_b,
           ssm_conv_w, ssm_conv_b, ssm_dt_bias, ssm_a_log, ssm_d, ssm_norm_g,
           nsa_q_norm, nsa_k_norm, nsa_cmp_pos, nsa_cmp_w1, nsa_cmp_b1, nsa_cmp_w2,
           w_br_rw, w_br_ssm, w_br_nsa, w_out, w_ffn_in, w_ffn_out):
    params = dict(w_ada=w_ada, b_ada=b_ada, ln1=ln1, ln2=ln2, w_in=w_in,
                  rw_mu=rw_mu, rw_w0=rw_w0, rw_w2=rw_w2, rw_a0=rw_a0, rw_a2=rw_a2, rw_g2=rw_g2,
                  rw_k_k=rw_k_k, rw_k_a=rw_k_a, rw_r_k=rw_r_k, rw_lnx_g=rw_lnx_g, rw_lnx_b=rw_lnx_b,
                  ssm_conv_w=ssm_conv_w, ssm_conv_b=ssm_conv_b, ssm_dt_bias=ssm_dt_bias, ssm_a_log=ssm_a_log,
                  ssm_d=ssm_d, ssm_norm_g=ssm_norm_g, nsa_q_norm=nsa_q_norm, nsa_k_norm=nsa_k_norm,
                  nsa_cmp_pos=nsa_cmp_pos, nsa_cmp_w1=nsa_cmp_w1, nsa_cmp_b1=nsa_cmp_b1, nsa_cmp_w2=nsa_cmp_w2,
                  w_br_rw=w_br_rw, w_br_ssm=w_br_ssm, w_br_nsa=w_br_nsa, w_out=w_out,
                  w_ffn_in=w_ffn_in, w_ffn_out=w_ffn_out)
    bp = x_prompt.shape[0]
    depth = w_in.shape[0]
    zero_shift = jnp.zeros((bp, RW_IN), F32)
    zero_wkv = jnp.zeros((bp, RW_HEADS, RW_HD, RW_HD), F32)
    zero_conv = jnp.zeros((bp, SSM_CONV - 1, SSM_XBC), F32)
    zero_ssm = jnp.zeros((bp, SSM_HEADS, SSM_HD, SSM_STATE), F32)
    xp, xs = x_prompt, x_sample
    st_p, st_s = [], []
    for l in range(depth):
        p = {name: arr[l] for name, arr in params.items()}
        wb = _prep_weights(p)
        xp, sp_l = _trunk_layer(xp, c_prompt, p, wb, zero_shift, zero_wkv, zero_conv, zero_ssm, _nsa_prompt_pallas, True)
        nsa_s = functools.partial(_nsa_sample_pallas, layer=l, cache_cmp_kv=cache_cmp_kv, cache_sel_kv=cache_sel_kv,
                                  win_buf=cache_win_kv[l], page_table=page_table)
        xs, ss_l = _trunk_layer(xs, c_sample, p, wb, state_rwkv_shift[l], state_rwkv_wkv[l], state_ssm_conv[l],
                                state_ssm[l], nsa_s, False)
        st_p.append(sp_l)
        st_s.append(ss_l)
    sp = [jnp.stack([s[i] for s in st_p]) for i in range(7)]
    ss = [jnp.stack([s[i] for s in st_s]) for i in range(7)]
    return (xp, xs, sp[4], ss[4], sp[5], ss[5], sp[6], ss[6], sp[0], ss[0], sp[1], ss[1], sp[2], ss[2], sp[3], ss[3])
```

```python
import functools

import jax
import jax.numpy as jnp
from jax import lax
from jax.experimental import pallas as pl
from jax.experimental.pallas import tpu as pltpu

F32 = jnp.float32
BF16 = jnp.bfloat16

D_MODEL = 1024
PAST_LEN = 16384
PAGE_SIZE = 128
RW_HEADS = 16
RW_HD = 64
RW_DIM = RW_HEADS * RW_HD
RW_W_LORA = 64
RW_A_LORA = 64
RW_G_LORA = 128
RW_IN = 3 * RW_DIM + RW_W_LORA + RW_A_LORA + RW_G_LORA
RW_SPLITS = (RW_DIM, 2 * RW_DIM, 3 * RW_DIM, 3 * RW_DIM + RW_W_LORA, 3 * RW_DIM + RW_W_LORA + RW_A_LORA)
RW_GN_EPS = 64e-5
SSM_DIM = 2 * D_MODEL
SSM_HD = 64
SSM_HEADS = SSM_DIM // SSM_HD
SSM_GROUPS = 4
SSM_RPG = SSM_HEADS // SSM_GROUPS
SSM_STATE = 128
SSM_CONV = 4
SSM_CHUNK = 128
SSM_XBC = SSM_DIM + 2 * SSM_GROUPS * SSM_STATE
SSM_IN = SSM_DIM + SSM_XBC + SSM_HEADS
NSA_HEADS = 16
NSA_HD = 64
NSA_KV_HEADS = 4
NSA_RPG = NSA_HEADS // NSA_KV_HEADS
NSA_DIM = NSA_HEADS * NSA_HD
NSA_KV_DIM = NSA_KV_HEADS * NSA_HD
CMP_STRIDE = 16
CMP_BLOCK = 2 * CMP_STRIDE
CMP_HIDDEN = 128
SEL_BLOCK = 64
N_SELECT = 16
WINDOW = 512
Q_BLOCK = 64
SEL_FORCE = 1e6
NSA_IN = NSA_DIM + 6 * NSA_KV_DIM + 3 * NSA_HEADS
N_BRANCH = 3
FFN_HIDDEN = ((8 * D_MODEL // 3 + 255) // 256) * 256
EPS = 1e-6

LANE = 128
VMEM_LIMIT = 56 * 1024 * 1024


def _cparams(*sem):
    return pltpu.CompilerParams(dimension_semantics=sem, vmem_limit_bytes=VMEM_LIMIT)


def _modnorm(x, g, sc, sh):
    y = x * lax.rsqrt(jnp.mean(x * x, axis=-1, keepdims=True) + EPS)
    return (y * g) * (1.0 + sc) + sh


def _mod_spec(per_batch, tm, tiles_per_batch):
    if per_batch:
        return pl.BlockSpec((None, 1, D_MODEL), lambda i: (i // tiles_per_batch, 0, 0))
    return pl.BlockSpec((tm, D_MODEL), lambda i: (i, 0))


def _full_spec(shape):
    return pl.BlockSpec(shape, lambda i: (0,) * len(shape))


def _ada_kernel(c_ref, w_ref, b_ref, o_ref):
    o_ref[...] = jnp.dot(c_ref[...].astype(BF16), w_ref[...], preferred_element_type=F32) + b_ref[...]


def _ada(c, w_bf, b):
    n, tn = c.shape[0], 1536
    return pl.pallas_call(
        _ada_kernel,
        grid=(6 * D_MODEL // tn,),
        in_specs=[pl.BlockSpec((n, D_MODEL), lambda j: (0, 0)),
                  pl.BlockSpec((D_MODEL, tn), lambda j: (0, j)),
                  pl.BlockSpec((1, tn), lambda j: (0, j))],
        out_specs=pl.BlockSpec((n, tn), lambda j: (0, j)),
        out_shape=jax.ShapeDtypeStruct((n, 6 * D_MODEL), F32),
        compiler_params=_cparams("arbitrary"),
        name="ada_mod",
    )(c, w_bf, b.reshape(1, -1))


def _norm_proj_kernel(x_ref, sc_ref, sh_ref, g_ref, w_ref, *o_refs, splits):
    h = _modnorm(x_ref[...], g_ref[...], sc_ref[...], sh_ref[...]).astype(BF16)
    for o_ref, (a, b) in zip(o_refs, splits):
        o_ref[...] = jnp.dot(h, w_ref[:, a:b], preferred_element_type=F32)


def _norm_proj(x2, sc, sh, g, w_bf, widths, per_batch, tokens_per_batch, name):
    n = x2.shape[0]
    tm = min(512, n)
    splits, a = [], 0
    for wd in widths:
        splits.append((a, a + wd))
        a += wd
    assert a == w_bf.shape[1] and n % tm == 0
    tpb = max(tokens_per_batch // tm, 1)
    return pl.pallas_call(
        functools.partial(_norm_proj_kernel, splits=tuple(splits)),
        grid=(n // tm,),
        in_specs=[pl.BlockSpec((tm, D_MODEL), lambda i: (i, 0)),
                  _mod_spec(per_batch, tm, tpb), _mod_spec(per_batch, tm, tpb),
                  _full_spec((1, D_MODEL)), _full_spec(w_bf.shape)],
        out_specs=[pl.BlockSpec((tm, wd), lambda i: (i, 0)) for wd in widths],
        out_shape=[jax.ShapeDtypeStruct((n, wd), F32) for wd in widths],
        compiler_params=_cparams("arbitrary"),
        name=name,
    )(x2, sc, sh, g.reshape(1, -1), w_bf)


def _merge_kernel(x_ref, gt_ref, orw_ref, ossm_ref, onsa_ref, gate_ref, wrw_ref, wssm_ref, wnsa_ref, wout_ref,
                  o_ref):
    gate = jax.nn.sigmoid(gate_ref[...])
    br = (jnp.dot(orw_ref[...].astype(BF16), wrw_ref[...], preferred_element_type=F32),
          jnp.dot(ossm_ref[...].astype(BF16), wssm_ref[...], preferred_element_type=F32),
          jnp.dot(onsa_ref[...].astype(BF16), wnsa_ref[...], preferred_element_type=F32))
    merged = sum(gate[:, i * D_MODEL:(i + 1) * D_MODEL] * br[i] for i in range(N_BRANCH))
    y = jnp.dot(merged.astype(BF16), wout_ref[...], preferred_element_type=F32)
    o_ref[...] = x_ref[...] + gt_ref[...] * y


def _merge(x2, gt, o_rw, o_ssm, o_nsa, gate, w_rw, w_ssm, w_nsa, w_out, per_batch, tokens_per_batch):
    n = x2.shape[0]
    tm = min(512, n)
    tpb = max(tokens_per_batch // tm, 1)
    row = lambda wd: pl.BlockSpec((tm, wd), lambda i: (i, 0))
    return pl.pallas_call(
        _merge_kernel,
        grid=(n // tm,),
        in_specs=[row(D_MODEL), _mod_spec(per_batch, tm, tpb), row(RW_DIM), row(SSM_DIM), row(NSA_DIM),
                  row(N_BRANCH * D_MODEL), _full_spec(w_rw.shape), _full_spec(w_ssm.shape),
                  _full_spec(w_nsa.shape), _full_spec(w_out.shape)],
        out_specs=row(D_MODEL),
        out_shape=jax.ShapeDtypeStruct((n, D_MODEL), F32),
        compiler_params=_cparams("arbitrary"),
        name="branch_merge",
    )(x2, gt, o_rw, o_ssm, o_nsa, gate, w_rw, w_ssm, w_nsa, w_out)


FFN_CHUNK = 256


def _ffn_kernel(x_ref, sc_ref, sh_ref, gt_ref, g_ref, win_ref, wout_ref, o_ref):
    x = x_ref[...]
    h = _modnorm(x, g_ref[...], sc_ref[...], sh_ref[...]).astype(BF16)
    acc = jnp.zeros(x.shape, F32)
    for c in range(FFN_HIDDEN // FFN_CHUNK):
        a = c * FFN_CHUNK
        up = jnp.dot(h, win_ref[:, a:a + FFN_CHUNK], preferred_element_type=F32)
        gf = jnp.dot(h, win_ref[:, FFN_HIDDEN + a:FFN_HIDDEN + a + FFN_CHUNK], preferred_element_type=F32)
        act = (jax.nn.silu(gf) * up).astype(BF16)
        acc = acc + jnp.dot(act, wout_ref[a:a + FFN_CHUNK, :], preferred_element_type=F32)
    o_ref[...] = x + gt_ref[...] * acc


def _ffn(x2, sc, sh, gt, g, w_in, w_out, per_batch, tokens_per_batch):
    n = x2.shape[0]
    tm = min(512, n)
    tpb = max(tokens_per_batch // tm, 1)
    row = pl.BlockSpec((tm, D_MODEL), lambda i: (i, 0))
    mod = _mod_spec(per_batch, tm, tpb)
    return pl.pallas_call(
        _ffn_kernel,
        grid=(n // tm,),
        in_specs=[row, mod, mod, mod, _full_spec((1, D_MODEL)), _full_spec(w_in.shape), _full_spec(w_out.shape)],
        out_specs=row,
        out_shape=jax.ShapeDtypeStruct((n, D_MODEL), F32),
        compiler_params=_cparams("arbitrary"),
        name="ffn",
    )(x2, sc, sh, gt, g.reshape(1, -1), w_in, w_out)


HIGHEST = lax.Precision.HIGHEST
NT_DIMS = (((1,), (1,)), ((), ()))
NEG_BIG = -1e30
F32_TINY = float(jnp.finfo(jnp.float32).tiny)


def _seg_rmsnorm(x, seg, seg_t, gain):
    ss = jnp.dot(x * x, seg, precision=HIGHEST, preferred_element_type=F32)
    inv = lax.rsqrt(ss * (1.0 / NSA_HD) + EPS)
    return x * jnp.dot(inv, seg_t, precision=HIGHEST, preferred_element_type=F32) * gain


def _nsa_prep_kernel(q_ref, sel_ref, win_ref, seg_ref, segt_ref, qg_ref, kg_ref,
                     qh_ref, ksh_ref, vsh_ref, kwh_ref, vwh_ref, selo_ref, wino_ref):
    seg, segt = seg_ref[...], segt_ref[...]
    qn = _seg_rmsnorm(q_ref[...], seg, segt, qg_ref[...]) * (NSA_HD ** -0.5)
    for h in range(NSA_HEADS):
        qh_ref[h] = qn[:, h * NSA_HD:(h + 1) * NSA_HD]
    kg = kg_ref[...]
    for src, dst, kh_ref, vh_ref, row in ((sel_ref, selo_ref, ksh_ref, vsh_ref, 1), (win_ref, wino_ref, kwh_ref, vwh_ref, 2)):
        x = src[...]
        kn = _seg_rmsnorm(x[:, :NSA_KV_DIM], seg[:NSA_KV_DIM], segt[:, :NSA_KV_DIM], kg[row:row + 1])
        v = x[:, NSA_KV_DIM:]
        dst[:, :NSA_KV_DIM] = kn
        dst[:, NSA_KV_DIM:] = v
        for g in range(NSA_KV_HEADS):
            kh_ref[g] = kn[:, g * NSA_HD:(g + 1) * NSA_HD].astype(BF16)
            vh_ref[g] = v[:, g * NSA_HD:(g + 1) * NSA_HD].astype(BF16)


def _seg_mats():
    c = jnp.arange(NSA_DIM)[:, None] // NSA_HD
    seg = (c == jnp.arange(LANE)[None, :]).astype(F32)
    return seg, seg.T


def _nsa_prep(q2, sel2, win2, q_norm, k_norm, B, T):
    tm = min(512, T)
    tpb = T // tm
    seg, segt = _seg_mats()
    row = lambda wd: pl.BlockSpec((tm, wd), lambda i: (i, 0))
    hm = lambda nh: pl.BlockSpec((None, nh, tm, NSA_HD), lambda i: (i // tpb, 0, i % tpb, 0))
    kvs = jax.ShapeDtypeStruct((B, NSA_KV_HEADS, T, NSA_HD), BF16)
    rows = jax.ShapeDtypeStruct((B * T, 2 * NSA_KV_DIM), F32)
    return pl.pallas_call(
        _nsa_prep_kernel,
        grid=(B * T // tm,),
        in_specs=[row(NSA_DIM), row(2 * NSA_KV_DIM), row(2 * NSA_KV_DIM), _full_spec(seg.shape), _full_spec(segt.shape),
                  _full_spec((1, NSA_DIM)), _full_spec((3, NSA_KV_DIM))],
        out_specs=[hm(NSA_HEADS), hm(NSA_KV_HEADS), hm(NSA_KV_HEADS), hm(NSA_KV_HEADS), hm(NSA_KV_HEADS),
                   row(2 * NSA_KV_DIM), row(2 * NSA_KV_DIM)],
        out_shape=[jax.ShapeDtypeStruct((B, NSA_HEADS, T, NSA_HD), F32), kvs, kvs, kvs, kvs, rows, rows],
        compiler_params=_cparams("arbitrary"),
        name="nsa_prep",
    )(q2, sel2, win2, seg, segt, jnp.tile(q_norm, NSA_HEADS).reshape(1, -1), jnp.tile(k_norm, (1, NSA_KV_HEADS)))


def _nsa_cmp_kernel(xk_ref, xv_ref, w1_ref, pos_ref, b1_ref, w2_ref, kn_ref, kc_ref, vc_ref, *, n_valid):
    half = CMP_STRIDE * NSA_HD
    for c, x_ref in ((0, xk_ref), (1, xv_ref)):
        x = x_ref[...]
        const = jnp.dot(pos_ref[c].astype(BF16), w1_ref[c], preferred_element_type=F32)[0:1] + b1_ref[c:c + 1]
        top = jnp.dot(x, w1_ref[c, :half, :], preferred_element_type=F32)
        bot = jnp.dot(x, w1_ref[c, half:, :], preferred_element_type=F32)
        bot_next = jnp.concatenate([bot[1:], jnp.zeros((1, CMP_HIDDEN), F32)], axis=0)
        hid = jax.nn.gelu(top + bot_next + const)
        kv = jnp.dot(hid.astype(BF16), w2_ref[c], preferred_element_type=F32)
        if c == 0:
            kv = kv * lax.rsqrt(jnp.mean(kv * kv, axis=-1, keepdims=True) + EPS) * kn_ref[...]
            kc_ref[...] = kv
        else:
            vc_ref[...] = kv.astype(BF16)


def _nsa_compress(xh, w1_bf, pos, b1, w2_bf, k_norm0):
    B, _, nh, _ = xh.shape
    blk = lambda off: pl.BlockSpec((None, None, nh, CMP_STRIDE * NSA_HD), lambda b, g: (b, off + g, 0, 0))
    full = lambda shape: pl.BlockSpec(shape, lambda b, g: (0,) * len(shape))
    out = pl.BlockSpec((None, None, nh, NSA_HD), lambda b, g: (b, g, 0, 0))
    return pl.pallas_call(
        functools.partial(_nsa_cmp_kernel, n_valid=nh - 1),
        grid=(B, NSA_KV_HEADS),
        in_specs=[blk(0), blk(NSA_KV_HEADS), full(w1_bf.shape), full((2, 8, CMP_BLOCK * NSA_HD)), full((2, CMP_HIDDEN)),
                  full(w2_bf.shape), full((1, NSA_HD))],
        out_specs=[out, out],
        out_shape=[jax.ShapeDtypeStruct((B, NSA_KV_HEADS, nh, NSA_HD), F32),
                   jax.ShapeDtypeStruct((B, NSA_KV_HEADS, nh, NSA_HD), BF16)],
        compiler_params=_cparams("arbitrary", "arbitrary"),
        name="nsa_compress",
    )(xh, xh, w1_bf, jnp.broadcast_to(pos.reshape(2, 1, -1), (2, 8, CMP_BLOCK * NSA_HD)), b1, w2_bf, k_norm0.reshape(1, -1))


NSA_TQ = 128
NSA_TK = 512


def _softmax_direct(s, mask):
    sm = jnp.where(mask[None], s, NEG_BIG)
    m = jnp.max(sm, axis=-1, keepdims=True)
    e = jnp.where(mask[None], jnp.exp(sm - m), 0.0)
    return e / jnp.maximum(jnp.sum(e, axis=-1, keepdims=True), F32_TINY)


def _top_blocks(score, n_top):
    ns = score.shape[-1]
    lane = lax.broadcasted_iota(jnp.int32, score.shape, 1).astype(F32)
    sel = jnp.zeros(score.shape, F32)
    for _ in range(n_top):
        m = jnp.max(score, axis=-1, keepdims=True)
        first = jnp.min(jnp.where(score == m, lane, float(ns)), axis=-1, keepdims=True)
        hit = lane == first
        sel = jnp.where(hit, 1.0, sel)
        score = jnp.where(hit, -jnp.inf, score)
    return sel


def _nsa_prompt_kernel(q_ref, kc_ref, vc_ref, c2s_ref, ks_ref, vs_ref, kw_ref, vw_ref, g_ref, o_ref, selexp_ref):
    R, TQ, TK = NSA_RPG, NSA_TQ, NSA_TK
    T = ks_ref.shape[0]
    nc = kc_ref.shape[0]
    i = pl.program_id(2)
    q0 = i * TQ
    q = q_ref[...].reshape(R * TQ, NSA_HD)
    qb = q.astype(BF16)
    t_row = q0 + lax.broadcasted_iota(jnp.int32, (TQ, 1), 0)

    wk = WINDOW + TQ
    ws = pl.multiple_of(jnp.maximum(q0 - WINDOW, 0), TQ)
    kw = kw_ref[pl.ds(ws, wk), :]
    vw = vw_ref[pl.ds(ws, wk), :]
    sw = lax.dot_general(qb, kw, NT_DIMS, preferred_element_type=F32).reshape(R, TQ, wk)
    kpos = ws + lax.broadcasted_iota(jnp.int32, (1, wk), 1)
    pw = _softmax_direct(sw, (kpos <= t_row) & (kpos > t_row - WINDOW))
    o_win = jnp.dot(pw.reshape(R * TQ, wk).astype(BF16), vw, preferred_element_type=F32).reshape(R, TQ, NSA_HD)

    s = lax.dot_general(q, kc_ref[...], NT_DIMS, precision=HIGHEST, preferred_element_type=F32).reshape(R, TQ, nc)
    cmp_end = lax.broadcasted_iota(jnp.int32, (1, nc), 1) * CMP_STRIDE + (CMP_BLOCK - 1)
    p = _softmax_direct(s, cmp_end <= t_row)
    o_cmp = jnp.dot(p.reshape(R * TQ, nc).astype(BF16), vc_ref[...], preferred_element_type=F32).reshape(R, TQ, NSA_HD)
    imp = jnp.dot(jnp.sum(p, axis=0), c2s_ref[...], precision=HIGHEST, preferred_element_type=F32)
    ns = imp.shape[-1]
    blk = lax.broadcasted_iota(jnp.int32, (1, ns), 1)
    cur = t_row // SEL_BLOCK
    forced = (blk == 0) | (blk == cur) | (blk == cur - 1)
    score = jnp.where(forced, SEL_FORCE, jnp.where(blk * SEL_BLOCK <= t_row, imp, -SEL_FORCE))
    sel = _top_blocks(score, min(N_SELECT, ns)).astype(BF16)
    gs = jax.nn.sigmoid(g_ref[...])
    o_cw = [gs[:, r:r + 1] * o_cmp[r] + gs[:, 2 * R + r:2 * R + r + 1] * o_win[r] for r in range(R)]
    per_tile = TK // SEL_BLOCK
    key_blk = lax.broadcasted_iota(jnp.int32, (ns, TK), 1) // SEL_BLOCK
    row_blk = lax.broadcasted_iota(jnp.int32, (ns, TK), 0)
    for j in range(T // TK):
        expand = (key_blk + j * per_tile == row_blk).astype(BF16)
        selexp_ref[j] = jnp.dot(sel, expand, preferred_element_type=F32)

    def body(j, carry):
        m, l, acc = carry
        start = pl.multiple_of(j * TK, TK)
        k = ks_ref[pl.ds(start, TK), :]
        v = vs_ref[pl.ds(start, TK), :]
        sj = lax.dot_general(qb, k, NT_DIMS, preferred_element_type=F32).reshape(R, TQ, TK)
        kpos = start + lax.broadcasted_iota(jnp.int32, (1, TK), 1)
        mask = (selexp_ref[j] > 0.5) & (kpos <= t_row)
        sm = jnp.where(mask[None], sj, NEG_BIG)
        m_new = jnp.maximum(m, jnp.max(sm, axis=-1, keepdims=True))
        alpha = jnp.exp(m - m_new)
        pj = jnp.where(mask[None], jnp.exp(sm - m_new), 0.0)
        l = alpha * l + jnp.sum(pj, axis=-1, keepdims=True)
        pv = jnp.dot(pj.reshape(R * TQ, TK).astype(BF16), v, preferred_element_type=F32).reshape(R, TQ, NSA_HD)
        return m_new, l, alpha * acc + pv

    carry0 = (jnp.full((R, TQ, 1), NEG_BIG, F32), jnp.zeros((R, TQ, 1), F32), jnp.zeros((R, TQ, NSA_HD), F32))
    _, l, acc = lax.fori_loop(0, (q0 + TQ + TK - 1) // TK, body, carry0)
    o_sel = acc / jnp.maximum(l, F32_TINY)
    for r in range(R):
        o_ref[:, r * NSA_HD:(r + 1) * NSA_HD] = o_cw[r] + gs[:, R + r:R + r + 1] * o_sel[r]


def _cmp_to_sel_padded(nh, ns):
    i = jnp.arange(nh)[:, None] * CMP_STRIDE
    j = jnp.arange(ns)[None, :] * SEL_BLOCK
    ov = jnp.minimum(i + CMP_BLOCK, j + SEL_BLOCK) - jnp.maximum(i, j)
    return jnp.clip(ov, 0, None).astype(F32) / CMP_BLOCK


def _nsa_prompt_attn(qh, kc, vc, ksh, vsh, kwh, vwh, gates4):
    B, _, T, _ = qh.shape
    nh = kc.shape[2]
    ns = T // SEL_BLOCK
    c2s = _cmp_to_sel_padded(nh, ns)
    kv = lambda rows: pl.BlockSpec((None, None, rows, NSA_HD), lambda b, g, i: (b, g, 0, 0))
    return pl.pallas_call(
        _nsa_prompt_kernel,
        grid=(B, NSA_KV_HEADS, T // NSA_TQ),
        in_specs=[pl.BlockSpec((None, NSA_RPG, NSA_TQ, NSA_HD), lambda b, g, i: (b, g, i, 0)),
                  kv(nh), kv(nh), pl.BlockSpec(c2s.shape, lambda b, g, i: (0, 0)),
                  kv(T), kv(T), kv(T), kv(T),
                  pl.BlockSpec((None, None, NSA_TQ, 16), lambda b, g, i: (b, g, i, 0))],
        out_specs=pl.BlockSpec((None, NSA_TQ, NSA_RPG * NSA_HD), lambda b, g, i: (b, i, g)),
        out_shape=jax.ShapeDtypeStruct((B, T, NSA_DIM), F32),
        scratch_shapes=[pltpu.VMEM((T // NSA_TK, NSA_TQ, NSA_TK), F32)],
        compiler_params=_cparams("arbitrary", "arbitrary", "arbitrary"),
        name="nsa_prompt_attn",
    )(qh, kc, vc, c2s, ksh, vsh, kwh, vwh, gates4)


def _nsa_prompt_pallas(q2, cmp2, sel2, win2, gates2, p, wb, B, T):
    qh, ksh, vsh, kwh, vwh, sel_rows, win_rows = _nsa_prep(q2, sel2, win2, p['nsa_q_norm'], p['nsa_k_norm'], B, T)
    nh = T // CMP_STRIDE
    xh = cmp2.astype(BF16).reshape(B, nh, CMP_STRIDE, 2 * NSA_KV_HEADS, NSA_HD)
    xh = xh.transpose(0, 3, 1, 2, 4).reshape(B, 2 * NSA_KV_HEADS, nh, CMP_STRIDE * NSA_HD)
    kc, vc = _nsa_compress(xh, wb['cmp_w1'], p['nsa_cmp_pos'], p['nsa_cmp_b1'], wb['cmp_w2'], p['nsa_k_norm'][0])
    gates4 = gates2[:, :NSA_KV_HEADS * 16].reshape(B, T, NSA_KV_HEADS, 16).transpose(0, 2, 1, 3)
    o = _nsa_prompt_attn(qh, kc, vc, ksh, vsh, kwh, vwh, gates4)
    shp = (B, T, 2, NSA_KV_HEADS, NSA_HD)
    return o, (cmp2.reshape(shp), sel_rows.reshape(shp), win_rows.reshape(shp)[:, -WINDOW:])


HALF_COLS = CMP_STRIDE * 2 * NSA_KV_DIM
PAGE_HALVES = PAGE_SIZE // CMP_STRIDE
N_PAIR = 2 * NSA_KV_HEADS // 2
CMP_PAGES = 32
SEL_PAGES = 8


def _half_proj_body(x, w_ref, o_ref):
    for pr in range(N_PAIR):
        lhs = jnp.concatenate([x[:, s * 2 * NSA_KV_DIM + pr * LANE:s * 2 * NSA_KV_DIM + (pr + 1) * LANE]
                               for s in range(CMP_STRIDE)], axis=1)
        o_ref[:, pr * 4 * CMP_HIDDEN:(pr + 1) * 4 * CMP_HIDDEN] = jnp.dot(lhs, w_ref[pr], preferred_element_type=F32)


def _half_proj_paged_kernel(pt_ref, *refs):
    pages, (w_ref, o_ref) = refs[:CMP_PAGES], refs[CMP_PAGES:]
    x = jnp.concatenate([pg[...].astype(BF16) for pg in pages], axis=0)
    _half_proj_body(x, w_ref, o_ref)


def _half_proj_rows_kernel(x_ref, w_ref, o_ref):
    _half_proj_body(x_ref[...].astype(BF16), w_ref, o_ref)


def _pair_weights(w1):
    top, bot = w1[:, :CMP_STRIDE], w1[:, CMP_STRIDE:]
    tb = jnp.concatenate([top, bot], axis=-1)
    z = jnp.zeros_like(tb)
    par0 = jnp.concatenate([tb, z], axis=-1)
    par1 = jnp.concatenate([z, tb], axis=-1)
    w = jnp.stack([par0, par1], axis=2)
    w = w.reshape(2, CMP_STRIDE * 2 * NSA_HD, 4 * CMP_HIDDEN)
    return jnp.stack([w[0], w[0], w[1], w[1]], axis=0).astype(BF16)


def _half_proj_paged(pool4, layer, page_table, w_pair):
    B, n_pages = page_table.shape
    m = CMP_PAGES * PAGE_HALVES
    page_spec = lambda n: pl.BlockSpec((None, None, PAGE_HALVES, HALF_COLS),
                                       lambda b, j, pt: (layer, pt[b, j * CMP_PAGES + n], 0, 0))
    grid_spec = pltpu.PrefetchScalarGridSpec(
        num_scalar_prefetch=1, grid=(B, n_pages // CMP_PAGES),
        in_specs=[page_spec(n) for n in range(CMP_PAGES)] + [pl.BlockSpec(w_pair.shape, lambda b, j, pt: (0, 0, 0))],
        out_specs=pl.BlockSpec((None, m, 8 * 2 * CMP_HIDDEN), lambda b, j, pt: (b, j, 0)))
    return pl.pallas_call(
        _half_proj_paged_kernel, grid_spec=grid_spec,
        out_shape=jax.ShapeDtypeStruct((B, n_pages * PAGE_HALVES, 8 * 2 * CMP_HIDDEN), F32),
        compiler_params=_cparams("arbitrary", "arbitrary"),
        name="nsa_half_proj_paged",
    )(page_table, *([pool4] * CMP_PAGES), w_pair)


def _half_proj_rows(x, w_pair):
    m = x.shape[0]
    return pl.pallas_call(
        _half_proj_rows_kernel, grid=(1,),
        in_specs=[_full_spec(x.shape), _full_spec(w_pair.shape)],
        out_specs=_full_spec((m, 8 * 2 * CMP_HIDDEN)),
        out_shape=jax.ShapeDtypeStruct((m, 8 * 2 * CMP_HIDDEN), F32),
        compiler_params=_cparams("arbitrary"),
        name="nsa_half_proj_new",
    )(x, w_pair)


def _cmp_finish_kernel(tk_ref, tv_ref, nk_ref, nv_ref, w1_ref, pos_ref, b1_ref, w2_ref, kn_ref, kc_ref, vc_ref):
    for c, t_ref, n_ref in ((0, tk_ref, nk_ref), (1, tv_ref, nv_ref)):
        const = jnp.dot(pos_ref[c].astype(BF16), w1_ref[c], preferred_element_type=F32)[0:1] + b1_ref[c:c + 1]
        tb = t_ref[...]
        top = tb[:, :CMP_HIDDEN]
        bot_next = jnp.concatenate([tb[1:, CMP_HIDDEN:], n_ref[0:1, CMP_HIDDEN:]], axis=0)
        hid = jax.nn.gelu(top + bot_next + const)
        kv = jnp.dot(hid.astype(BF16), w2_ref[c], preferred_element_type=F32)
        if c == 0:
            kc_ref[...] = kv * lax.rsqrt(jnp.mean(kv * kv, axis=-1, keepdims=True) + EPS) * kn_ref[...]
        else:
            vc_ref[...] = kv.astype(BF16)


def _cmp_finish(tb_past, tb_new, w1_bf, pos, b1, w2_bf, k_norm0):
    B, nh, _ = tb_past.shape
    wd = 2 * CMP_HIDDEN
    full = lambda shape: pl.BlockSpec(shape, lambda b, g: (0,) * len(shape))
    past = lambda off: pl.BlockSpec((None, nh, wd), lambda b, g: (b, 0, off + g))
    new = lambda off: pl.BlockSpec((None, 8, wd), lambda b, g: (b, 0, off + g))
    out = pl.BlockSpec((None, None, nh, NSA_HD), lambda b, g: (b, g, 0, 0))
    return pl.pallas_call(
        _cmp_finish_kernel, grid=(B, NSA_KV_HEADS),
        in_specs=[past(0), past(NSA_KV_HEADS), new(0), new(NSA_KV_HEADS), full(w1_bf.shape),
                  full((2, 8, CMP_BLOCK * NSA_HD)), full((2, CMP_HIDDEN)), full(w2_bf.shape), full((1, NSA_HD))],
        out_specs=[out, out],
        out_shape=[jax.ShapeDtypeStruct((B, NSA_KV_HEADS, nh, NSA_HD), F32),
                   jax.ShapeDtypeStruct((B, NSA_KV_HEADS, nh, NSA_HD), BF16)],
        compiler_params=_cparams("arbitrary", "arbitrary"),
        name="nsa_cmp_finish",
    )(tb_past, tb_past, tb_new, tb_new, w1_bf, jnp.broadcast_to(pos.reshape(2, 1, -1), (2, 8, CMP_BLOCK * NSA_HD)),
      b1, w2_bf, k_norm0.reshape(1, -1))


def _nsa_sample_kernel(pt_ref, *refs, T, ns_pad):
    pages = refs[:SEL_PAGES]
    (q_ref, kc_ref, vc_ref, c2s_ref, newsel_ref, wincache_ref, newwin_ref, gate_ref, o_ref,
     sel_scr, m_scr, l_scr, acc_scr, ocmp_scr, owin_scr) = refs[SEL_PAGES:]
    H, G, R, D = NSA_HEADS, NSA_KV_HEADS, NSA_RPG, NSA_HD
    rows, gr = H * T, R * T
    j = pl.program_id(1)
    n_steps = pl.num_programs(1)
    tk = SEL_PAGES * PAGE_SIZE
    row_t = PAST_LEN + lax.broadcasted_iota(jnp.int32, (gr, 1), 0) % T
    rs = lambda g: slice(g * gr, (g + 1) * gr)
    kT = lambda x, g: x[g * D:(g + 1) * D]
    vT = lambda x, g: x[G * D + g * D:G * D + (g + 1) * D]

    @pl.when(j == 0)
    def _():
        q = q_ref[...].reshape(rows, D)
        qb = q.astype(BF16)
        nc = kc_ref.shape[1]
        t_q = PAST_LEN + lax.broadcasted_iota(jnp.int32, (T, 1), 0)
        cmp_end = lax.broadcasted_iota(jnp.int32, (1, nc), 1) * CMP_STRIDE + (CMP_BLOCK - 1)
        blk = lax.broadcasted_iota(jnp.int32, (1, ns_pad), 1)
        cur = t_q // SEL_BLOCK
        forced = (blk == 0) | (blk == cur) | (blk == cur - 1)
        valid = blk * SEL_BLOCK <= t_q
        wc = wincache_ref[...].astype(BF16)
        nw = newwin_ref[...].astype(BF16)
        kpos = jnp.concatenate([PAST_LEN - WINDOW + lax.broadcasted_iota(jnp.int32, (1, WINDOW), 1),
                                PAST_LEN + lax.broadcasted_iota(jnp.int32, (1, T), 1)], axis=1)
        win_mask = (kpos <= row_t) & (kpos > row_t - WINDOW)
        for g in range(G):
            qg = q[rs(g)]
            s = lax.dot_general(qg, kc_ref[g], NT_DIMS, precision=HIGHEST, preferred_element_type=F32).reshape(R, T, nc)
            p = _softmax_direct(s, cmp_end <= t_q)
            ocmp_scr[rs(g), :] = jnp.dot(p.reshape(gr, nc).astype(BF16), vc_ref[g], preferred_element_type=F32)
            imp = jnp.dot(jnp.sum(p, axis=0), c2s_ref[...], precision=HIGHEST, preferred_element_type=F32)
            score = jnp.where(forced, SEL_FORCE, jnp.where(valid, imp, -SEL_FORCE))
            sel = _top_blocks(score, N_SELECT)
            sel_scr[rs(g), :] = jnp.broadcast_to(sel[None], (R, T, ns_pad)).reshape(gr, ns_pad)
            sw = jnp.concatenate([jnp.dot(qb[rs(g)], kT(wc, g), preferred_element_type=F32),
                                  lax.dot_general(qb[rs(g)], nw[:, g * D:(g + 1) * D], NT_DIMS,
                                                  preferred_element_type=F32)], axis=1)
            pw = _softmax_direct(sw[None], win_mask)[0].astype(BF16)
            owin_scr[rs(g), :] = (lax.dot_general(pw[:, :WINDOW], vT(wc, g), NT_DIMS, preferred_element_type=F32)
                                  + jnp.dot(pw[:, WINDOW:], nw[:, G * D + g * D:G * D + (g + 1) * D],
                                            preferred_element_type=F32))
        m_scr[...] = jnp.full(m_scr.shape, NEG_BIG, F32)
        l_scr[...] = jnp.zeros(l_scr.shape, F32)
        acc_scr[...] = jnp.zeros(acc_scr.shape, F32)

    def flash(g, s, v_dot, mask):
        sm = jnp.where(mask, s, NEG_BIG)
        m_old = m_scr[rs(g), :]
        m_new = jnp.maximum(m_old, jnp.max(sm, axis=-1, keepdims=True))
        alpha = jnp.exp(m_old - m_new)
        p = jnp.where(mask, jnp.exp(sm - m_new), 0.0)
        l_scr[rs(g), :] = alpha * l_scr[rs(g), :] + jnp.sum(p, axis=-1, keepdims=True)
        acc_scr[rs(g), :] = alpha * acc_scr[rs(g), :] + v_dot(p.astype(BF16))
        m_scr[rs(g), :] = m_new

    qb = q_ref[...].reshape(rows, D).astype(BF16)
    kv = jnp.concatenate([pg[...].astype(BF16) for pg in pages], axis=1)
    per_step = tk // SEL_BLOCK
    key_blk = lax.broadcasted_iota(jnp.int32, (ns_pad, tk), 1) // SEL_BLOCK + j * per_step
    expand = (key_blk == lax.broadcasted_iota(jnp.int32, (ns_pad, tk), 0)).astype(BF16)
    member = jnp.dot(sel_scr[...].astype(BF16), expand, preferred_element_type=F32) > 0.5
    for g in range(G):
        flash(g, jnp.dot(qb[rs(g)], kT(kv, g), preferred_element_type=F32),
              lambda p, g=g: lax.dot_general(p, vT(kv, g), NT_DIMS, preferred_element_type=F32), member[rs(g)])

    @pl.when(j == n_steps - 1)
    def _():
        new_blk = PAST_LEN // SEL_BLOCK
        kpos = PAST_LEN + lax.broadcasted_iota(jnp.int32, (1, T), 1)
        ns_rows = newsel_ref[...].astype(BF16)
        for g in range(G):
            flash(g, lax.dot_general(qb[rs(g)], ns_rows[:, g * D:(g + 1) * D], NT_DIMS, preferred_element_type=F32),
                  lambda p, g=g: jnp.dot(p, ns_rows[:, G * D + g * D:G * D + (g + 1) * D], preferred_element_type=F32),
                  (sel_scr[rs(g), new_blk:new_blk + 1] > 0.5) & (kpos <= row_t))
        o_sel = acc_scr[...] / jnp.maximum(l_scr[...], F32_TINY)
        gs = jax.nn.sigmoid(gate_ref[...])
        o = gs[:, 0:1] * ocmp_scr[...] + gs[:, 1:2] * o_sel + gs[:, 2:3] * owin_scr[...]
        for h in range(H):
            o_ref[:, h * D:(h + 1) * D] = o[h * T:(h + 1) * T]


def _nsa_sample_attn(qh, kc, vc, pool_sel, layer, page_table, new_sel, win_cache, new_win, gate_rows):
    B, H, T, D = qh.shape
    n_pages = page_table.shape[1]
    nc = kc.shape[2]
    ns = -(-(PAST_LEN + T) // SEL_BLOCK)
    ns_pad = -(-ns // LANE) * LANE
    c2s = _cmp_to_sel_padded(nc, ns_pad)
    rows = H * T
    cst = lambda shape: pl.BlockSpec(shape, lambda b, j, pt: (0,) * len(shape))
    per_b = lambda shape: pl.BlockSpec((None,) + shape, lambda b, j, pt: (b,) + (0,) * len(shape))
    page_spec = lambda n: pl.BlockSpec((None, None, 2 * NSA_KV_DIM, PAGE_SIZE),
                                       lambda b, j, pt: (layer, pt[b, j * SEL_PAGES + n], 0, 0))
    grid_spec = pltpu.PrefetchScalarGridSpec(
        num_scalar_prefetch=1, grid=(B, n_pages // SEL_PAGES),
        in_specs=[page_spec(n) for n in range(SEL_PAGES)] + [
            per_b((H, T, D)), per_b((NSA_KV_HEADS, nc, D)), per_b((NSA_KV_HEADS, nc, D)), cst(c2s.shape),
            per_b((T, 2 * NSA_KV_DIM)), per_b((2 * NSA_KV_DIM, WINDOW)), per_b((T, 2 * NSA_KV_DIM)), per_b((rows, 3))],
        out_specs=per_b((T, NSA_DIM)),
        scratch_shapes=[pltpu.VMEM((rows, ns_pad), F32), pltpu.VMEM((rows, 1), F32), pltpu.VMEM((rows, 1), F32),
                        pltpu.VMEM((rows, D), F32), pltpu.VMEM((rows, D), F32), pltpu.VMEM((rows, D), F32)])
    return pl.pallas_call(
        functools.partial(_nsa_sample_kernel, T=T, ns_pad=ns_pad), grid_spec=grid_spec,
        out_shape=jax.ShapeDtypeStruct((B, T, NSA_DIM), F32),
        compiler_params=_cparams("arbitrary", "arbitrary"),
        name="nsa_sample_attn",
    )(page_table, *([pool_sel] * SEL_PAGES), qh, kc, vc, c2s, new_sel, win_cache, new_win, gate_rows)


def _nsa_sample_pallas(q2, cmp2, sel2, win2, gates2, p, wb, B, T, layer, cache_cmp_kv, cache_sel_kv, win_buf, page_table):
    qh, _, _, _, _, sel_rows, win_rows = _nsa_prep(q2, sel2, win2, p['nsa_q_norm'], p['nsa_k_norm'], B, T)
    depth, n_pool = cache_cmp_kv.shape[:2]
    pool_cmp = cache_cmp_kv.reshape(depth, n_pool, PAGE_HALVES, HALF_COLS)
    tb_past = _half_proj_paged(pool_cmp, layer, page_table, wb['cmp_pair'])
    new_half = jnp.pad(cmp2.reshape(B, 1, T * 2 * NSA_KV_DIM), ((0, 0), (0, 7), (0, HALF_COLS - T * 2 * NSA_KV_DIM)))
    tb_new = _half_proj_rows(new_half.reshape(B * 8, HALF_COLS), wb['cmp_pair']).reshape(B, 8, -1)
    kc, vc = _cmp_finish(tb_past, tb_new, wb['cmp_w1'], p['nsa_cmp_pos'], p['nsa_cmp_b1'], wb['cmp_w2'], p['nsa_k_norm'][0])
    pool_sel = cache_sel_kv.transpose(0, 1, 3, 4, 5, 2).reshape(depth, n_pool, 2 * NSA_KV_DIM, PAGE_SIZE)
    win_cache = win_buf.transpose(0, 2, 3, 4, 1).reshape(B, 2 * NSA_KV_DIM, WINDOW)
    g = gates2[:, :NSA_KV_HEADS * 16].reshape(B, T, NSA_KV_HEADS, 4, NSA_RPG)[:, :, :, :3]
    gate_rows = g.transpose(0, 2, 4, 1, 3).reshape(B, NSA_HEADS * T, 3)
    sel3, win3 = sel_rows.reshape(B, T, -1), win_rows.reshape(B, T, -1)
    o = _nsa_sample_attn(qh, kc, vc, pool_sel, layer, page_table, sel3, win_cache, win3, gate_rows)
    shp = (B, T, 2, NSA_KV_HEADS, NSA_HD)
    win_out = jnp.concatenate([win_buf, win_rows.reshape(shp)], axis=1)[:, -WINDOW:]
    return o, (cmp2.reshape(shp), sel_rows.reshape(shp), win_out)


RW_CHUNK = 64


def _heads(x):
    return jnp.stack([x[:, h * RW_HD:(h + 1) * RW_HD] for h in range(x.shape[1] // RW_HD)], axis=0)


def _unheads(x):
    return jnp.concatenate([x[h] for h in range(x.shape[0])], axis=1)


def _bmm(a, b):
    return jnp.einsum('hlm,hmn->hln', a.astype(BF16), b.astype(BF16), preferred_element_type=F32)


def _bmm_nt(a, b):
    return jnp.einsum('hlk,hmk->hlm', a.astype(BF16), b.astype(BF16), preferred_element_type=F32)


def _bmm_tn(a, b):
    return jnp.einsum('hlv,hlk->hvk', a.astype(BF16), b.astype(BF16), preferred_element_type=F32)


def _rwkv_kernel(x_ref, shift_ref, s0_ref, mu_ref, w0_ref, a0_ref, w2_ref, a2_ref, g2_ref, kk_ref, ka_ref, rk_ref,
                 lng_ref, lnb_ref, o_ref, sfin_ref, prev_scr, s_scr):
    L = x_ref.shape[0]
    c = pl.program_id(1)

    @pl.when(c == 0)
    def _():
        prev_scr[...] = shift_ref[...]
        s_scr[...] = s0_ref[...]

    x = x_ref[...]
    prev = jnp.concatenate([prev_scr[...], x[:L - 1]], axis=0)
    prev_scr[...] = x[L - 1:]
    xs = x + mu_ref[...] * (prev - x)
    r, k, v = xs[:, :RW_DIM], xs[:, RW_DIM:2 * RW_DIM], xs[:, 2 * RW_DIM:3 * RW_DIM]
    wl = xs[:, RW_SPLITS[2]:RW_SPLITS[3]]
    al = xs[:, RW_SPLITS[3]:RW_SPLITS[4]]
    gl = xs[:, RW_SPLITS[4]:]
    w = -jax.nn.softplus(-(w0_ref[...] + jnp.dot(jnp.tanh(wl).astype(BF16), w2_ref[...], preferred_element_type=F32))) - 0.5
    logd = -jnp.exp(w)
    a = jax.nn.sigmoid(a0_ref[...] + jnp.dot(al.astype(BF16), a2_ref[...], preferred_element_type=F32))
    g = jnp.dot(jax.nn.sigmoid(gl).astype(BF16), g2_ref[...], preferred_element_type=F32)
    kk = k * kk_ref[...]
    k = k * (1.0 + (a - 1.0) * ka_ref[...])
    row = lax.broadcasted_iota(jnp.int32, (L, L), 0)
    col = lax.broadcasted_iota(jnp.int32, (L, L), 1)
    cs = jnp.dot((col <= row).astype(F32), logd, precision=HIGHEST, preferred_element_type=F32)
    g_incl, g_prev, g_inv = jnp.exp(cs), jnp.exp(cs - logd), jnp.exp(-cs)

    kk_h = _heads(kk)
    kk_h = kk_h * lax.rsqrt(jnp.sum(kk_h * kk_h, axis=-1, keepdims=True) + 1e-12)
    r_h, k_h, v_h = _heads(r), _heads(k), _heads(v)
    inv_h = _heads(g_inv)
    at = kk_h * _heads(g_prev)
    rt = r_h * _heads(g_incl)
    bt = -(kk_h * _heads(a)) * inv_h
    kt = k_h * inv_h
    ar = jnp.concatenate([at, rt], axis=1)
    bk = jnp.concatenate([bt, kt], axis=1)
    gram = _bmm_nt(ar, bk)
    strict, incl = (col < row)[None], (col <= row)[None]
    a_ab = jnp.where(strict, gram[:, :L, :L], 0.0)
    a_ak = jnp.where(strict, gram[:, :L, L:], 0.0)
    a_rbk = jnp.concatenate([jnp.where(incl, gram[:, L:, :L], 0.0), jnp.where(incl, gram[:, L:, L:], 0.0)], axis=2)
    tinv = jnp.where((col == row)[None], 1.0, a_ab)
    pw = a_ab
    n_sq, span = 0, 2
    while span < L:
        n_sq, span = n_sq + 1, span * 2
    for _ in range(n_sq):
        pw = _bmm(pw, pw)
        tinv = tinv + _bmm(tinv, pw)

    s0 = s_scr[...]
    ars = _bmm_nt(ar, s0)
    u = _bmm(tinv, ars[:, :L] + _bmm(a_ak, v_h))
    uv = jnp.concatenate([u, v_h], axis=1)
    o = ars[:, L:] + _bmm(a_rbk, uv)
    s_new = (s0 + _bmm_tn(uv, bk)) * _heads(g_incl[L - 1:])
    s_scr[...] = s_new

    @pl.when(c == pl.num_programs(1) - 1)
    def _():
        sfin_ref[...] = s_new

    mean = jnp.mean(o, axis=-1, keepdims=True)
    var = jnp.mean(jnp.square(o - mean), axis=-1, keepdims=True)
    on = _unheads((o - mean) * lax.rsqrt(var + RW_GN_EPS)) * lng_ref[...] + lnb_ref[...]
    bonus = _unheads(jnp.sum(r_h * k_h * _heads(rk_ref[...]), axis=-1, keepdims=True) * v_h)
    o_ref[...] = (on + bonus) * g


def _rwkv7_pallas(p_rw, shift_state, wkv_state, p, wb):
    B, T, _ = p_rw.shape
    L = min(RW_CHUNK, T)
    assert T % L == 0
    vec = lambda name: p[name].reshape(1, -1)
    full = lambda shape: pl.BlockSpec(shape, lambda b, c: (0,) * len(shape))
    st = pl.BlockSpec((None, RW_HEADS, RW_HD, RW_HD), lambda b, c: (b, 0, 0, 0))
    o, s_fin = pl.pallas_call(
        _rwkv_kernel,
        grid=(B, T // L),
        in_specs=[pl.BlockSpec((None, L, RW_IN), lambda b, c: (b, c, 0)),
                  pl.BlockSpec((None, 1, RW_IN), lambda b, c: (b, 0, 0)), st,
                  full((1, RW_IN)), full((1, RW_DIM)), full((1, RW_DIM)),
                  full((RW_W_LORA, RW_DIM)), full((RW_A_LORA, RW_DIM)), full((RW_G_LORA, RW_DIM)),
                  full((1, RW_DIM)), full((1, RW_DIM)), full((1, RW_DIM)), full((1, RW_DIM)), full((1, RW_DIM))],
        out_specs=[pl.BlockSpec((None, L, RW_DIM), lambda b, c: (b, c, 0)), st],
        out_shape=[jax.ShapeDtypeStruct((B, T, RW_DIM), F32),
                   jax.ShapeDtypeStruct((B, RW_HEADS, RW_HD, RW_HD), F32)],
        scratch_shapes=[pltpu.VMEM((1, RW_IN), F32), pltpu.VMEM((RW_HEADS, RW_HD, RW_HD), F32)],
        compiler_params=_cparams("arbitrary", "arbitrary"),
        name="rwkv7_mix",
    )(p_rw, shift_state.reshape(B, 1, RW_IN), wkv_state, vec('rw_mu'), vec('rw_w0'), vec('rw_a0'),
      wb['rw_w2'], wb['rw_a2'], wb['rw_g2'], vec('rw_k_k'), vec('rw_k_a'), vec('rw_r_k'), vec('rw_lnx_g'), vec('rw_lnx_b'))
    return o, p_rw[:, -1], s_fin


SSM_GW = SSM_RPG * SSM_HD


def _mamba_kernel(z_ref, xbc_ref, dt_ref, conv0_ref, s0_ref, cw_ref, cb_ref, dtb_ref, alog_ref, dvec_ref, ng_ref,
                  hexp_ref, o_ref, convf_ref, sfin_ref, tail_scr, s_scr):
    L = z_ref.shape[0]
    c = pl.program_id(1)
    last = pl.num_programs(1) - 1
    G, N, K = SSM_GROUPS, SSM_STATE, SSM_CONV

    @pl.when(c == 0)
    def _():
        tail_scr[...] = conv0_ref[...]
        s_scr[...] = s0_ref[...]

    ext = jnp.concatenate([tail_scr[...], xbc_ref[...]], axis=0)
    new_tail = ext[L:L + K - 1]
    tail_scr[...] = new_tail
    conv = cb_ref[...] + sum(ext[i:i + L] * cw_ref[i:i + 1] for i in range(K))
    act = jax.nn.silu(conv)
    xs, bm, cm = act[:, :SSM_DIM], act[:, SSM_DIM:SSM_DIM + G * N], act[:, SSM_DIM + G * N:]

    dt = jax.nn.softplus(dt_ref[...] + dtb_ref[...])
    da = dt * (-jnp.exp(alog_ref[...]))
    row = lax.broadcasted_iota(jnp.int32, (L, L), 0)
    col = lax.broadcasted_iota(jnp.int32, (L, L), 1)
    causal = col <= row
    a_cs = jnp.dot(causal.astype(F32), da, precision=HIGHEST, preferred_element_type=F32)
    a_last = a_cs[L - 1:]
    a_cs_t = a_cs.T
    per_head = jnp.concatenate([dt, dt * jnp.exp(a_last - a_cs), jnp.exp(a_cs)], axis=0)
    wide = jnp.dot(per_head, hexp_ref[...], precision=HIGHEST, preferred_element_type=F32)
    xdt_b = (xs * wide[:L]).astype(BF16)
    xds = (xs * wide[L:2 * L]).astype(BF16)
    dec_in = wide[2 * L:]
    st_head = lax.broadcasted_iota(jnp.int32, (SSM_GW, LANE), 0) // SSM_HD
    st_lane = lax.broadcasted_iota(jnp.int32, (SSM_GW, LANE), 1)
    dec_chunk = jnp.exp(a_last)

    ys = []
    for g in range(G):
        bg = bm[:, g * N:(g + 1) * N].astype(BF16)
        cg = cm[:, g * N:(g + 1) * N].astype(BF16)
        cb = lax.dot_general(cg, bg, NT_DIMS, preferred_element_type=F32)
        s_in = s_scr[g]
        y_off = lax.dot_general(cg, s_in.astype(BF16), NT_DIMS, preferred_element_type=F32)
        cols = slice(g * SSM_GW, (g + 1) * SSM_GW)
        y_diag = []
        for r in range(SSM_RPG):
            h = g * SSM_RPG + r
            seg = a_cs[:, h:h + 1] - a_cs_t[h:h + 1, :]
            w = (cb * jnp.exp(jnp.where(causal, seg, -jnp.inf))).astype(BF16)
            y_diag.append(jnp.dot(w, xdt_b[:, h * SSM_HD:(h + 1) * SSM_HD], preferred_element_type=F32))
        ys.append(jnp.concatenate(y_diag, axis=1) + y_off * dec_in[:, cols])
        states = lax.dot_general(xds[:, cols], bg, (((0,), (0,)), ((), ())), preferred_element_type=F32)
        dec_col = jnp.sum(jnp.where(st_head + g * SSM_RPG == st_lane, dec_chunk, 0.0), axis=-1, keepdims=True)
        s_scr[g] = s_in * dec_col + states
    y = jnp.concatenate(ys, axis=1) + dvec_ref[...] * xs
    y = y * jax.nn.silu(z_ref[...])
    o_ref[...] = y * lax.rsqrt(jnp.mean(y * y, axis=-1, keepdims=True) + EPS) * ng_ref[...]

    @pl.when(c == last)
    def _():
        convf_ref[...] = new_tail
        sfin_ref[...] = s_scr[...]


def _mamba2_pallas(z, xbc, dt, conv_state, ssm_state, p):
    B, T, _ = z.shape
    L = SSM_CHUNK if T % SSM_CHUNK == 0 else T
    pad = lambda v: jnp.pad(v, (0, LANE - SSM_HEADS)).reshape(1, LANE)
    hexp = (jnp.arange(LANE)[:, None] == jnp.arange(SSM_DIM)[None, :] // SSM_HD).astype(F32)
    full = lambda shape: pl.BlockSpec(shape, lambda b, c: (0,) * len(shape))
    tile = lambda wd: pl.BlockSpec((None, L, wd), lambda b, c: (b, c, 0))
    per_b = lambda shape: pl.BlockSpec((None,) + shape, lambda b, c: (b,) + (0,) * len(shape))
    st_shape = (SSM_GROUPS, SSM_GW, SSM_STATE)
    y, conv_f, s_fin = pl.pallas_call(
        _mamba_kernel,
        grid=(B, T // L),
        in_specs=[tile(SSM_DIM), tile(SSM_XBC), tile(LANE), per_b((SSM_CONV - 1, SSM_XBC)), per_b(st_shape),
                  full((SSM_CONV, SSM_XBC)), full((1, SSM_XBC)), full((1, LANE)), full((1, LANE)), full((1, SSM_DIM)),
                  full((1, SSM_DIM)), full((LANE, SSM_DIM))],
        out_specs=[tile(SSM_DIM), per_b((SSM_CONV - 1, SSM_XBC)), per_b(st_shape)],
        out_shape=[jax.ShapeDtypeStruct((B, T, SSM_DIM), F32),
                   jax.ShapeDtypeStruct((B, SSM_CONV - 1, SSM_XBC), F32),
                   jax.ShapeDtypeStruct((B,) + st_shape, F32)],
        scratch_shapes=[pltpu.VMEM((SSM_CONV - 1, SSM_XBC), F32), pltpu.VMEM(st_shape, F32)],
        compiler_params=_cparams("arbitrary", "arbitrary"),
        name="mamba2_mix",
    )(z, xbc, dt, conv_state, ssm_state.reshape((B,) + st_shape), p['ssm_conv_w'], p['ssm_conv_b'].reshape(1, -1),
      pad(p['ssm_dt_bias']), pad(p['ssm_a_log']), jnp.repeat(p['ssm_d'], SSM_HD).reshape(1, -1),
      p['ssm_norm_g'].reshape(1, -1), hexp)
    return y, conv_f, s_fin.reshape(B, SSM_HEADS, SSM_HD, SSM_STATE)


def _rmsnorm(x, g):
    y = x * lax.rsqrt(jnp.mean(x * x, axis=-1, keepdims=True) + EPS)
    return y * g


def _masked_softmax(s, mask):
    s = jnp.where(mask, s, -jnp.inf)
    m = jnp.max(s, axis=-1, keepdims=True)
    m = jnp.where(jnp.isfinite(m), m, 0.0)
    e = jnp.exp(s - m)
    return e / jnp.maximum(jnp.sum(e, axis=-1, keepdims=True), jnp.finfo(F32).tiny)


def _rwkv7_mix(proj, shift_state, wkv_state, p):
    B, T, _ = proj.shape
    prev = jnp.concatenate([shift_state[:, None], proj[:, :-1]], axis=1)
    xs = proj + p['rw_mu'] * (prev - proj)
    r, k, v, wl, al, gl = jnp.split(xs, RW_SPLITS, axis=-1)
    w = -jax.nn.softplus(-(p['rw_w0'] + jnp.tanh(wl) @ p['rw_w2'])) - 0.5
    decay = jnp.exp(-jnp.exp(w))
    a = jax.nn.sigmoid(p['rw_a0'] + al @ p['rw_a2'])
    g = jax.nn.sigmoid(gl) @ p['rw_g2']
    kk = k * p['rw_k_k']
    k = k * (1.0 + (a - 1.0) * p['rw_k_a'])
    hd = lambda t: t.reshape(B, T, RW_HEADS, RW_HD)
    r, k, v, kk, a, decay = hd(r), hd(k), hd(v), hd(kk), hd(a), hd(decay)
    kk = kk * lax.rsqrt(jnp.sum(kk * kk, axis=-1, keepdims=True) + 1e-12)

    def step(S, inp):
        r_t, w_t, k_t, v_t, kk_t, a_t = inp
        sa = jnp.einsum('bhvk,bhk->bhv', S, kk_t)
        S = S * w_t[:, :, None, :] - sa[..., None] * (kk_t * a_t)[:, :, None, :] + v_t[..., None] * k_t[:, :, None, :]
        return S, jnp.einsum('bhvk,bhk->bhv', S, r_t)

    seqs = tuple(jnp.swapaxes(t, 0, 1) for t in (r, decay, k, v, kk, a))
    S_fin, o = lax.scan(step, wkv_state, seqs)
    o = jnp.swapaxes(o, 0, 1)
    mean = jnp.mean(o, axis=-1, keepdims=True)
    var = jnp.mean(jnp.square(o - mean), axis=-1, keepdims=True)
    o = ((o - mean) * lax.rsqrt(var + RW_GN_EPS)).reshape(B, T, RW_DIM) * p['rw_lnx_g'] + p['rw_lnx_b']
    bonus = jnp.sum(r * k * p['rw_r_k'], axis=-1, keepdims=True) * v
    o = (o + bonus.reshape(B, T, RW_DIM)) * g
    return o, proj[:, -1], S_fin


def _ssd_chunked(x, dt, A, Bm, Cm, s0):
    Bsz, T = x.shape[:2]
    L = SSM_CHUNK if T % SSM_CHUNK == 0 else T
    nc = T // L
    ch = lambda t: t.reshape((Bsz, nc, L) + t.shape[2:])
    xdt = ch(x * dt[..., None])
    a_cs = jnp.cumsum(ch(dt * A), axis=2)
    Bc, Cc = ch(Bm), ch(Cm)
    seg = a_cs[:, :, :, None] - a_cs[:, :, None, :]
    causal = jnp.tril(jnp.ones((L, L), dtype=bool))[:, :, None, None]
    decay_ls = jnp.exp(jnp.where(causal, seg, -jnp.inf))
    cb = jnp.einsum('bclgn,bcsgn->bclsg', Cc, Bc)
    y_diag = jnp.einsum('bclsgr,bcsgrp->bclgrp', cb[..., None] * decay_ls, xdt)
    decay_s = jnp.exp(a_cs[:, :, -1:] - a_cs)
    states = jnp.einsum('bclgn,bclgr,bclgrp->bcgrpn', Bc, decay_s, xdt)
    chunk_decay = jnp.exp(a_cs[:, :, -1])

    def step(s, inp):
        st, dec = inp
        return s * dec[..., None, None] + st, s

    s_fin, s_in = lax.scan(step, s0, (jnp.swapaxes(states, 0, 1), jnp.swapaxes(chunk_decay, 0, 1)))
    s_in = jnp.swapaxes(s_in, 0, 1)
    y_off = jnp.einsum('bclgn,bcgrpn,bclgr->bclgrp', Cc, s_in, jnp.exp(a_cs))
    return (y_diag + y_off).reshape(x.shape), s_fin


def _mamba2_mix(z, xbc, dt, conv_state, ssm_state, p):
    B, T, _ = z.shape
    xpad = jnp.concatenate([conv_state, xbc], axis=1)
    conv = p['ssm_conv_b'] + sum(xpad[:, i:i + T] * p['ssm_conv_w'][i] for i in range(SSM_CONV))
    xbc_c = jax.nn.silu(conv)
    xs, Bm, Cm = jnp.split(xbc_c, (SSM_DIM, SSM_DIM + SSM_GROUPS * SSM_STATE), axis=-1)
    x = xs.reshape(B, T, SSM_GROUPS, SSM_RPG, SSM_HD)
    Bm = Bm.reshape(B, T, SSM_GROUPS, SSM_STATE)
    Cm = Cm.reshape(B, T, SSM_GROUPS, SSM_STATE)
    dt = jax.nn.softplus(dt + p['ssm_dt_bias']).reshape(B, T, SSM_GROUPS, SSM_RPG)
    A = -jnp.exp(p['ssm_a_log']).reshape(SSM_GROUPS, SSM_RPG)
    s0 = ssm_state.reshape(B, SSM_GROUPS, SSM_RPG, SSM_HD, SSM_STATE)
    y, s_fin = _ssd_chunked(x, dt, A, Bm, Cm, s0)
    y = y + p['ssm_d'].reshape(SSM_GROUPS, SSM_RPG)[:, :, None] * x
    y = y.reshape(B, T, SSM_DIM) * jax.nn.silu(z)
    y = _rmsnorm(y, p['ssm_norm_g'])
    return y, xpad[:, -(SSM_CONV - 1):], s_fin.reshape(B, SSM_HEADS, SSM_HD, SSM_STATE)


def _nsa_project(q, kv, gates, p):
    B, T, _ = q.shape
    kc, vc, ks, vs, kw, vw = jnp.split(kv, 6, axis=-1)
    kvh = lambda t: t.reshape(B, T, NSA_KV_HEADS, NSA_HD)
    q = _rmsnorm(q.reshape(B, T, NSA_HEADS, NSA_HD), p['nsa_q_norm']).reshape(B, T, NSA_KV_HEADS, NSA_RPG, NSA_HD)
    cmp_rows = jnp.stack([kvh(kc), kvh(vc)], axis=2)
    sel_rows = jnp.stack([_rmsnorm(kvh(ks), p['nsa_k_norm'][1]), kvh(vs)], axis=2)
    win_rows = jnp.stack([_rmsnorm(kvh(kw), p['nsa_k_norm'][2]), kvh(vw)], axis=2)
    g = jax.nn.sigmoid(gates).reshape(B, T, 3, NSA_KV_HEADS, NSA_RPG, 1)
    return q, cmp_rows, sel_rows, win_rows, g


def _half_proj(rows, w1):
    B, T = rows.shape[:2]
    h = rows.reshape(B, T // CMP_STRIDE, CMP_STRIDE, 2, NSA_KV_HEADS, NSA_HD)
    top = jnp.einsum('bnsckd,csdh->bnckh', h, w1[:, :CMP_STRIDE])
    bot = jnp.einsum('bnsckd,csdh->bnckh', h, w1[:, CMP_STRIDE:])
    return top, bot


def _compress_finish(top, bot, p):
    w1 = p['nsa_cmp_w1']
    const = jnp.einsum('csd,csdh->ch', p['nsa_cmp_pos'], w1) + p['nsa_cmp_b1']
    hid = jax.nn.gelu(top[:, :-1] + bot[:, 1:] + const[:, None, :])
    kv = jnp.einsum('bnckh,chd->bnckd', hid, p['nsa_cmp_w2'])
    return _rmsnorm(kv[:, :, 0], p['nsa_k_norm'][0]), kv[:, :, 1]


def _cmp_to_sel(nc, ns):
    i = jnp.arange(nc)[:, None] * CMP_STRIDE
    j = jnp.arange(ns)[None, :] * SEL_BLOCK
    ov = jnp.minimum(i + CMP_BLOCK, j + SEL_BLOCK) - jnp.maximum(i, j)
    return jnp.clip(ov, 0, None).astype(F32) / CMP_BLOCK


def _fetch_rows(kv, pos):
    b = jnp.arange(kv.shape[0])[:, None, None, None]
    g = jnp.arange(NSA_KV_HEADS)[None, :, None, None]
    return kv[b, jnp.clip(pos, 0, kv.shape[1] - 1), :, g, :]


def _fetch_paged(pool, l, page_table, new_rows, pos):
    b = jnp.arange(page_table.shape[0])[:, None, None, None]
    g = jnp.arange(NSA_KV_HEADS)[None, :, None, None]
    pp = jnp.clip(pos, 0, PAST_LEN - 1)
    phys = page_table[b, pp // PAGE_SIZE]
    past = pool[l, phys, pp % PAGE_SIZE, :, g, :]
    new = _fetch_rows(new_rows, pos - PAST_LEN)
    return jnp.where((pos < PAST_LEN)[..., None, None], past, new)


def _nsa_cmp_sel(q, t, kc, vc, fetch, ns):
    B, Tq = q.shape[:2]
    scale = NSA_HD ** -0.5
    nc = kc.shape[1]
    s = jnp.einsum('bqgrd,bngd->bgrqn', q, kc) * scale
    cmp_end = jnp.arange(nc) * CMP_STRIDE + CMP_BLOCK - 1
    p_cmp = _masked_softmax(s, cmp_end[None, :] <= t[:, None])
    o_cmp = jnp.einsum('bgrqn,bngd->bqgrd', p_cmp, vc)
    imp = jnp.einsum('bgrqn,ns->bgqs', p_cmp, _cmp_to_sel(nc, ns))
    blk = jnp.arange(ns)[None, :]
    cur = t[:, None] // SEL_BLOCK
    forced = (blk == 0) | (blk == cur) | (blk == cur - 1)
    valid = blk * SEL_BLOCK <= t[:, None]
    score = jnp.where(forced, SEL_FORCE, jnp.where(valid, imp, -SEL_FORCE))
    n_top = min(N_SELECT, ns)
    _, idx = lax.top_k(score, n_top)
    pos = (idx[..., None] * SEL_BLOCK + jnp.arange(SEL_BLOCK)).reshape(B, NSA_KV_HEADS, Tq, n_top * SEL_BLOCK)
    kvs = fetch(pos)
    s2 = jnp.einsum('bqgrd,bgqkd->bgrqk', q, kvs[..., 0, :]) * scale
    p_sel = _masked_softmax(s2, (pos <= t[:, None])[:, :, None])
    o_sel = jnp.einsum('bgrqk,bgqkd->bqgrd', p_sel, kvs[..., 1, :])
    return o_cmp, o_sel


def _nsa_window(q, t, kw, vw, kpos):
    s = jnp.einsum('bqgrd,bngd->bgrqn', q, kw) * (NSA_HD ** -0.5)
    mask = (kpos[None, :] <= t[:, None]) & (kpos[None, :] > t[:, None] - WINDOW) & (kpos[None, :] >= 0)
    p = _masked_softmax(s, mask)
    return jnp.einsum('bgrqn,bngd->bqgrd', p, vw)


def _nsa_merge(o, g):
    B, T = o.shape[:2]
    return jnp.sum(g * o, axis=2).reshape(B, T, NSA_DIM)


def _nsa_prompt(q, kv, gates, p):
    q, cmp_rows, sel_rows, win_rows, g = _nsa_project(q, kv, gates, p)
    B, T = q.shape[:2]
    top, bot = _half_proj(cmp_rows, p['nsa_cmp_w1'])
    kc, vc = _compress_finish(top, bot, p)
    ns = -(-T // SEL_BLOCK)
    fetch = lambda pos: _fetch_rows(sel_rows, pos)
    win_pad = jnp.pad(win_rows, ((0, 0), (WINDOW, 0), (0, 0), (0, 0), (0, 0)))
    nqb = T // Q_BLOCK
    qb = jnp.swapaxes(q.reshape(B, nqb, Q_BLOCK, NSA_KV_HEADS, NSA_RPG, NSA_HD), 0, 1)

    def body(args):
        q_blk, i = args
        t = i * Q_BLOCK + jnp.arange(Q_BLOCK)
        o_cmp, o_sel = _nsa_cmp_sel(q_blk, t, kc, vc, fetch, ns)
        kv_w = lax.dynamic_slice_in_dim(win_pad, i * Q_BLOCK, WINDOW + Q_BLOCK, axis=1)
        kpos = i * Q_BLOCK - WINDOW + jnp.arange(WINDOW + Q_BLOCK)
        o_win = _nsa_window(q_blk, t, kv_w[:, :, 0], kv_w[:, :, 1], kpos)
        return jnp.stack([o_cmp, o_sel, o_win], axis=2)

    o = lax.map(body, (qb, jnp.arange(nqb)))
    o = jnp.swapaxes(o, 0, 1).reshape(B, T, 3, NSA_KV_HEADS, NSA_RPG, NSA_HD)
    return _nsa_merge(o, g), (cmp_rows, sel_rows, win_pad[:, -WINDOW:])


def _nsa_sample(q, kv, gates, p, l, cache_cmp_kv, cache_sel_kv, win_buf, page_table):
    q, cmp_rows, sel_rows, win_rows, g = _nsa_project(q, kv, gates, p)
    B, T = q.shape[:2]
    past_cmp = cache_cmp_kv[l, page_table].reshape(B, PAST_LEN, 2, NSA_KV_HEADS, NSA_HD)
    new_cmp = jnp.pad(cmp_rows, ((0, 0), (0, (-T) % CMP_STRIDE), (0, 0), (0, 0), (0, 0)))
    tp, bp = _half_proj(past_cmp, p['nsa_cmp_w1'])
    tn, bn = _half_proj(new_cmp, p['nsa_cmp_w1'])
    kc, vc = _compress_finish(jnp.concatenate([tp, tn], axis=1), jnp.concatenate([bp, bn], axis=1), p)
    t = PAST_LEN + jnp.arange(T)
    ns = -(-(PAST_LEN + T) // SEL_BLOCK)
    fetch = lambda pos: _fetch_paged(cache_sel_kv, l, page_table, sel_rows, pos)
    o_cmp, o_sel = _nsa_cmp_sel(q, t, kc, vc, fetch, ns)
    kv_w = jnp.concatenate([win_buf, win_rows], axis=1)
    kpos = PAST_LEN - WINDOW + jnp.arange(WINDOW + T)
    o_win = _nsa_window(q, t, kv_w[:, :, 0], kv_w[:, :, 1], kpos)
    o = jnp.stack([o_cmp, o_sel, o_win], axis=2)
    return _nsa_merge(o, g), (cmp_rows, sel_rows, kv_w[:, -WINDOW:])


def _prep_weights(p):
    w_in = p['w_in']
    o_ssm = RW_IN
    o_nsa = RW_IN + SSM_IN
    o_gate = o_nsa + NSA_IN
    pad = lambda w, n: jnp.pad(w, ((0, 0), (0, n - w.shape[1])))
    w_rw = w_in[:, :RW_IN]
    w_ssm = jnp.concatenate([w_in[:, o_ssm:o_ssm + SSM_DIM + SSM_XBC],
                             pad(w_in[:, o_ssm + SSM_DIM + SSM_XBC:o_nsa], LANE)], axis=1)
    o_g = o_nsa + NSA_DIM + 6 * NSA_KV_DIM
    wg = w_in[:, o_g:o_gate].reshape(D_MODEL, 3, NSA_KV_HEADS, NSA_RPG).transpose(0, 2, 1, 3)
    wg = jnp.pad(wg.reshape(D_MODEL, NSA_KV_HEADS, 3 * NSA_RPG), ((0, 0), (0, 0), (0, 16 - 3 * NSA_RPG)))
    w_nsa = jnp.concatenate([w_in[:, o_nsa:o_g], pad(wg.reshape(D_MODEL, NSA_KV_HEADS * 16), LANE),
                             w_in[:, o_gate:]], axis=1)
    bf = lambda w: w.astype(BF16)
    return dict(w_ada=bf(p['w_ada']), w_rw=bf(w_rw), w_ssm=bf(w_ssm), w_nsa=bf(w_nsa),
                w_br_rw=bf(p['w_br_rw']), w_br_ssm=bf(p['w_br_ssm']), w_br_nsa=bf(p['w_br_nsa']),
                w_out=bf(p['w_out']), w_ffn_in=bf(p['w_ffn_in']), w_ffn_out=bf(p['w_ffn_out']),
                rw_w2=bf(p['rw_w2']), rw_a2=bf(p['rw_a2']), rw_g2=bf(p['rw_g2']),
                cmp_w1=bf(p['nsa_cmp_w1'].reshape(2, CMP_BLOCK * NSA_HD, CMP_HIDDEN)), cmp_w2=bf(p['nsa_cmp_w2']),
                cmp_pair=_pair_weights(p['nsa_cmp_w1']))


def _trunk_layer(x, c, p, wb, rw_shift, rw_wkv, conv_state, ssm_state, nsa_fn, per_batch):
    B, T, _ = x.shape
    n = B * T
    mod = _ada(c, wb['w_ada'], p['b_ada'])
    if per_batch:
        mods = [m.reshape(B, 1, D_MODEL) for m in jnp.split(mod, 6, axis=-1)]
    else:
        mods = [jnp.repeat(m, T, axis=0) for m in jnp.split(mod, 6, axis=-1)]
    sh1, sc1, gt1, sh2, sc2, gt2 = mods
    x2 = x.reshape(n, D_MODEL)
    (p_rw,) = _norm_proj(x2, sc1, sh1, p['ln1'], wb['w_rw'], (RW_IN,), per_batch, T, "proj_rw")
    z, xbc, dt = _norm_proj(x2, sc1, sh1, p['ln1'], wb['w_ssm'], (SSM_DIM, SSM_XBC, LANE), per_batch, T, "proj_ssm")
    kvw = 2 * NSA_KV_DIM
    q, cmp2, sel2, win2, gates, p_gate = _norm_proj(x2, sc1, sh1, p['ln1'], wb['w_nsa'],
                                                    (NSA_DIM, kvw, kvw, kvw, LANE, N_BRANCH * D_MODEL), per_batch, T,
                                                    "proj_nsa")
    r3 = lambda t: t.reshape(B, T, t.shape[-1])
    o_rw, rw_shift, rw_wkv = _rwkv7_pallas(r3(p_rw), rw_shift, rw_wkv, p, wb)
    o_ssm, conv_state, ssm_state = _mamba2_pallas(r3(z), r3(xbc), r3(dt), conv_state, ssm_state, p)
    o_nsa, (cmp_rows, sel_rows, win_buf) = nsa_fn(q, cmp2, sel2, win2, gates, p, wb, B, T)
    x2 = _merge(x2, gt1, o_rw.reshape(n, -1), o_ssm.reshape(n, -1), o_nsa.reshape(n, -1), p_gate,
                wb['w_br_rw'], wb['w_br_ssm'], wb['w_br_nsa'], wb['w_out'], per_batch, T)
    x2 = _ffn(x2, sc2, sh2, gt2, p['ln2'], wb['w_ffn_in'], wb['w_ffn_out'], per_batch, T)
    return x2.reshape(B, T, D_MODEL), (rw_shift, rw_wkv, conv_state, ssm_state, cmp_rows, sel_rows, win_buf)


def kernel(x_prompt, x_sample, cache_cmp_kv, cache_sel_kv, cache_win_kv, state_rwkv_shift, state_rwkv_wkv,
           state_ssm_conv, state_ssm, page_table, c_prompt, c_sample, w_ada, b_ada, ln1, ln2, w_in,
           rw_mu, rw_w0, rw_w2, rw_a0, rw_a2, rw_g2, rw_k_k, rw_k_a, rw_r_k, rw_lnx_g, rw_lnx_b,
           ssm_conv_w, ssm_conv_b, ssm_dt_bias, ssm_a_log, ssm_d, ssm_norm_g,
           nsa_q_norm, nsa_k_norm, nsa_cmp_pos, nsa_cmp_w1, nsa_cmp_b1, nsa_cmp_w2,
           w_br_rw, w_br_ssm, w_br_nsa, w_out, w_ffn_in, w_ffn_out):
    params = dict(w_ada=w_ada, b_ada=b_ada, ln1=ln1, ln2=ln2, w_in=w_in,
                  rw_mu=rw_mu, rw_w0=rw_w0, rw_w2=rw_w2, rw_a0=rw_a0, rw_a2=rw_a2, rw_g2=rw_g2,
                  rw_k_k=rw_k_k, rw_k_a=rw_k_a, rw_r_k=rw_r_k, rw_lnx_g=rw_lnx_g, rw_lnx_b=rw_lnx_b,
                  ssm_conv_w=ssm_conv_w, ssm_conv_b=ssm_conv_b, ssm_dt_bias=ssm_dt_bias, ssm_a_log=ssm_a_log,
                  ssm_d=ssm_d, ssm_norm_g=ssm_norm_g, nsa_q_norm=nsa_q_norm, nsa_k_norm=nsa_k_norm,
                  nsa_cmp_pos=nsa_cmp_pos, nsa_cmp_w1=nsa_cmp_w1, nsa_cmp_b1=nsa_cmp_b1, nsa_cmp_w2=nsa_cmp_w2,
                  w_br_rw=w_br_rw, w_br_ssm=w_br_ssm, w_br_nsa=w_br_nsa, w_out=w_out,
                  w_ffn_in=w_ffn_in, w_ffn_out=w_ffn_out)
    bp = x_prompt.shape[0]
    depth = w_in.shape[0]
    zero_shift = jnp.zeros((bp, RW_IN), F32)
    zero_wkv = jnp.zeros((bp, RW_HEADS, RW_HD, RW_HD), F32)
    zero_conv = jnp.zeros((bp, SSM_CONV - 1, SSM_XBC), F32)
    zero_ssm = jnp.zeros((bp, SSM_HEADS, SSM_HD, SSM_STATE), F32)
    xp, xs = x_prompt, x_sample
    st_p, st_s = [], []
    for l in range(depth):
        p = {name: arr[l] for name, arr in params.items()}
        wb = _prep_weights(p)
        xp, sp_l = _trunk_layer(xp, c_prompt, p, wb, zero_shift, zero_wkv, zero_conv, zero_ssm, _nsa_prompt_pallas, True)
        nsa_s = functools.partial(_nsa_sample_pallas, layer=l, cache_cmp_kv=cache_cmp_kv, cache_sel_kv=cache_sel_kv,
                                  win_buf=cache_win_kv[l], page_table=page_table)
        xs, ss_l = _trunk_layer(xs, c_sample, p, wb, state_rwkv_shift[l], state_rwkv_wkv[l], state_ssm_conv[l],
                                state_ssm[l], nsa_s, False)
        st_p.append(sp_l)
        st_s.append(ss_l)
    sp = [jnp.stack([s[i] for s in st_p]) for i in range(7)]
    ss = [jnp.stack([s[i] for s in st_s]) for i in range(7)]
    return (xp, xs, sp[4], ss[4], sp[5], ss[5], sp[6], ss[6], sp[0], ss[0], sp[1], ss[1], sp[2], ss[2], sp[3], ss[3])
```

```python
import functools

import jax
import jax.numpy as jnp
from jax import lax
from jax.experimental import pallas as pl
from jax.experimental.pallas import tpu as pltpu

F32 = jnp.float32
BF16 = jnp.bfloat16

D_MODEL = 1024
PAST_LEN = 16384
PAGE_SIZE = 128
RW_HEADS = 16
RW_HD = 64
RW_DIM = RW_HEADS * RW_HD
RW_W_LORA = 64
RW_A_LORA = 64
RW_G_LORA = 128
RW_IN = 3 * RW_DIM + RW_W_LORA + RW_A_LORA + RW_G_LORA
RW_SPLITS = (RW_DIM, 2 * RW_DIM, 3 * RW_DIM, 3 * RW_DIM + RW_W_LORA, 3 * RW_DIM + RW_W_LORA + RW_A_LORA)
RW_GN_EPS = 64e-5
SSM_DIM = 2 * D_MODEL
SSM_HD = 64
SSM_HEADS = SSM_DIM // SSM_HD
SSM_GROUPS = 4
SSM_RPG = SSM_HEADS // SSM_GROUPS
SSM_STATE = 128
SSM_CONV = 4
SSM_CHUNK = 128
SSM_XBC = SSM_DIM + 2 * SSM_GROUPS * SSM_STATE
SSM_IN = SSM_DIM + SSM_XBC + SSM_HEADS
NSA_HEADS = 16
NSA_HD = 64
NSA_KV_HEADS = 4
NSA_RPG = NSA_HEADS // NSA_KV_HEADS
NSA_DIM = NSA_HEADS * NSA_HD
NSA_KV_DIM = NSA_KV_HEADS * NSA_HD
CMP_STRIDE = 16
CMP_BLOCK = 2 * CMP_STRIDE
CMP_HIDDEN = 128
SEL_BLOCK = 64
N_SELECT = 16
WINDOW = 512
Q_BLOCK = 64
SEL_FORCE = 1e6
NSA_IN = NSA_DIM + 6 * NSA_KV_DIM + 3 * NSA_HEADS
N_BRANCH = 3
FFN_HIDDEN = ((8 * D_MODEL // 3 + 255) // 256) * 256
EPS = 1e-6

LANE = 128
VMEM_LIMIT = 56 * 1024 * 1024


def _cparams(*sem):
    return pltpu.CompilerParams(dimension_semantics=sem, vmem_limit_bytes=VMEM_LIMIT)


def _modnorm(x, g, sc, sh):
    y = x * lax.rsqrt(jnp.mean(x * x, axis=-1, keepdims=True) + EPS)
    return (y * g) * (1.0 + sc) + sh


def _mod_spec(per_batch, tm, tiles_per_batch):
    if per_batch:
        return pl.BlockSpec((None, 1, D_MODEL), lambda i: (i // tiles_per_batch, 0, 0))
    return pl.BlockSpec((tm, D_MODEL), lambda i: (i, 0))


def _full_spec(shape):
    return pl.BlockSpec(shape, lambda i: (0,) * len(shape))


def _ada_kernel(c_ref, w_ref, b_ref, o_ref):
    o_ref[...] = jnp.dot(c_ref[...].astype(BF16), w_ref[...], preferred_element_type=F32) + b_ref[...]


def _ada(c, w_bf, b):
    n, tn = c.shape[0], 1536
    return pl.pallas_call(
        _ada_kernel,
        grid=(6 * D_MODEL // tn,),
        in_specs=[pl.BlockSpec((n, D_MODEL), lambda j: (0, 0)),
                  pl.BlockSpec((D_MODEL, tn), lambda j: (0, j)),
                  pl.BlockSpec((1, tn), lambda j: (0, j))],
        out_specs=pl.BlockSpec((n, tn), lambda j: (0, j)),
        out_shape=jax.ShapeDtypeStruct((n, 6 * D_MODEL), F32),
        compiler_params=_cparams("arbitrary"),
        name="ada_mod",
    )(c, w_bf, b.reshape(1, -1))


def _norm_proj_kernel(x_ref, sc_ref, sh_ref, g_ref, w_ref, *o_refs, splits):
    h = _modnorm(x_ref[...], g_ref[...], sc_ref[...], sh_ref[...]).astype(BF16)
    for o_ref, (a, b) in zip(o_refs, splits):
        o_ref[...] = jnp.dot(h, w_ref[:, a:b], preferred_element_type=F32)


def _norm_proj(x2, sc, sh, g, w_bf, widths, per_batch, tokens_per_batch, name):
    n = x2.shape[0]
    tm = min(512, n)
    splits, a = [], 0
    for wd in widths:
        splits.append((a, a + wd))
        a += wd
    assert a == w_bf.shape[1] and n % tm == 0
    tpb = max(tokens_per_batch // tm, 1)
    return pl.pallas_call(
        functools.partial(_norm_proj_kernel, splits=tuple(splits)),
        grid=(n // tm,),
        in_specs=[pl.BlockSpec((tm, D_MODEL), lambda i: (i, 0)),
                  _mod_spec(per_batch, tm, tpb), _mod_spec(per_batch, tm, tpb),
                  _full_spec((1, D_MODEL)), _full_spec(w_bf.shape)],
        out_specs=[pl.BlockSpec((tm, wd), lambda i: (i, 0)) for wd in widths],
        out_shape=[jax.ShapeDtypeStruct((n, wd), F32) for wd in widths],
        compiler_params=_cparams("arbitrary"),
        name=name,
    )(x2, sc, sh, g.reshape(1, -1), w_bf)


def _merge_kernel(x_ref, gt_ref, orw_ref, ossm_ref, onsa_ref, gate_ref, wrw_ref, wssm_ref, wnsa_ref, wout_ref,
                  o_ref):
    gate = jax.nn.sigmoid(gate_ref[...])
    br = (jnp.dot(orw_ref[...].astype(BF16), wrw_ref[...], preferred_element_type=F32),
          jnp.dot(ossm_ref[...].astype(BF16), wssm_ref[...], preferred_element_type=F32),
          jnp.dot(onsa_ref[...].astype(BF16), wnsa_ref[...], preferred_element_type=F32))
    merged = sum(gate[:, i * D_MODEL:(i + 1) * D_MODEL] * br[i] for i in range(N_BRANCH))
    y = jnp.dot(merged.astype(BF16), wout_ref[...], preferred_element_type=F32)
    o_ref[...] = x_ref[...] + gt_ref[...] * y


def _merge(x2, gt, o_rw, o_ssm, o_nsa, gate, w_rw, w_ssm, w_nsa, w_out, per_batch, tokens_per_batch):
    n = x2.shape[0]
    tm = min(512, n)
    tpb = max(tokens_per_batch // tm, 1)
    row = lambda wd: pl.BlockSpec((tm, wd), lambda i: (i, 0))
    return pl.pallas_call(
        _merge_kernel,
        grid=(n // tm,),
        in_specs=[row(D_MODEL), _mod_spec(per_batch, tm, tpb), row(RW_DIM), row(SSM_DIM), row(NSA_DIM),
                  row(N_BRANCH * D_MODEL), _full_spec(w_rw.shape), _full_spec(w_ssm.shape),
                  _full_spec(w_nsa.shape), _full_spec(w_out.shape)],
        out_specs=row(D_MODEL),
        out_shape=jax.ShapeDtypeStruct((n, D_MODEL), F32),
        compiler_params=_cparams("arbitrary"),
        name="branch_merge",
    )(x2, gt, o_rw, o_ssm, o_nsa, gate, w_rw, w_ssm, w_nsa, w_out)


FFN_CHUNK = 256


def _ffn_kernel(x_ref, sc_ref, sh_ref, gt_ref, g_ref, win_ref, wout_ref, o_ref):
    x = x_ref[...]
    h = _modnorm(x, g_ref[...], sc_ref[...], sh_ref[...]).astype(BF16)
    acc = jnp.zeros(x.shape, F32)
    for c in range(FFN_HIDDEN // FFN_CHUNK):
        a = c * FFN_CHUNK
        up = jnp.dot(h, win_ref[:, a:a + FFN_CHUNK], preferred_element_type=F32)
        gf = jnp.dot(h, win_ref[:, FFN_HIDDEN + a:FFN_HIDDEN + a + FFN_CHUNK], preferred_element_type=F32)
        act = (jax.nn.silu(gf) * up).astype(BF16)
        acc = acc + jnp.dot(act, wout_ref[a:a + FFN_CHUNK, :], preferred_element_type=F32)
    o_ref[...] = x + gt_ref[...] * acc


def _ffn(x2, sc, sh, gt, g, w_in, w_out, per_batch, tokens_per_batch):
    n = x2.shape[0]
    tm = min(512, n)
    tpb = max(tokens_per_batch // tm, 1)
    row = pl.BlockSpec((tm, D_MODEL), lambda i: (i, 0))
    mod = _mod_spec(per_batch, tm, tpb)
    return pl.pallas_call(
        _ffn_kernel,
        grid=(n // tm,),
        in_specs=[row, mod, mod, mod, _full_spec((1, D_MODEL)), _full_spec(w_in.shape), _full_spec(w_out.shape)],
        out_specs=row,
        out_shape=jax.ShapeDtypeStruct((n, D_MODEL), F32),
        compiler_params=_cparams("arbitrary"),
        name="ffn",
    )(x2, sc, sh, gt, g.reshape(1, -1), w_in, w_out)


HIGHEST = lax.Precision.HIGHEST
NT_DIMS = (((1,), (1,)), ((), ()))
NEG_BIG = -1e30
F32_TINY = float(jnp.finfo(jnp.float32).tiny)


def _split3(x):
    hi = x.astype(BF16)
    r = x - hi.astype(F32)
    mid = r.astype(BF16)
    return hi, mid, (r - mid.astype(F32)).astype(BF16)


def _dot_f32_nt(a3, b3):
    pairs = ((2, 0), (0, 2), (1, 1), (1, 0), (0, 1), (0, 0))
    return sum(lax.dot_general(a3[i], b3[j], NT_DIMS, preferred_element_type=F32) for i, j in pairs)


def _dot_f32_exact_rhs(a, b_bf):
    return sum(jnp.dot(t, b_bf, preferred_element_type=F32) for t in reversed(_split3(a)))


def _seg_rmsnorm(x, seg, seg_t, gain):
    ss = jnp.dot(x * x, seg, precision=HIGHEST, preferred_element_type=F32)
    inv = lax.rsqrt(ss * (1.0 / NSA_HD) + EPS)
    return x * jnp.dot(inv, seg_t, precision=HIGHEST, preferred_element_type=F32) * gain


def _nsa_prep_kernel(q_ref, sel_ref, win_ref, seg_ref, segt_ref, qg_ref, kg_ref,
                     qh_ref, ksh_ref, vsh_ref, kwh_ref, vwh_ref, selo_ref, wino_ref):
    seg, segt = seg_ref[...], segt_ref[...]
    qn = _seg_rmsnorm(q_ref[...], seg, segt, qg_ref[...]) * (NSA_HD ** -0.5)
    for h in range(NSA_HEADS):
        qh_ref[h] = qn[:, h * NSA_HD:(h + 1) * NSA_HD]
    kg = kg_ref[...]
    for src, dst, kh_ref, vh_ref, row in ((sel_ref, selo_ref, ksh_ref, vsh_ref, 1), (win_ref, wino_ref, kwh_ref, vwh_ref, 2)):
        x = src[...]
        kn = _seg_rmsnorm(x[:, :NSA_KV_DIM], seg[:NSA_KV_DIM], segt[:, :NSA_KV_DIM], kg[row:row + 1])
        v = x[:, NSA_KV_DIM:]
        dst[:, :NSA_KV_DIM] = kn
        dst[:, NSA_KV_DIM:] = v
        ones_col = (lax.broadcasted_iota(jnp.int32, (x.shape[0], NSA_HD), 1) == 0).astype(BF16)
        for g in range(NSA_KV_HEADS):
            kh_ref[g] = kn[:, g * NSA_HD:(g + 1) * NSA_HD].astype(BF16)
            vh_ref[g] = jnp.concatenate([v[:, g * NSA_HD:(g + 1) * NSA_HD].astype(BF16), ones_col], axis=1)


def _seg_mats():
    c = jnp.arange(NSA_DIM)[:, None] // NSA_HD
    seg = (c == jnp.arange(LANE)[None, :]).astype(F32)
    return seg, seg.T


def _nsa_prep(q2, sel2, win2, q_norm, k_norm, B, T):
    tm = min(512, T)
    tpb = T // tm
    seg, segt = _seg_mats()
    row = lambda wd: pl.BlockSpec((tm, wd), lambda i: (i, 0))
    hm = lambda nh, wd=NSA_HD: pl.BlockSpec((None, nh, tm, wd), lambda i: (i // tpb, 0, i % tpb, 0))
    ks = jax.ShapeDtypeStruct((B, NSA_KV_HEADS, T, NSA_HD), BF16)
    vs = jax.ShapeDtypeStruct((B, NSA_KV_HEADS, T, 2 * NSA_HD), BF16)
    rows = jax.ShapeDtypeStruct((B * T, 2 * NSA_KV_DIM), F32)
    return pl.pallas_call(
        _nsa_prep_kernel,
        grid=(B * T // tm,),
        in_specs=[row(NSA_DIM), row(2 * NSA_KV_DIM), row(2 * NSA_KV_DIM), _full_spec(seg.shape), _full_spec(segt.shape),
                  _full_spec((1, NSA_DIM)), _full_spec((3, NSA_KV_DIM))],
        out_specs=[hm(NSA_HEADS), hm(NSA_KV_HEADS), hm(NSA_KV_HEADS, 2 * NSA_HD), hm(NSA_KV_HEADS),
                   hm(NSA_KV_HEADS, 2 * NSA_HD), row(2 * NSA_KV_DIM), row(2 * NSA_KV_DIM)],
        out_shape=[jax.ShapeDtypeStruct((B, NSA_HEADS, T, NSA_HD), F32), ks, vs, ks, vs, rows, rows],
        compiler_params=_cparams("arbitrary"),
        name="nsa_prep",
    )(q2, sel2, win2, seg, segt, jnp.tile(q_norm, NSA_HEADS).reshape(1, -1), jnp.tile(k_norm, (1, NSA_KV_HEADS)))


def _nsa_cmp_kernel(xk_ref, xv_ref, w1_ref, pos_ref, b1_ref, w2_ref, kn_ref, kc_ref, vc_ref, *, n_valid):
    half = CMP_STRIDE * NSA_HD
    for c, x_ref in ((0, xk_ref), (1, xv_ref)):
        x = x_ref[...]
        const = jnp.dot(pos_ref[c].astype(BF16), w1_ref[c], preferred_element_type=F32)[0:1] + b1_ref[c:c + 1]
        top = jnp.dot(x, w1_ref[c, :half, :], preferred_element_type=F32)
        bot = jnp.dot(x, w1_ref[c, half:, :], preferred_element_type=F32)
        bot_next = jnp.concatenate([bot[1:], jnp.zeros((1, CMP_HIDDEN), F32)], axis=0)
        hid = jax.nn.gelu(top + bot_next + const)
        kv = jnp.dot(hid.astype(BF16), w2_ref[c], preferred_element_type=F32)
        if c == 0:
            kv = kv * lax.rsqrt(jnp.mean(kv * kv, axis=-1, keepdims=True) + EPS) * kn_ref[...]
            for i, term in enumerate(_split3(kv)):
                kc_ref[i] = term
        else:
            vc_ref[...] = kv.astype(BF16)


def _nsa_compress(xh, w1_bf, pos, b1, w2_bf, k_norm0):
    B, _, nh, _ = xh.shape
    blk = lambda off: pl.BlockSpec((None, None, nh, CMP_STRIDE * NSA_HD), lambda b, g: (b, off + g, 0, 0))
    full = lambda shape: pl.BlockSpec(shape, lambda b, g: (0,) * len(shape))
    out = pl.BlockSpec((None, None, nh, NSA_HD), lambda b, g: (b, g, 0, 0))
    return pl.pallas_call(
        functools.partial(_nsa_cmp_kernel, n_valid=nh - 1),
        grid=(B, NSA_KV_HEADS),
        in_specs=[blk(0), blk(NSA_KV_HEADS), full(w1_bf.shape), full((2, 8, CMP_BLOCK * NSA_HD)), full((2, CMP_HIDDEN)),
                  full(w2_bf.shape), full((1, NSA_HD))],
        out_specs=[pl.BlockSpec((None, None, 3, nh, NSA_HD), lambda b, g: (b, g, 0, 0, 0)), out],
        out_shape=[jax.ShapeDtypeStruct((B, NSA_KV_HEADS, 3, nh, NSA_HD), BF16),
                   jax.ShapeDtypeStruct((B, NSA_KV_HEADS, nh, NSA_HD), BF16)],
        compiler_params=_cparams("arbitrary", "arbitrary"),
        name="nsa_compress",
    )(xh, xh, w1_bf, jnp.broadcast_to(pos.reshape(2, 1, -1), (2, 8, CMP_BLOCK * NSA_HD)), b1, w2_bf, k_norm0.reshape(1, -1))


NSA_TQ = 128
NSA_TK = 512


def _softmax_direct(s, mask):
    sm = jnp.where(mask[None], s, NEG_BIG)
    m = jnp.max(sm, axis=-1, keepdims=True)
    e = jnp.where(mask[None], jnp.exp(sm - m), 0.0)
    return e / jnp.maximum(jnp.sum(e, axis=-1, keepdims=True), F32_TINY)


def _top_blocks(score, n_top):
    ns = score.shape[-1]
    lane = lax.broadcasted_iota(jnp.int32, score.shape, 1).astype(F32)
    sel = jnp.zeros(score.shape, F32)
    for _ in range(n_top):
        m = jnp.max(score, axis=-1, keepdims=True)
        first = jnp.min(jnp.where(score == m, lane, float(ns)), axis=-1, keepdims=True)
        hit = lane == first
        sel = jnp.where(hit, 1.0, sel)
        score = jnp.where(hit, -jnp.inf, score)
    return sel


def _nsa_prompt_kernel(q_ref, kc_ref, vc_ref, c2s_ref, ks_ref, vs_ref, kw_ref, vw_ref, g_ref, o_ref, bias_ref):
    R, TQ, TK = NSA_RPG, NSA_TQ, NSA_TK
    T = ks_ref.shape[0]
    nc = kc_ref.shape[1]
    i = pl.program_id(2)
    q0 = i * TQ
    D = NSA_HD
    q = q_ref[...].reshape(R * TQ, D)
    qb = q.astype(BF16)
    t_row = q0 + lax.broadcasted_iota(jnp.int32, (TQ, 1), 0)

    wk = WINDOW + TQ
    ws = pl.multiple_of(jnp.maximum(q0 - WINDOW, 0), TQ)
    kw = kw_ref[pl.ds(ws, wk), :]
    vw = vw_ref[pl.ds(ws, wk), :]
    kpos = ws + lax.broadcasted_iota(jnp.int32, (1, wk), 1)
    sw = lax.dot_general(qb, kw, NT_DIMS, preferred_element_type=F32).reshape(R, TQ, wk)
    sw = sw + jnp.where((kpos <= t_row) & (kpos > t_row - WINDOW), 0.0, NEG_BIG)[None]
    ew = jnp.exp(sw - jnp.max(sw, axis=-1, keepdims=True))
    ow = jnp.dot(ew.reshape(R * TQ, wk).astype(BF16), vw, preferred_element_type=F32).reshape(R, TQ, 2 * D)
    o_win = ow[..., :D] / ow[..., D:D + 1]

    s = _dot_f32_nt(_split3(q), (kc_ref[0], kc_ref[1], kc_ref[2])).reshape(R, TQ, nc)
    cmp_end = lax.broadcasted_iota(jnp.int32, (1, nc), 1) * CMP_STRIDE + (CMP_BLOCK - 1)
    p = _softmax_direct(s, cmp_end <= t_row)
    o_cmp = jnp.dot(p.reshape(R * TQ, nc).astype(BF16), vc_ref[...], preferred_element_type=F32).reshape(R, TQ, D)
    imp = _dot_f32_exact_rhs(jnp.sum(p, axis=0), c2s_ref[...])
    gs = jax.nn.sigmoid(g_ref[...])
    o_cw = [gs[:, r:r + 1] * o_cmp[r] + gs[:, 2 * R + r:2 * R + r + 1] * o_win[r] for r in range(R)]
    ns = imp.shape[-1]
    blk = lax.broadcasted_iota(jnp.int32, (1, ns), 1)
    cur = t_row // SEL_BLOCK
    forced = (blk == 0) | (blk == cur) | (blk == cur - 1)
    score = jnp.where(forced, SEL_FORCE, jnp.where(blk * SEL_BLOCK <= t_row, imp, -SEL_FORCE))
    sel = _top_blocks(score, min(N_SELECT, ns)).astype(BF16)
    per_tile = TK // SEL_BLOCK
    key_blk = lax.broadcasted_iota(jnp.int32, (ns, TK), 1) // SEL_BLOCK
    row_blk = lax.broadcasted_iota(jnp.int32, (ns, TK), 0)
    for j in range(T // TK):
        @pl.when(j * TK < q0 + TQ)
        def _(j=j):
            expand = (key_blk + j * per_tile == row_blk).astype(BF16)
            member = jnp.dot(sel, expand, preferred_element_type=F32) > 0.5
            kpos_j = j * TK + lax.broadcasted_iota(jnp.int32, (1, TK), 1)
            bias_ref[j] = jnp.where(member & (kpos_j <= t_row), 0.0, NEG_BIG)

    def body(j, carry):
        m, acc = carry
        start = pl.multiple_of(j * TK, TK)
        k = ks_ref[pl.ds(start, TK), :]
        v = vs_ref[pl.ds(start, TK), :]
        sm = lax.dot_general(qb, k, NT_DIMS, preferred_element_type=F32).reshape(R, TQ, TK) + bias_ref[j][None]
        m_new = jnp.maximum(m, jnp.max(sm, axis=-1, keepdims=True))
        pj = jnp.exp(sm - m_new).astype(BF16)
        pv = jnp.dot(pj.reshape(R * TQ, TK), v, preferred_element_type=F32).reshape(R, TQ, 2 * D)
        return m_new, jnp.exp(m - m_new) * acc + pv

    carry0 = (jnp.full((R, TQ, 1), NEG_BIG, F32), jnp.zeros((R, TQ, 2 * D), F32))
    _, acc = lax.fori_loop(0, (q0 + TQ + TK - 1) // TK, body, carry0)
    o_sel = acc[..., :D] / acc[..., D:D + 1]
    for r in range(R):
        o_ref[:, r * D:(r + 1) * D] = o_cw[r] + gs[:, R + r:R + r + 1] * o_sel[r]


def _cmp_to_sel_padded(nh, ns):
    i = jnp.arange(nh)[:, None] * CMP_STRIDE
    j = jnp.arange(ns)[None, :] * SEL_BLOCK
    ov = jnp.minimum(i + CMP_BLOCK, j + SEL_BLOCK) - jnp.maximum(i, j)
    return (jnp.clip(ov, 0, None).astype(F32) / CMP_BLOCK).astype(BF16)


def _nsa_prompt_attn(qh, kc, vc, ksh, vsh, kwh, vwh, gates4):
    B, _, T, _ = qh.shape
    nh = kc.shape[3]
    ns = T // SEL_BLOCK
    c2s = _cmp_to_sel_padded(nh, ns)
    kv = lambda n, wd=NSA_HD: pl.BlockSpec((None, None, n, wd), lambda b, g, i: (b, g, 0, 0))
    return pl.pallas_call(
        _nsa_prompt_kernel,
        grid=(B, NSA_KV_HEADS, T // NSA_TQ),
        in_specs=[pl.BlockSpec((None, NSA_RPG, NSA_TQ, NSA_HD), lambda b, g, i: (b, g, i, 0)),
                  pl.BlockSpec((None, None, 3, nh, NSA_HD), lambda b, g, i: (b, g, 0, 0, 0)), kv(nh),
                  pl.BlockSpec(c2s.shape, lambda b, g, i: (0, 0)),
                  kv(T), kv(T, 2 * NSA_HD), kv(T), kv(T, 2 * NSA_HD),
                  pl.BlockSpec((None, None, NSA_TQ, 16), lambda b, g, i: (b, g, i, 0))],
        out_specs=pl.BlockSpec((None, NSA_TQ, NSA_RPG * NSA_HD), lambda b, g, i: (b, i, g)),
        out_shape=jax.ShapeDtypeStruct((B, T, NSA_DIM), F32),
        scratch_shapes=[pltpu.VMEM((T // NSA_TK, NSA_TQ, NSA_TK), F32)],
        compiler_params=_cparams("arbitrary", "arbitrary", "arbitrary"),
        name="nsa_prompt_attn",
    )(qh, kc, vc, c2s, ksh, vsh, kwh, vwh, gates4)


def _nsa_prompt_pallas(q2, cmp2, sel2, win2, gates2, p, wb, B, T):
    qh, ksh, vsh, kwh, vwh, sel_rows, win_rows = _nsa_prep(q2, sel2, win2, p['nsa_q_norm'], p['nsa_k_norm'], B, T)
    nh = T // CMP_STRIDE
    xh = cmp2.astype(BF16).reshape(B, nh, CMP_STRIDE, 2 * NSA_KV_HEADS, NSA_HD)
    xh = xh.transpose(0, 3, 1, 2, 4).reshape(B, 2 * NSA_KV_HEADS, nh, CMP_STRIDE * NSA_HD)
    kc, vc = _nsa_compress(xh, wb['cmp_w1'], p['nsa_cmp_pos'], p['nsa_cmp_b1'], wb['cmp_w2'], p['nsa_k_norm'][0])
    gates4 = gates2[:, :NSA_KV_HEADS * 16].reshape(B, T, NSA_KV_HEADS, 16).transpose(0, 2, 1, 3)
    o = _nsa_prompt_attn(qh, kc, vc, ksh, vsh, kwh, vwh, gates4)
    shp = (B, T, 2, NSA_KV_HEADS, NSA_HD)
    return o, (cmp2.reshape(shp), sel_rows.reshape(shp), win_rows.reshape(shp)[:, -WINDOW:])


HALF_COLS = CMP_STRIDE * 2 * NSA_KV_DIM
PAGE_HALVES = PAGE_SIZE // CMP_STRIDE
N_PAIR = 2 * NSA_KV_HEADS // 2
CMP_PAGES = 32
SEL_PAGES = 8


def _half_proj_body(x, w_ref, o_ref):
    for pr in range(N_PAIR):
        lhs = jnp.concatenate([x[:, s * 2 * NSA_KV_DIM + pr * LANE:s * 2 * NSA_KV_DIM + (pr + 1) * LANE]
                               for s in range(CMP_STRIDE)], axis=1)
        o_ref[:, pr * 4 * CMP_HIDDEN:(pr + 1) * 4 * CMP_HIDDEN] = jnp.dot(lhs, w_ref[pr], preferred_element_type=F32)


def _half_proj_paged_kernel(pt_ref, *refs):
    pages, (w_ref, o_ref) = refs[:CMP_PAGES], refs[CMP_PAGES:]
    x = jnp.concatenate([pg[...].astype(BF16) for pg in pages], axis=0)
    _half_proj_body(x, w_ref, o_ref)


def _half_proj_rows_kernel(x_ref, w_ref, o_ref):
    _half_proj_body(x_ref[...].astype(BF16), w_ref, o_ref)


def _pair_weights(w1):
    top, bot = w1[:, :CMP_STRIDE], w1[:, CMP_STRIDE:]
    tb = jnp.concatenate([top, bot], axis=-1)
    z = jnp.zeros_like(tb)
    par0 = jnp.concatenate([tb, z], axis=-1)
    par1 = jnp.concatenate([z, tb], axis=-1)
    w = jnp.stack([par0, par1], axis=2)
    w = w.reshape(2, CMP_STRIDE * 2 * NSA_HD, 4 * CMP_HIDDEN)
    return jnp.stack([w[0], w[0], w[1], w[1]], axis=0).astype(BF16)


def _half_proj_paged(pool4, layer, page_table, w_pair):
    B, n_pages = page_table.shape
    m = CMP_PAGES * PAGE_HALVES
    page_spec = lambda n: pl.BlockSpec((None, None, PAGE_HALVES, HALF_COLS),
                                       lambda b, j, pt: (layer, pt[b, j * CMP_PAGES + n], 0, 0))
    grid_spec = pltpu.PrefetchScalarGridSpec(
        num_scalar_prefetch=1, grid=(B, n_pages // CMP_PAGES),
        in_specs=[page_spec(n) for n in range(CMP_PAGES)] + [pl.BlockSpec(w_pair.shape, lambda b, j, pt: (0, 0, 0))],
        out_specs=pl.BlockSpec((None, m, 8 * 2 * CMP_HIDDEN), lambda b, j, pt: (b, j, 0)))
    return pl.pallas_call(
        _half_proj_paged_kernel, grid_spec=grid_spec,
        out_shape=jax.ShapeDtypeStruct((B, n_pages * PAGE_HALVES, 8 * 2 * CMP_HIDDEN), F32),
        compiler_params=_cparams("arbitrary", "arbitrary"),
        name="nsa_half_proj_paged",
    )(page_table, *([pool4] * CMP_PAGES), w_pair)


def _half_proj_rows(x, w_pair):
    m = x.shape[0]
    return pl.pallas_call(
        _half_proj_rows_kernel, grid=(1,),
        in_specs=[_full_spec(x.shape), _full_spec(w_pair.shape)],
        out_specs=_full_spec((m, 8 * 2 * CMP_HIDDEN)),
        out_shape=jax.ShapeDtypeStruct((m, 8 * 2 * CMP_HIDDEN), F32),
        compiler_params=_cparams("arbitrary"),
        name="nsa_half_proj_new",
    )(x, w_pair)


def _cmp_finish_kernel(tk_ref, tv_ref, nk_ref, nv_ref, w1_ref, pos_ref, b1_ref, w2_ref, kn_ref, kc_ref, vc_ref):
    for c, t_ref, n_ref in ((0, tk_ref, nk_ref), (1, tv_ref, nv_ref)):
        const = jnp.dot(pos_ref[c].astype(BF16), w1_ref[c], preferred_element_type=F32)[0:1] + b1_ref[c:c + 1]
        tb = t_ref[...]
        top = tb[:, :CMP_HIDDEN]
        bot_next = jnp.concatenate([tb[1:, CMP_HIDDEN:], n_ref[0:1, CMP_HIDDEN:]], axis=0)
        hid = jax.nn.gelu(top + bot_next + const)
        kv = jnp.dot(hid.astype(BF16), w2_ref[c], preferred_element_type=F32)
        if c == 0:
            kn = kv * lax.rsqrt(jnp.mean(kv * kv, axis=-1, keepdims=True) + EPS) * kn_ref[...]
            for i, term in enumerate(_split3(kn)):
                kc_ref[i] = term
        else:
            vc_ref[...] = kv.astype(BF16)


def _cmp_finish(tb_past, tb_new, w1_bf, pos, b1, w2_bf, k_norm0):
    B, nh, _ = tb_past.shape
    wd = 2 * CMP_HIDDEN
    full = lambda shape: pl.BlockSpec(shape, lambda b, g: (0,) * len(shape))
    past = lambda off: pl.BlockSpec((None, nh, wd), lambda b, g: (b, 0, off + g))
    new = lambda off: pl.BlockSpec((None, 8, wd), lambda b, g: (b, 0, off + g))
    out = pl.BlockSpec((None, None, nh, NSA_HD), lambda b, g: (b, g, 0, 0))
    return pl.pallas_call(
        _cmp_finish_kernel, grid=(B, NSA_KV_HEADS),
        in_specs=[past(0), past(NSA_KV_HEADS), new(0), new(NSA_KV_HEADS), full(w1_bf.shape),
                  full((2, 8, CMP_BLOCK * NSA_HD)), full((2, CMP_HIDDEN)), full(w2_bf.shape), full((1, NSA_HD))],
        out_specs=[pl.BlockSpec((None, None, 3, nh, NSA_HD), lambda b, g: (b, g, 0, 0, 0)), out],
        out_shape=[jax.ShapeDtypeStruct((B, NSA_KV_HEADS, 3, nh, NSA_HD), BF16),
                   jax.ShapeDtypeStruct((B, NSA_KV_HEADS, nh, NSA_HD), BF16)],
        compiler_params=_cparams("arbitrary", "arbitrary"),
        name="nsa_cmp_finish",
    )(tb_past, tb_past, tb_new, tb_new, w1_bf, jnp.broadcast_to(pos.reshape(2, 1, -1), (2, 8, CMP_BLOCK * NSA_HD)),
      b1, w2_bf, k_norm0.reshape(1, -1))


def _nsa_sample_kernel(pt_ref, *refs, T, ns_pad):
    pages = refs[:SEL_PAGES]
    (q_ref, kc_ref, vc_ref, c2s_ref, newsel_ref, wincache_ref, newwin_ref, gate_ref, o_ref,
     sel_scr, m_scr, l_scr, acc_scr, ocmp_scr, owin_scr) = refs[SEL_PAGES:]
    H, G, R, D = NSA_HEADS, NSA_KV_HEADS, NSA_RPG, NSA_HD
    rows, gr = H * T, R * T
    j = pl.program_id(1)
    n_steps = pl.num_programs(1)
    tk = SEL_PAGES * PAGE_SIZE
    row_t = PAST_LEN + lax.broadcasted_iota(jnp.int32, (gr, 1), 0) % T
    rs = lambda g: slice(g * gr, (g + 1) * gr)
    kT = lambda x, g: x[g * D:(g + 1) * D]
    vT = lambda x, g: x[G * D + g * D:G * D + (g + 1) * D]

    @pl.when(j == 0)
    def _():
        q = q_ref[...].reshape(rows, D)
        qb = q.astype(BF16)
        nc = kc_ref.shape[2]
        t_q = PAST_LEN + lax.broadcasted_iota(jnp.int32, (T, 1), 0)
        cmp_end = lax.broadcasted_iota(jnp.int32, (1, nc), 1) * CMP_STRIDE + (CMP_BLOCK - 1)
        blk = lax.broadcasted_iota(jnp.int32, (1, ns_pad), 1)
        cur = t_q // SEL_BLOCK
        forced = (blk == 0) | (blk == cur) | (blk == cur - 1)
        valid = blk * SEL_BLOCK <= t_q
        wc = wincache_ref[...].astype(BF16)
        nw = newwin_ref[...].astype(BF16)
        kpos = jnp.concatenate([PAST_LEN - WINDOW + lax.broadcasted_iota(jnp.int32, (1, WINDOW), 1),
                                PAST_LEN + lax.broadcasted_iota(jnp.int32, (1, T), 1)], axis=1)
        win_mask = (kpos <= row_t) & (kpos > row_t - WINDOW)
        scores = []
        for g in range(G):
            qg = q[rs(g)]
            s = _dot_f32_nt(_split3(qg), (kc_ref[g, 0], kc_ref[g, 1], kc_ref[g, 2])).reshape(R, T, nc)
            p = _softmax_direct(s, cmp_end <= t_q)
            ocmp_scr[rs(g), :] = jnp.dot(p.reshape(gr, nc).astype(BF16), vc_ref[g], preferred_element_type=F32)
            imp = _dot_f32_exact_rhs(jnp.sum(p, axis=0), c2s_ref[...])
            scores.append(jnp.where(forced, SEL_FORCE, jnp.where(valid, imp, -SEL_FORCE)))
            sw = jnp.concatenate([jnp.dot(qb[rs(g)], kT(wc, g), preferred_element_type=F32),
                                  lax.dot_general(qb[rs(g)], nw[:, g * D:(g + 1) * D], NT_DIMS,
                                                  preferred_element_type=F32)], axis=1)
            pw = _softmax_direct(sw[None], win_mask)[0].astype(BF16)
            owin_scr[rs(g), :] = (lax.dot_general(pw[:, :WINDOW], vT(wc, g), NT_DIMS, preferred_element_type=F32)
                                  + jnp.dot(pw[:, WINDOW:], nw[:, G * D + g * D:G * D + (g + 1) * D],
                                            preferred_element_type=F32))
        sel = _top_blocks(jnp.concatenate(scores, axis=0), N_SELECT).reshape(G, 1, T, ns_pad)
        sel_scr[...] = jnp.broadcast_to(sel, (G, R, T, ns_pad)).reshape(rows, ns_pad).T
        m_scr[...] = jnp.full(m_scr.shape, NEG_BIG, F32)
        l_scr[...] = jnp.zeros(l_scr.shape, F32)
        acc_scr[...] = jnp.zeros(acc_scr.shape, F32)

    def flash(g, s, v_dot, mask):
        sm = jnp.where(mask, s, NEG_BIG)
        m_old = m_scr[rs(g), :]
        m_new = jnp.maximum(m_old, jnp.max(sm, axis=-1, keepdims=True))
        alpha = jnp.exp(m_old - m_new)
        p = jnp.where(mask, jnp.exp(sm - m_new), 0.0)
        l_scr[rs(g), :] = alpha * l_scr[rs(g), :] + jnp.sum(p, axis=-1, keepdims=True)
        acc_scr[rs(g), :] = alpha * acc_scr[rs(g), :] + v_dot(p.astype(BF16))
        m_scr[rs(g), :] = m_new

    qb = q_ref[...].reshape(rows, D).astype(BF16)
    kv = jnp.concatenate([pg[...].astype(BF16) for pg in pages], axis=1)
    per_step = tk // SEL_BLOCK
    tn_dims = (((0,), (0,)), ((), ()))
    expand = (lax.broadcasted_iota(jnp.int32, (per_step, tk), 1) // SEL_BLOCK
              == lax.broadcasted_iota(jnp.int32, (per_step, tk), 0)).astype(BF16)
    sel_step = sel_scr[pl.ds(pl.multiple_of(j * per_step, per_step), per_step), :].astype(BF16)
    member = lax.dot_general(sel_step, expand, tn_dims, preferred_element_type=F32) > 0.5
    for g in range(G):
        flash(g, jnp.dot(qb[rs(g)], kT(kv, g), preferred_element_type=F32),
              lambda p, g=g: lax.dot_general(p, vT(kv, g), NT_DIMS, preferred_element_type=F32), member[rs(g)])

    @pl.when(j == n_steps - 1)
    def _():
        new_blk = PAST_LEN // SEL_BLOCK
        kpos = PAST_LEN + lax.broadcasted_iota(jnp.int32, (1, T), 1)
        ns_rows = newsel_ref[...].astype(BF16)
        base = new_blk - new_blk % 8
        pick = (lax.broadcasted_iota(jnp.int32, (8, T), 0) == new_blk % 8).astype(BF16)
        new_member = lax.dot_general(sel_scr[base:base + 8, :].astype(BF16), pick, tn_dims,
                                     preferred_element_type=F32) > 0.5
        for g in range(G):
            flash(g, lax.dot_general(qb[rs(g)], ns_rows[:, g * D:(g + 1) * D], NT_DIMS, preferred_element_type=F32),
                  lambda p, g=g: jnp.dot(p, ns_rows[:, G * D + g * D:G * D + (g + 1) * D], preferred_element_type=F32),
                  new_member[rs(g)] & (kpos <= row_t))
        o_sel = acc_scr[...] / jnp.maximum(l_scr[...], F32_TINY)
        gs = jax.nn.sigmoid(gate_ref[...])
        o = gs[:, 0:1] * ocmp_scr[...] + gs[:, 1:2] * o_sel + gs[:, 2:3] * owin_scr[...]
        for h in range(H):
            o_ref[:, h * D:(h + 1) * D] = o[h * T:(h + 1) * T]


def _nsa_sample_attn(qh, kc, vc, pool_sel, layer, page_table, new_sel, win_cache, new_win, gate_rows):
    B, H, T, D = qh.shape
    n_pages = page_table.shape[1]
    nc = kc.shape[3]
    ns = -(-(PAST_LEN + T) // SEL_BLOCK)
    ns_pad = -(-ns // LANE) * LANE
    c2s = _cmp_to_sel_padded(nc, ns_pad)
    rows = H * T
    cst = lambda shape: pl.BlockSpec(shape, lambda b, j, pt: (0,) * len(shape))
    per_b = lambda shape: pl.BlockSpec((None,) + shape, lambda b, j, pt: (b,) + (0,) * len(shape))
    page_spec = lambda n: pl.BlockSpec((None, None, 2 * NSA_KV_DIM, PAGE_SIZE),
                                       lambda b, j, pt: (layer, pt[b, j * SEL_PAGES + n], 0, 0))
    grid_spec = pltpu.PrefetchScalarGridSpec(
        num_scalar_prefetch=1, grid=(B, n_pages // SEL_PAGES),
        in_specs=[page_spec(n) for n in range(SEL_PAGES)] + [
            per_b((H, T, D)), per_b((NSA_KV_HEADS, 3, nc, D)), per_b((NSA_KV_HEADS, nc, D)), cst(c2s.shape),
            per_b((T, 2 * NSA_KV_DIM)), per_b((2 * NSA_KV_DIM, WINDOW)), per_b((T, 2 * NSA_KV_DIM)), per_b((rows, 3))],
        out_specs=per_b((T, NSA_DIM)),
        scratch_shapes=[pltpu.VMEM((ns_pad, rows), F32), pltpu.VMEM((rows, 1), F32), pltpu.VMEM((rows, 1), F32),
                        pltpu.VMEM((rows, D), F32), pltpu.VMEM((rows, D), F32), pltpu.VMEM((rows, D), F32)])
    return pl.pallas_call(
        functools.partial(_nsa_sample_kernel, T=T, ns_pad=ns_pad), grid_spec=grid_spec,
        out_shape=jax.ShapeDtypeStruct((B, T, NSA_DIM), F32),
        compiler_params=_cparams("arbitrary", "arbitrary"),
        name="nsa_sample_attn",
    )(page_table, *([pool_sel] * SEL_PAGES), qh, kc, vc, c2s, new_sel, win_cache, new_win, gate_rows)


def _nsa_sample_pallas(q2, cmp2, sel2, win2, gates2, p, wb, B, T, layer, cache_cmp_kv, cache_sel_kv, win_buf, page_table):
    qh, _, _, _, _, sel_rows, win_rows = _nsa_prep(q2, sel2, win2, p['nsa_q_norm'], p['nsa_k_norm'], B, T)
    depth, n_pool = cache_cmp_kv.shape[:2]
    pool_cmp = cache_cmp_kv.transpose(0, 1, 3, 4, 5, 2).reshape(depth, n_pool, 2 * NSA_KV_DIM, PAGE_SIZE)
    pool_cmp = jnp.swapaxes(pool_cmp, 2, 3).reshape(depth, n_pool, PAGE_HALVES, HALF_COLS)
    tb_past = _half_proj_paged(pool_cmp, layer, page_table, wb['cmp_pair'])
    new_half = jnp.pad(cmp2.reshape(B, 1, T * 2 * NSA_KV_DIM), ((0, 0), (0, 7), (0, HALF_COLS - T * 2 * NSA_KV_DIM)))
    tb_new = _half_proj_rows(new_half.reshape(B * 8, HALF_COLS), wb['cmp_pair']).reshape(B, 8, -1)
    kc, vc = _cmp_finish(tb_past, tb_new, wb['cmp_w1'], p['nsa_cmp_pos'], p['nsa_cmp_b1'], wb['cmp_w2'], p['nsa_k_norm'][0])
    pool_sel = cache_sel_kv.transpose(0, 1, 3, 4, 5, 2).reshape(depth, n_pool, 2 * NSA_KV_DIM, PAGE_SIZE)
    win_cache = win_buf.transpose(0, 2, 3, 4, 1).reshape(B, 2 * NSA_KV_DIM, WINDOW)
    g = gates2[:, :NSA_KV_HEADS * 16].reshape(B, T, NSA_KV_HEADS, 4, NSA_RPG)[:, :, :, :3]
    gate_rows = g.transpose(0, 2, 4, 1, 3).reshape(B, NSA_HEADS * T, 3)
    sel3, win3 = sel_rows.reshape(B, T, -1), win_rows.reshape(B, T, -1)
    o = _nsa_sample_attn(qh, kc, vc, pool_sel, layer, page_table, sel3, win_cache, win3, gate_rows)
    shp = (B, T, 2, NSA_KV_HEADS, NSA_HD)
    win_out = jnp.concatenate([win_buf, win_rows.reshape(shp)], axis=1)[:, -WINDOW:]
    return o, (cmp2.reshape(shp), sel_rows.reshape(shp), win_out)


RW_CHUNK = 64


def _heads(x):
    return jnp.stack([x[:, h * RW_HD:(h + 1) * RW_HD] for h in range(x.shape[1] // RW_HD)], axis=0)


def _unheads(x):
    return jnp.concatenate([x[h] for h in range(x.shape[0])], axis=1)


def _bmm(a, b):
    return jnp.einsum('hlm,hmn->hln', a.astype(BF16), b.astype(BF16), preferred_element_type=F32)


def _bmm_nt(a, b):
    return jnp.einsum('hlk,hmk->hlm', a.astype(BF16), b.astype(BF16), preferred_element_type=F32)


def _bmm_tn(a, b):
    return jnp.einsum('hlv,hlk->hvk', a.astype(BF16), b.astype(BF16), preferred_element_type=F32)


def _rwkv_kernel(x_ref, shift_ref, s0_ref, mu_ref, w0_ref, a0_ref, w2_ref, a2_ref, g2_ref, kk_ref, ka_ref, rk_ref,
                 lng_ref, lnb_ref, o_ref, sfin_ref, prev_scr, s_scr):
    L = x_ref.shape[0]
    c = pl.program_id(1)

    @pl.when(c == 0)
    def _():
        prev_scr[...] = shift_ref[...]
        s_scr[...] = s0_ref[...]

    x = x_ref[...]
    prev = jnp.concatenate([prev_scr[...], x[:L - 1]], axis=0)
    prev_scr[...] = x[L - 1:]
    xs = x + mu_ref[...] * (prev - x)
    r, k, v = xs[:, :RW_DIM], xs[:, RW_DIM:2 * RW_DIM], xs[:, 2 * RW_DIM:3 * RW_DIM]
    wl = xs[:, RW_SPLITS[2]:RW_SPLITS[3]]
    al = xs[:, RW_SPLITS[3]:RW_SPLITS[4]]
    gl = xs[:, RW_SPLITS[4]:]
    w = -jax.nn.softplus(-(w0_ref[...] + jnp.dot(jnp.tanh(wl).astype(BF16), w2_ref[...], preferred_element_type=F32))) - 0.5
    logd = -jnp.exp(w)
    a = jax.nn.sigmoid(a0_ref[...] + jnp.dot(al.astype(BF16), a2_ref[...], preferred_element_type=F32))
    g = jnp.dot(jax.nn.sigmoid(gl).astype(BF16), g2_ref[...], preferred_element_type=F32)
    kk = k * kk_ref[...]
    k = k * (1.0 + (a - 1.0) * ka_ref[...])
    row = lax.broadcasted_iota(jnp.int32, (L, L), 0)
    col = lax.broadcasted_iota(jnp.int32, (L, L), 1)
    cs = jnp.dot((col <= row).astype(F32), logd, precision=HIGHEST, preferred_element_type=F32)
    g_incl, g_prev, g_inv = jnp.exp(cs), jnp.exp(cs - logd), jnp.exp(-cs)

    kk_h = _heads(kk)
    kk_h = kk_h * lax.rsqrt(jnp.sum(kk_h * kk_h, axis=-1, keepdims=True) + 1e-12)
    r_h, k_h, v_h = _heads(r), _heads(k), _heads(v)
    inv_h = _heads(g_inv)
    at = kk_h * _heads(g_prev)
    rt = r_h * _heads(g_incl)
    bt = -(kk_h * _heads(a)) * inv_h
    kt = k_h * inv_h
    ar = jnp.concatenate([at, rt], axis=1)
    bk = jnp.concatenate([bt, kt], axis=1)
    gram = _bmm_nt(ar, bk)
    strict, incl = (col < row)[None], (col <= row)[None]
    a_ab = jnp.where(strict, gram[:, :L, :L], 0.0)
    a_ak = jnp.where(strict, gram[:, :L, L:], 0.0)
    a_rbk = jnp.concatenate([jnp.where(incl, gram[:, L:, :L], 0.0), jnp.where(incl, gram[:, L:, L:], 0.0)], axis=2)
    tinv = jnp.where((col == row)[None], 1.0, a_ab)
    pw = a_ab
    n_sq, span = 0, 2
    while span < L:
        n_sq, span = n_sq + 1, span * 2
    for _ in range(n_sq):
        pw = _bmm(pw, pw)
        tinv = tinv + _bmm(tinv, pw)

    s0 = s_scr[...]
    ars = _bmm_nt(ar, s0)
    u = _bmm(tinv, ars[:, :L] + _bmm(a_ak, v_h))
    uv = jnp.concatenate([u, v_h], axis=1)
    o = ars[:, L:] + _bmm(a_rbk, uv)
    s_new = (s0 + _bmm_tn(uv, bk)) * _heads(g_incl[L - 1:])
    s_scr[...] = s_new

    @pl.when(c == pl.num_programs(1) - 1)
    def _():
        sfin_ref[...] = s_new

    mean = jnp.mean(o, axis=-1, keepdims=True)
    var = jnp.mean(jnp.square(o - mean), axis=-1, keepdims=True)
    on = _unheads((o - mean) * lax.rsqrt(var + RW_GN_EPS)) * lng_ref[...] + lnb_ref[...]
    bonus = _unheads(jnp.sum(r_h * k_h * _heads(rk_ref[...]), axis=-1, keepdims=True) * v_h)
    o_ref[...] = (on + bonus) * g


def _rwkv7_pallas(p_rw, shift_state, wkv_state, p, wb):
    B, T, _ = p_rw.shape
    L = min(RW_CHUNK, T)
    assert T % L == 0
    vec = lambda name: p[name].reshape(1, -1)
    full = lambda shape: pl.BlockSpec(shape, lambda b, c: (0,) * len(shape))
    st = pl.BlockSpec((None, RW_HEADS, RW_HD, RW_HD), lambda b, c: (b, 0, 0, 0))
    o, s_fin = pl.pallas_call(
        _rwkv_kernel,
        grid=(B, T // L),
        in_specs=[pl.BlockSpec((None, L, RW_IN), lambda b, c: (b, c, 0)),
                  pl.BlockSpec((None, 1, RW_IN), lambda b, c: (b, 0, 0)), st,
                  full((1, RW_IN)), full((1, RW_DIM)), full((1, RW_DIM)),
                  full((RW_W_LORA, RW_DIM)), full((RW_A_LORA, RW_DIM)), full((RW_G_LORA, RW_DIM)),
                  full((1, RW_DIM)), full((1, RW_DIM)), full((1, RW_DIM)), full((1, RW_DIM)), full((1, RW_DIM))],
        out_specs=[pl.BlockSpec((None, L, RW_DIM), lambda b, c: (b, c, 0)), st],
        out_shape=[jax.ShapeDtypeStruct((B, T, RW_DIM), F32),
                   jax.ShapeDtypeStruct((B, RW_HEADS, RW_HD, RW_HD), F32)],
        scratch_shapes=[pltpu.VMEM((1, RW_IN), F32), pltpu.VMEM((RW_HEADS, RW_HD, RW_HD), F32)],
        compiler_params=_cparams("arbitrary", "arbitrary"),
        name="rwkv7_mix",
    )(p_rw, shift_state.reshape(B, 1, RW_IN), wkv_state, vec('rw_mu'), vec('rw_w0'), vec('rw_a0'),
      wb['rw_w2'], wb['rw_a2'], wb['rw_g2'], vec('rw_k_k'), vec('rw_k_a'), vec('rw_r_k'), vec('rw_lnx_g'), vec('rw_lnx_b'))
    return o, p_rw[:, -1], s_fin


SSM_GW = SSM_RPG * SSM_HD


def _mamba_kernel(z_ref, xbc_ref, dt_ref, conv0_ref, s0_ref, cw_ref, cb_ref, dtb_ref, alog_ref, dvec_ref, ng_ref,
                  hexp_ref, o_ref, convf_ref, sfin_ref, tail_scr, s_scr):
    L = z_ref.shape[0]
    c = pl.program_id(1)
    last = pl.num_programs(1) - 1
    G, N, K = SSM_GROUPS, SSM_STATE, SSM_CONV

    @pl.when(c == 0)
    def _():
        tail_scr[...] = conv0_ref[...]
        s_scr[...] = s0_ref[...]

    ext = jnp.concatenate([tail_scr[...], xbc_ref[...]], axis=0)
    new_tail = ext[L:L + K - 1]
    tail_scr[...] = new_tail
    conv = cb_ref[...] + sum(ext[i:i + L] * cw_ref[i:i + 1] for i in range(K))
    act = jax.nn.silu(conv)
    xs, bm, cm = act[:, :SSM_DIM], act[:, SSM_DIM:SSM_DIM + G * N], act[:, SSM_DIM + G * N:]

    dt = jax.nn.softplus(dt_ref[...] + dtb_ref[...])
    da = dt * (-jnp.exp(alog_ref[...]))
    row = lax.broadcasted_iota(jnp.int32, (L, L), 0)
    col = lax.broadcasted_iota(jnp.int32, (L, L), 1)
    causal = col <= row
    a_cs = jnp.dot(causal.astype(F32), da, precision=HIGHEST, preferred_element_type=F32)
    a_last = a_cs[L - 1:]
    a_cs_t = a_cs.T
    per_head = jnp.concatenate([dt, dt * jnp.exp(a_last - a_cs), jnp.exp(a_cs)], axis=0)
    wide = jnp.dot(per_head, hexp_ref[...], precision=HIGHEST, preferred_element_type=F32)
    xdt_b = (xs * wide[:L]).astype(BF16)
    xds = (xs * wide[L:2 * L]).astype(BF16)
    dec_in = wide[2 * L:]
    st_head = lax.broadcasted_iota(jnp.int32, (SSM_GW, LANE), 0) // SSM_HD
    st_lane = lax.broadcasted_iota(jnp.int32, (SSM_GW, LANE), 1)
    dec_chunk = jnp.exp(a_last)

    ys = []
    for g in range(G):
        bg = bm[:, g * N:(g + 1) * N].astype(BF16)
        cg = cm[:, g * N:(g + 1) * N].astype(BF16)
        cb = lax.dot_general(cg, bg, NT_DIMS, preferred_element_type=F32)
        s_in = s_scr[g]
        y_off = lax.dot_general(cg, s_in.astype(BF16), NT_DIMS, preferred_element_type=F32)
        cols = slice(g * SSM_GW, (g + 1) * SSM_GW)
        y_diag = []
        for r in range(SSM_RPG):
            h = g * SSM_RPG + r
            seg = a_cs[:, h:h + 1] - a_cs_t[h:h + 1, :]
            w = (cb * jnp.exp(jnp.where(causal, seg, -jnp.inf))).astype(BF16)
            y_diag.append(jnp.dot(w, xdt_b[:, h * SSM_HD:(h + 1) * SSM_HD], preferred_element_type=F32))
        ys.append(jnp.concatenate(y_diag, axis=1) + y_off * dec_in[:, cols])
        states = lax.dot_general(xds[:, cols], bg, (((0,), (0,)), ((), ())), preferred_element_type=F32)
        dec_col = jnp.sum(jnp.where(st_head + g * SSM_RPG == st_lane, dec_chunk, 0.0), axis=-1, keepdims=True)
        s_scr[g] = s_in * dec_col + states
    y = jnp.concatenate(ys, axis=1) + dvec_ref[...] * xs
    y = y * jax.nn.silu(z_ref[...])
    o_ref[...] = y * lax.rsqrt(jnp.mean(y * y, axis=-1, keepdims=True) + EPS) * ng_ref[...]

    @pl.when(c == last)
    def _():
        convf_ref[...] = new_tail
        sfin_ref[...] = s_scr[...]


def _mamba2_pallas(z, xbc, dt, conv_state, ssm_state, p):
    B, T, _ = z.shape
    L = SSM_CHUNK if T % SSM_CHUNK == 0 else T
    pad = lambda v: jnp.pad(v, (0, LANE - SSM_HEADS)).reshape(1, LANE)
    hexp = (jnp.arange(LANE)[:, None] == jnp.arange(SSM_DIM)[None, :] // SSM_HD).astype(F32)
    full = lambda shape: pl.BlockSpec(shape, lambda b, c: (0,) * len(shape))
    tile = lambda wd: pl.BlockSpec((None, L, wd), lambda b, c: (b, c, 0))
    per_b = lambda shape: pl.BlockSpec((None,) + shape, lambda b, c: (b,) + (0,) * len(shape))
    st_shape = (SSM_GROUPS, SSM_GW, SSM_STATE)
    y, conv_f, s_fin = pl.pallas_call(
        _mamba_kernel,
        grid=(B, T // L),
        in_specs=[tile(SSM_DIM), tile(SSM_XBC), tile(LANE), per_b((SSM_CONV - 1, SSM_XBC)), per_b(st_shape),
                  full((SSM_CONV, SSM_XBC)), full((1, SSM_XBC)), full((1, LANE)), full((1, LANE)), full((1, SSM_DIM)),
                  full((1, SSM_DIM)), full((LANE, SSM_DIM))],
        out_specs=[tile(SSM_DIM), per_b((SSM_CONV - 1, SSM_XBC)), per_b(st_shape)],
        out_shape=[jax.ShapeDtypeStruct((B, T, SSM_DIM), F32),
                   jax.ShapeDtypeStruct((B, SSM_CONV - 1, SSM_XBC), F32),
                   jax.ShapeDtypeStruct((B,) + st_shape, F32)],
        scratch_shapes=[pltpu.VMEM((SSM_CONV - 1, SSM_XBC), F32), pltpu.VMEM(st_shape, F32)],
        compiler_params=_cparams("arbitrary", "arbitrary"),
        name="mamba2_mix",
    )(z, xbc, dt, conv_state, ssm_state.reshape((B,) + st_shape), p['ssm_conv_w'], p['ssm_conv_b'].reshape(1, -1),
      pad(p['ssm_dt_bias']), pad(p['ssm_a_log']), jnp.repeat(p['ssm_d'], SSM_HD).reshape(1, -1),
      p['ssm_norm_g'].reshape(1, -1), hexp)
    return y, conv_f, s_fin.reshape(B, SSM_HEADS, SSM_HD, SSM_STATE)


def _rmsnorm(x, g):
    y = x * lax.rsqrt(jnp.mean(x * x, axis=-1, keepdims=True) + EPS)
    return y * g


def _masked_softmax(s, mask):
    s = jnp.where(mask, s, -jnp.inf)
    m = jnp.max(s, axis=-1, keepdims=True)
    m = jnp.where(jnp.isfinite(m), m, 0.0)
    e = jnp.exp(s - m)
    return e / jnp.maximum(jnp.sum(e, axis=-1, keepdims=True), jnp.finfo(F32).tiny)


def _rwkv7_mix(proj, shift_state, wkv_state, p):
    B, T, _ = proj.shape
    prev = jnp.concatenate([shift_state[:, None], proj[:, :-1]], axis=1)
    xs = proj + p['rw_mu'] * (prev - proj)
    r, k, v, wl, al, gl = jnp.split(xs, RW_SPLITS, axis=-1)
    w = -jax.nn.softplus(-(p['rw_w0'] + jnp.tanh(wl) @ p['rw_w2'])) - 0.5
    decay = jnp.exp(-jnp.exp(w))
    a = jax.nn.sigmoid(p['rw_a0'] + al @ p['rw_a2'])
    g = jax.nn.sigmoid(gl) @ p['rw_g2']
    kk = k * p['rw_k_k']
    k = k * (1.0 + (a - 1.0) * p['rw_k_a'])
    hd = lambda t: t.reshape(B, T, RW_HEADS, RW_HD)
    r, k, v, kk, a, decay = hd(r), hd(k), hd(v), hd(kk), hd(a), hd(decay)
    kk = kk * lax.rsqrt(jnp.sum(kk * kk, axis=-1, keepdims=True) + 1e-12)

    def step(S, inp):
        r_t, w_t, k_t, v_t, kk_t, a_t = inp
        sa = jnp.einsum('bhvk,bhk->bhv', S, kk_t)
        S = S * w_t[:, :, None, :] - sa[..., None] * (kk_t * a_t)[:, :, None, :] + v_t[..., None] * k_t[:, :, None, :]
        return S, jnp.einsum('bhvk,bhk->bhv', S, r_t)

    seqs = tuple(jnp.swapaxes(t, 0, 1) for t in (r, decay, k, v, kk, a))
    S_fin, o = lax.scan(step, wkv_state, seqs)
    o = jnp.swapaxes(o, 0, 1)
    mean = jnp.mean(o, axis=-1, keepdims=True)
    var = jnp.mean(jnp.square(o - mean), axis=-1, keepdims=True)
    o = ((o - mean) * lax.rsqrt(var + RW_GN_EPS)).reshape(B, T, RW_DIM) * p['rw_lnx_g'] + p['rw_lnx_b']
    bonus = jnp.sum(r * k * p['rw_r_k'], axis=-1, keepdims=True) * v
    o = (o + bonus.reshape(B, T, RW_DIM)) * g
    return o, proj[:, -1], S_fin


def _ssd_chunked(x, dt, A, Bm, Cm, s0):
    Bsz, T = x.shape[:2]
    L = SSM_CHUNK if T % SSM_CHUNK == 0 else T
    nc = T // L
    ch = lambda t: t.reshape((Bsz, nc, L) + t.shape[2:])
    xdt = ch(x * dt[..., None])
    a_cs = jnp.cumsum(ch(dt * A), axis=2)
    Bc, Cc = ch(Bm), ch(Cm)
    seg = a_cs[:, :, :, None] - a_cs[:, :, None, :]
    causal = jnp.tril(jnp.ones((L, L), dtype=bool))[:, :, None, None]
    decay_ls = jnp.exp(jnp.where(causal, seg, -jnp.inf))
    cb = jnp.einsum('bclgn,bcsgn->bclsg', Cc, Bc)
    y_diag = jnp.einsum('bclsgr,bcsgrp->bclgrp', cb[..., None] * decay_ls, xdt)
    decay_s = jnp.exp(a_cs[:, :, -1:] - a_cs)
    states = jnp.einsum('bclgn,bclgr,bclgrp->bcgrpn', Bc, decay_s, xdt)
    chunk_decay = jnp.exp(a_cs[:, :, -1])

    def step(s, inp):
        st, dec = inp
        return s * dec[..., None, None] + st, s

    s_fin, s_in = lax.scan(step, s0, (jnp.swapaxes(states, 0, 1), jnp.swapaxes(chunk_decay, 0, 1)))
    s_in = jnp.swapaxes(s_in, 0, 1)
    y_off = jnp.einsum('bclgn,bcgrpn,bclgr->bclgrp', Cc, s_in, jnp.exp(a_cs))
    return (y_diag + y_off).reshape(x.shape), s_fin


def _mamba2_mix(z, xbc, dt, conv_state, ssm_state, p):
    B, T, _ = z.shape
    xpad = jnp.concatenate([conv_state, xbc], axis=1)
    conv = p['ssm_conv_b'] + sum(xpad[:, i:i + T] * p['ssm_conv_w'][i] for i in range(SSM_CONV))
    xbc_c = jax.nn.silu(conv)
    xs, Bm, Cm = jnp.split(xbc_c, (SSM_DIM, SSM_DIM + SSM_GROUPS * SSM_STATE), axis=-1)
    x = xs.reshape(B, T, SSM_GROUPS, SSM_RPG, SSM_HD)
    Bm = Bm.reshape(B, T, SSM_GROUPS, SSM_STATE)
    Cm = Cm.reshape(B, T, SSM_GROUPS, SSM_STATE)
    dt = jax.nn.softplus(dt + p['ssm_dt_bias']).reshape(B, T, SSM_GROUPS, SSM_RPG)
    A = -jnp.exp(p['ssm_a_log']).reshape(SSM_GROUPS, SSM_RPG)
    s0 = ssm_state.reshape(B, SSM_GROUPS, SSM_RPG, SSM_HD, SSM_STATE)
    y, s_fin = _ssd_chunked(x, dt, A, Bm, Cm, s0)
    y = y + p['ssm_d'].reshape(SSM_GROUPS, SSM_RPG)[:, :, None] * x
    y = y.reshape(B, T, SSM_DIM) * jax.nn.silu(z)
    y = _rmsnorm(y, p['ssm_norm_g'])
    return y, xpad[:, -(SSM_CONV - 1):], s_fin.reshape(B, SSM_HEADS, SSM_HD, SSM_STATE)


def _nsa_project(q, kv, gates, p):
    B, T, _ = q.shape
    kc, vc, ks, vs, kw, vw = jnp.split(kv, 6, axis=-1)
    kvh = lambda t: t.reshape(B, T, NSA_KV_HEADS, NSA_HD)
    q = _rmsnorm(q.reshape(B, T, NSA_HEADS, NSA_HD), p['nsa_q_norm']).reshape(B, T, NSA_KV_HEADS, NSA_RPG, NSA_HD)
    cmp_rows = jnp.stack([kvh(kc), kvh(vc)], axis=2)
    sel_rows = jnp.stack([_rmsnorm(kvh(ks), p['nsa_k_norm'][1]), kvh(vs)], axis=2)
    win_rows = jnp.stack([_rmsnorm(kvh(kw), p['nsa_k_norm'][2]), kvh(vw)], axis=2)
    g = jax.nn.sigmoid(gates).reshape(B, T, 3, NSA_KV_HEADS, NSA_RPG, 1)
    return q, cmp_rows, sel_rows, win_rows, g


def _half_proj(rows, w1):
    B, T = rows.shape[:2]
    h = rows.reshape(B, T // CMP_STRIDE, CMP_STRIDE, 2, NSA_KV_HEADS, NSA_HD)
    top = jnp.einsum('bnsckd,csdh->bnckh', h, w1[:, :CMP_STRIDE])
    bot = jnp.einsum('bnsckd,csdh->bnckh', h, w1[:, CMP_STRIDE:])
    return top, bot


def _compress_finish(top, bot, p):
    w1 = p['nsa_cmp_w1']
    const = jnp.einsum('csd,csdh->ch', p['nsa_cmp_pos'], w1) + p['nsa_cmp_b1']
    hid = jax.nn.gelu(top[:, :-1] + bot[:, 1:] + const[:, None, :])
    kv = jnp.einsum('bnckh,chd->bnckd', hid, p['nsa_cmp_w2'])
    return _rmsnorm(kv[:, :, 0], p['nsa_k_norm'][0]), kv[:, :, 1]


def _cmp_to_sel(nc, ns):
    i = jnp.arange(nc)[:, None] * CMP_STRIDE
    j = jnp.arange(ns)[None, :] * SEL_BLOCK
    ov = jnp.minimum(i + CMP_BLOCK, j + SEL_BLOCK) - jnp.maximum(i, j)
    return jnp.clip(ov, 0, None).astype(F32) / CMP_BLOCK


def _fetch_rows(kv, pos):
    b = jnp.arange(kv.shape[0])[:, None, None, None]
    g = jnp.arange(NSA_KV_HEADS)[None, :, None, None]
    return kv[b, jnp.clip(pos, 0, kv.shape[1] - 1), :, g, :]


def _fetch_paged(pool, l, page_table, new_rows, pos):
    b = jnp.arange(page_table.shape[0])[:, None, None, None]
    g = jnp.arange(NSA_KV_HEADS)[None, :, None, None]
    pp = jnp.clip(pos, 0, PAST_LEN - 1)
    phys = page_table[b, pp // PAGE_SIZE]
    past = pool[l, phys, pp % PAGE_SIZE, :, g, :]
    new = _fetch_rows(new_rows, pos - PAST_LEN)
    return jnp.where((pos < PAST_LEN)[..., None, None], past, new)


def _nsa_cmp_sel(q, t, kc, vc, fetch, ns):
    B, Tq = q.shape[:2]
    scale = NSA_HD ** -0.5
    nc = kc.shape[1]
    s = jnp.einsum('bqgrd,bngd->bgrqn', q, kc) * scale
    cmp_end = jnp.arange(nc) * CMP_STRIDE + CMP_BLOCK - 1
    p_cmp = _masked_softmax(s, cmp_end[None, :] <= t[:, None])
    o_cmp = jnp.einsum('bgrqn,bngd->bqgrd', p_cmp, vc)
    imp = jnp.einsum('bgrqn,ns->bgqs', p_cmp, _cmp_to_sel(nc, ns))
    blk = jnp.arange(ns)[None, :]
    cur = t[:, None] // SEL_BLOCK
    forced = (blk == 0) | (blk == cur) | (blk == cur - 1)
    valid = blk * SEL_BLOCK <= t[:, None]
    score = jnp.where(forced, SEL_FORCE, jnp.where(valid, imp, -SEL_FORCE))
    n_top = min(N_SELECT, ns)
    _, idx = lax.top_k(score, n_top)
    pos = (idx[..., None] * SEL_BLOCK + jnp.arange(SEL_BLOCK)).reshape(B, NSA_KV_HEADS, Tq, n_top * SEL_BLOCK)
    kvs = fetch(pos)
    s2 = jnp.einsum('bqgrd,bgqkd->bgrqk', q, kvs[..., 0, :]) * scale
    p_sel = _masked_softmax(s2, (pos <= t[:, None])[:, :, None])
    o_sel = jnp.einsum('bgrqk,bgqkd->bqgrd', p_sel, kvs[..., 1, :])
    return o_cmp, o_sel


def _nsa_window(q, t, kw, vw, kpos):
    s = jnp.einsum('bqgrd,bngd->bgrqn', q, kw) * (NSA_HD ** -0.5)
    mask = (kpos[None, :] <= t[:, None]) & (kpos[None, :] > t[:, None] - WINDOW) & (kpos[None, :] >= 0)
    p = _masked_softmax(s, mask)
    return jnp.einsum('bgrqn,bngd->bqgrd', p, vw)


def _nsa_merge(o, g):
    B, T = o.shape[:2]
    return jnp.sum(g * o, axis=2).reshape(B, T, NSA_DIM)


def _nsa_prompt(q, kv, gates, p):
    q, cmp_rows, sel_rows, win_rows, g = _nsa_project(q, kv, gates, p)
    B, T = q.shape[:2]
    top, bot = _half_proj(cmp_rows, p['nsa_cmp_w1'])
    kc, vc = _compress_finish(top, bot, p)
    ns = -(-T // SEL_BLOCK)
    fetch = lambda pos: _fetch_rows(sel_rows, pos)
    win_pad = jnp.pad(win_rows, ((0, 0), (WINDOW, 0), (0, 0), (0, 0), (0, 0)))
    nqb = T // Q_BLOCK
    qb = jnp.swapaxes(q.reshape(B, nqb, Q_BLOCK, NSA_KV_HEADS, NSA_RPG, NSA_HD), 0, 1)

    def body(args):
        q_blk, i = args
        t = i * Q_BLOCK + jnp.arange(Q_BLOCK)
        o_cmp, o_sel = _nsa_cmp_sel(q_blk, t, kc, vc, fetch, ns)
        kv_w = lax.dynamic_slice_in_dim(win_pad, i * Q_BLOCK, WINDOW + Q_BLOCK, axis=1)
        kpos = i * Q_BLOCK - WINDOW + jnp.arange(WINDOW + Q_BLOCK)
        o_win = _nsa_window(q_blk, t, kv_w[:, :, 0], kv_w[:, :, 1], kpos)
        return jnp.stack([o_cmp, o_sel, o_win], axis=2)

    o = lax.map(body, (qb, jnp.arange(nqb)))
    o = jnp.swapaxes(o, 0, 1).reshape(B, T, 3, NSA_KV_HEADS, NSA_RPG, NSA_HD)
    return _nsa_merge(o, g), (cmp_rows, sel_rows, win_pad[:, -WINDOW:])


def _nsa_sample(q, kv, gates, p, l, cache_cmp_kv, cache_sel_kv, win_buf, page_table):
    q, cmp_rows, sel_rows, win_rows, g = _nsa_project(q, kv, gates, p)
    B, T = q.shape[:2]
    past_cmp = cache_cmp_kv[l, page_table].reshape(B, PAST_LEN, 2, NSA_KV_HEADS, NSA_HD)
    new_cmp = jnp.pad(cmp_rows, ((0, 0), (0, (-T) % CMP_STRIDE), (0, 0), (0, 0), (0, 0)))
    tp, bp = _half_proj(past_cmp, p['nsa_cmp_w1'])
    tn, bn = _half_proj(new_cmp, p['nsa_cmp_w1'])
    kc, vc = _compress_finish(jnp.concatenate([tp, tn], axis=1), jnp.concatenate([bp, bn], axis=1), p)
    t = PAST_LEN + jnp.arange(T)
    ns = -(-(PAST_LEN + T) // SEL_BLOCK)
    fetch = lambda pos: _fetch_paged(cache_sel_kv, l, page_table, sel_rows, pos)
    o_cmp, o_sel = _nsa_cmp_sel(q, t, kc, vc, fetch, ns)
    kv_w = jnp.concatenate([win_buf, win_rows], axis=1)
    kpos = PAST_LEN - WINDOW + jnp.arange(WINDOW + T)
    o_win = _nsa_window(q, t, kv_w[:, :, 0], kv_w[:, :, 1], kpos)
    o = jnp.stack([o_cmp, o_sel, o_win], axis=2)
    return _nsa_merge(o, g), (cmp_rows, sel_rows, kv_w[:, -WINDOW:])


def _prep_weights(p):
    w_in = p['w_in']
    o_ssm = RW_IN
    o_nsa = RW_IN + SSM_IN
    o_gate = o_nsa + NSA_IN
    pad = lambda w, n: jnp.pad(w, ((0, 0), (0, n - w.shape[1])))
    w_rw = w_in[:, :RW_IN]
    w_ssm = jnp.concatenate([w_in[:, o_ssm:o_ssm + SSM_DIM + SSM_XBC],
                             pad(w_in[:, o_ssm + SSM_DIM + SSM_XBC:o_nsa], LANE)], axis=1)
    o_g = o_nsa + NSA_DIM + 6 * NSA_KV_DIM
    wg = w_in[:, o_g:o_gate].reshape(D_MODEL, 3, NSA_KV_HEADS, NSA_RPG).transpose(0, 2, 1, 3)
    wg = jnp.pad(wg.reshape(D_MODEL, NSA_KV_HEADS, 3 * NSA_RPG), ((0, 0), (0, 0), (0, 16 - 3 * NSA_RPG)))
    w_nsa = jnp.concatenate([w_in[:, o_nsa:o_g], pad(wg.reshape(D_MODEL, NSA_KV_HEADS * 16), LANE),
                             w_in[:, o_gate:]], axis=1)
    bf = lambda w: w.astype(BF16)
    return dict(w_ada=bf(p['w_ada']), w_rw=bf(w_rw), w_ssm=bf(w_ssm), w_nsa=bf(w_nsa),
                w_br_rw=bf(p['w_br_rw']), w_br_ssm=bf(p['w_br_ssm']), w_br_nsa=bf(p['w_br_nsa']),
                w_out=bf(p['w_out']), w_ffn_in=bf(p['w_ffn_in']), w_ffn_out=bf(p['w_ffn_out']),
                rw_w2=bf(p['rw_w2']), rw_a2=bf(p['rw_a2']), rw_g2=bf(p['rw_g2']),
                cmp_w1=bf(p['nsa_cmp_w1'].reshape(2, CMP_BLOCK * NSA_HD, CMP_HIDDEN)), cmp_w2=bf(p['nsa_cmp_w2']),
                cmp_pair=_pair_weights(p['nsa_cmp_w1']))


def _trunk_layer(x, c, p, wb, rw_shift, rw_wkv, conv_state, ssm_state, nsa_fn, per_batch):
    B, T, _ = x.shape
    n = B * T
    mod = _ada(c, wb['w_ada'], p['b_ada'])
    if per_batch:
        mods = [m.reshape(B, 1, D_MODEL) for m in jnp.split(mod, 6, axis=-1)]
    else:
        mods = [jnp.repeat(m, T, axis=0) for m in jnp.split(mod, 6, axis=-1)]
    sh1, sc1, gt1, sh2, sc2, gt2 = mods
    x2 = x.reshape(n, D_MODEL)
    (p_rw,) = _norm_proj(x2, sc1, sh1, p['ln1'], wb['w_rw'], (RW_IN,), per_batch, T, "proj_rw")
    z, xbc, dt = _norm_proj(x2, sc1, sh1, p['ln1'], wb['w_ssm'], (SSM_DIM, SSM_XBC, LANE), per_batch, T, "proj_ssm")
    kvw = 2 * NSA_KV_DIM
    q, cmp2, sel2, win2, gates, p_gate = _norm_proj(x2, sc1, sh1, p['ln1'], wb['w_nsa'],
                                                    (NSA_DIM, kvw, kvw, kvw, LANE, N_BRANCH * D_MODEL), per_batch, T,
                                                    "proj_nsa")
    r3 = lambda t: t.reshape(B, T, t.shape[-1])
    o_rw, rw_shift, rw_wkv = _rwkv7_pallas(r3(p_rw), rw_shift, rw_wkv, p, wb)
    o_ssm, conv_state, ssm_state = _mamba2_pallas(r3(z), r3(xbc), r3(dt), conv_state, ssm_state, p)
    o_nsa, (cmp_rows, sel_rows, win_buf) = nsa_fn(q, cmp2, sel2, win2, gates, p, wb, B, T)
    x2 = _merge(x2, gt1, o_rw.reshape(n, -1), o_ssm.reshape(n, -1), o_nsa.reshape(n, -1), p_gate,
                wb['w_br_rw'], wb['w_br_ssm'], wb['w_br_nsa'], wb['w_out'], per_batch, T)
    x2 = _ffn(x2, sc2, sh2, gt2, p['ln2'], wb['w_ffn_in'], wb['w_ffn_out'], per_batch, T)
    return x2.reshape(B, T, D_MODEL), (rw_shift, rw_wkv, conv_state, ssm_state, cmp_rows, sel_rows, win_buf)


def kernel(x_prompt, x_sample, cache_cmp_kv, cache_sel_kv, cache_win_kv, state_rwkv_shift, state_rwkv_wkv,
           state_ssm_conv, state_ssm, page_table, c_prompt, c_sample, w_ada, b_ada, ln1, ln2, w_in,
           rw_mu, rw_w0, rw_w2, rw_a0, rw_a2, rw_g2, rw_k_k, rw_k_a, rw_r_k, rw_lnx_g, rw_lnx_b,
           ssm_conv_w, ssm_conv_b, ssm_dt_bias, ssm_a_log, ssm_d, ssm_norm_g,
           nsa_q_norm, nsa_k_norm, nsa_cmp_pos, nsa_cmp_w1, nsa_cmp_b1, nsa_cmp_w2,
           w_br_rw, w_br_ssm, w_br_nsa, w_out, w_ffn_in, w_ffn_out):
    params = dict(w_ada=w_ada, b_ada=b_ada, ln1=ln1, ln2=ln2, w_in=w_in,
                  rw_mu=rw_mu, rw_w0=rw_w0, rw_w2=rw_w2, rw_a0=rw_a0, rw_a2=rw_a2, rw_g2=rw_g2,
                  rw_k_k=rw_k_k, rw_k_a=rw_k_a, rw_r_k=rw_r_k, rw_lnx_g=rw_lnx_g, rw_lnx_b=rw_lnx_b,
                  ssm_conv_w=ssm_conv_w, ssm_conv_b=ssm_conv_b, ssm_dt_bias=ssm_dt_bias, ssm_a_log=ssm_a_log,
                  ssm_d=ssm_d, ssm_norm_g=ssm_norm_g, nsa_q_norm=nsa_q_norm, nsa_k_norm=nsa_k_norm,
                  nsa_cmp_pos=nsa_cmp_pos, nsa_cmp_w1=nsa_cmp_w1, nsa_cmp_b1=nsa_cmp_b1, nsa_cmp_w2=nsa_cmp_w2,
                  w_br_rw=w_br_rw, w_br_ssm=w_br_ssm, w_br_nsa=w_br_nsa, w_out=w_out,
                  w_ffn_in=w_ffn_in, w_ffn_out=w_ffn_out)
    bp = x_prompt.shape[0]
    depth = w_in.shape[0]
    zero_shift = jnp.zeros((bp, RW_IN), F32)
    zero_wkv = jnp.zeros((bp, RW_HEADS, RW_HD, RW_HD), F32)
    zero_conv = jnp.zeros((bp, SSM_CONV - 1, SSM_XBC), F32)
    zero_ssm = jnp.zeros((bp, SSM_HEADS, SSM_HD, SSM_STATE), F32)
    xp, xs = x_prompt, x_sample
    st_p, st_s = [], []
    for l in range(depth):
        p = {name: arr[l] for name, arr in params.items()}
        wb = _prep_weights(p)
        xp, sp_l = _trunk_layer(xp, c_prompt, p, wb, zero_shift, zero_wkv, zero_conv, zero_ssm, _nsa_prompt_pallas, True)
        nsa_s = functools.partial(_nsa_sample_pallas, layer=l, cache_cmp_kv=cache_cmp_kv, cache_sel_kv=cache_sel_kv,
                                  win_buf=cache_win_kv[l], page_table=page_table)
        xs, ss_l = _trunk_layer(xs, c_sample, p, wb, state_rwkv_shift[l], state_rwkv_wkv[l], state_ssm_conv[l],
                                state_ssm[l], nsa_s, False)
        st_p.append(sp_l)
        st_s.append(ss_l)
    sp = [jnp.stack([s[i] for s in st_p]) for i in range(7)]
    ss = [jnp.stack([s[i] for s in st_s]) for i in range(7)]
    return (xp, xs, sp[4], ss[4], sp[5], ss[5], sp[6], ss[6], sp[0], ss[0], sp[1], ss[1], sp[2], ss[2], sp[3], ss[3])
```

```python
import functools

import jax
import jax.numpy as jnp
from jax import lax
from jax.experimental import pallas as pl
from jax.experimental.pallas import tpu as pltpu

F32 = jnp.float32
BF16 = jnp.bfloat16

D_MODEL = 1024
PAST_LEN = 16384
PAGE_SIZE = 128
RW_HEADS = 16
RW_HD = 64
RW_DIM = RW_HEADS * RW_HD
RW_W_LORA = 64
RW_A_LORA = 64
RW_G_LORA = 128
RW_IN = 3 * RW_DIM + RW_W_LORA + RW_A_LORA + RW_G_LORA
RW_SPLITS = (RW_DIM, 2 * RW_DIM, 3 * RW_DIM, 3 * RW_DIM + RW_W_LORA, 3 * RW_DIM + RW_W_LORA + RW_A_LORA)
RW_GN_EPS = 64e-5
SSM_DIM = 2 * D_MODEL
SSM_HD = 64
SSM_HEADS = SSM_DIM // SSM_HD
SSM_GROUPS = 4
SSM_RPG = SSM_HEADS // SSM_GROUPS
SSM_STATE = 128
SSM_CONV = 4
SSM_CHUNK = 128
SSM_XBC = SSM_DIM + 2 * SSM_GROUPS * SSM_STATE
SSM_IN = SSM_DIM + SSM_XBC + SSM_HEADS
NSA_HEADS = 16
NSA_HD = 64
NSA_KV_HEADS = 4
NSA_RPG = NSA_HEADS // NSA_KV_HEADS
NSA_DIM = NSA_HEADS * NSA_HD
NSA_KV_DIM = NSA_KV_HEADS * NSA_HD
CMP_STRIDE = 16
CMP_BLOCK = 2 * CMP_STRIDE
CMP_HIDDEN = 128
SEL_BLOCK = 64
N_SELECT = 16
WINDOW = 512
Q_BLOCK = 64
SEL_FORCE = 1e6
NSA_IN = NSA_DIM + 6 * NSA_KV_DIM + 3 * NSA_HEADS
N_BRANCH = 3
FFN_HIDDEN = ((8 * D_MODEL // 3 + 255) // 256) * 256
EPS = 1e-6

LANE = 128
VMEM_LIMIT = 56 * 1024 * 1024


def _cparams(*sem):
    return pltpu.CompilerParams(dimension_semantics=sem, vmem_limit_bytes=VMEM_LIMIT)


def _modnorm(x, g, sc, sh):
    y = x * lax.rsqrt(jnp.mean(x * x, axis=-1, keepdims=True) + EPS)
    return (y * g) * (1.0 + sc) + sh


def _mod_spec(per_batch, tm, tiles_per_batch):
    if per_batch:
        return pl.BlockSpec((None, 1, D_MODEL), lambda i: (i // tiles_per_batch, 0, 0))
    return pl.BlockSpec((tm, D_MODEL), lambda i: (i, 0))


def _full_spec(shape):
    return pl.BlockSpec(shape, lambda i: (0,) * len(shape))


def _ada_kernel(c_ref, w_ref, b_ref, o_ref):
    o_ref[...] = jnp.dot(c_ref[...].astype(BF16), w_ref[...], preferred_element_type=F32) + b_ref[...]


def _ada(c, w_bf, b):
    n, tn = c.shape[0], 1536
    return pl.pallas_call(
        _ada_kernel,
        grid=(6 * D_MODEL // tn,),
        in_specs=[pl.BlockSpec((n, D_MODEL), lambda j: (0, 0)),
                  pl.BlockSpec((D_MODEL, tn), lambda j: (0, j)),
                  pl.BlockSpec((1, tn), lambda j: (0, j))],
        out_specs=pl.BlockSpec((n, tn), lambda j: (0, j)),
        out_shape=jax.ShapeDtypeStruct((n, 6 * D_MODEL), F32),
        compiler_params=_cparams("arbitrary"),
        name="ada_mod",
    )(c, w_bf, b.reshape(1, -1))


def _norm_proj_kernel(x_ref, sc_ref, sh_ref, g_ref, w_ref, *o_refs, splits):
    h = _modnorm(x_ref[...], g_ref[...], sc_ref[...], sh_ref[...]).astype(BF16)
    for o_ref, (a, b) in zip(o_refs, splits):
        o_ref[...] = jnp.dot(h, w_ref[:, a:b], preferred_element_type=F32)


def _norm_proj(x2, sc, sh, g, w_bf, widths, per_batch, tokens_per_batch, name):
    n = x2.shape[0]
    tm = min(512, n)
    splits, a = [], 0
    for wd in widths:
        splits.append((a, a + wd))
        a += wd
    assert a == w_bf.shape[1] and n % tm == 0
    tpb = max(tokens_per_batch // tm, 1)
    return pl.pallas_call(
        functools.partial(_norm_proj_kernel, splits=tuple(splits)),
        grid=(n // tm,),
        in_specs=[pl.BlockSpec((tm, D_MODEL), lambda i: (i, 0)),
                  _mod_spec(per_batch, tm, tpb), _mod_spec(per_batch, tm, tpb),
                  _full_spec((1, D_MODEL)), _full_spec(w_bf.shape)],
        out_specs=[pl.BlockSpec((tm, wd), lambda i: (i, 0)) for wd in widths],
        out_shape=[jax.ShapeDtypeStruct((n, wd), F32) for wd in widths],
        compiler_params=_cparams("arbitrary"),
        name=name,
    )(x2, sc, sh, g.reshape(1, -1), w_bf)


def _merge_kernel(x_ref, gt_ref, orw_ref, ossm_ref, onsa_ref, gate_ref, wrw_ref, wssm_ref, wnsa_ref, wout_ref,
                  o_ref):
    gate = jax.nn.sigmoid(gate_ref[...])
    br = (jnp.dot(orw_ref[...].astype(BF16), wrw_ref[...], preferred_element_type=F32),
          jnp.dot(ossm_ref[...].astype(BF16), wssm_ref[...], preferred_element_type=F32),
          jnp.dot(onsa_ref[...].astype(BF16), wnsa_ref[...], preferred_element_type=F32))
    merged = sum(gate[:, i * D_MODEL:(i + 1) * D_MODEL] * br[i] for i in range(N_BRANCH))
    y = jnp.dot(merged.astype(BF16), wout_ref[...], preferred_element_type=F32)
    o_ref[...] = x_ref[...] + gt_ref[...] * y


def _merge(x2, gt, o_rw, o_ssm, o_nsa, gate, w_rw, w_ssm, w_nsa, w_out, per_batch, tokens_per_batch):
    n = x2.shape[0]
    tm = min(512, n)
    tpb = max(tokens_per_batch // tm, 1)
    row = lambda wd: pl.BlockSpec((tm, wd), lambda i: (i, 0))
    return pl.pallas_call(
        _merge_kernel,
        grid=(n // tm,),
        in_specs=[row(D_MODEL), _mod_spec(per_batch, tm, tpb), row(RW_DIM), row(SSM_DIM), row(NSA_DIM),
                  row(N_BRANCH * D_MODEL), _full_spec(w_rw.shape), _full_spec(w_ssm.shape),
                  _full_spec(w_nsa.shape), _full_spec(w_out.shape)],
        out_specs=row(D_MODEL),
        out_shape=jax.ShapeDtypeStruct((n, D_MODEL), F32),
        compiler_params=_cparams("arbitrary"),
        name="branch_merge",
    )(x2, gt, o_rw, o_ssm, o_nsa, gate, w_rw, w_ssm, w_nsa, w_out)


FFN_CHUNK = 256


def _ffn_kernel(x_ref, sc_ref, sh_ref, gt_ref, g_ref, win_ref, wout_ref, o_ref):
    x = x_ref[...]
    h = _modnorm(x, g_ref[...], sc_ref[...], sh_ref[...]).astype(BF16)
    acc = jnp.zeros(x.shape, F32)
    for c in range(FFN_HIDDEN // FFN_CHUNK):
        a = c * FFN_CHUNK
        up = jnp.dot(h, win_ref[:, a:a + FFN_CHUNK], preferred_element_type=F32)
        gf = jnp.dot(h, win_ref[:, FFN_HIDDEN + a:FFN_HIDDEN + a + FFN_CHUNK], preferred_element_type=F32)
        act = (jax.nn.silu(gf) * up).astype(BF16)
        acc = acc + jnp.dot(act, wout_ref[a:a + FFN_CHUNK, :], preferred_element_type=F32)
    o_ref[...] = x + gt_ref[...] * acc


def _ffn(x2, sc, sh, gt, g, w_in, w_out, per_batch, tokens_per_batch):
    n = x2.shape[0]
    tm = min(512, n)
    tpb = max(tokens_per_batch // tm, 1)
    row = pl.BlockSpec((tm, D_MODEL), lambda i: (i, 0))
    mod = _mod_spec(per_batch, tm, tpb)
    return pl.pallas_call(
        _ffn_kernel,
        grid=(n // tm,),
        in_specs=[row, mod, mod, mod, _full_spec((1, D_MODEL)), _full_spec(w_in.shape), _full_spec(w_out.shape)],
        out_specs=row,
        out_shape=jax.ShapeDtypeStruct((n, D_MODEL), F32),
        compiler_params=_cparams("arbitrary"),
        name="ffn",
    )(x2, sc, sh, gt, g.reshape(1, -1), w_in, w_out)


HIGHEST = lax.Precision.HIGHEST
NT_DIMS = (((1,), (1,)), ((), ()))
NEG_BIG = -1e30
F32_TINY = float(jnp.finfo(jnp.float32).tiny)


def _split3(x):
    hi = x.astype(BF16)
    r = x - hi.astype(F32)
    mid = r.astype(BF16)
    return hi, mid, (r - mid.astype(F32)).astype(BF16)


def _dot_f32_nt(a3, b3):
    pairs = ((2, 0), (0, 2), (1, 1), (1, 0), (0, 1), (0, 0))
    return sum(lax.dot_general(a3[i], b3[j], NT_DIMS, preferred_element_type=F32) for i, j in pairs)


def _dot_f32_exact_rhs(a, b_bf):
    return sum(jnp.dot(t, b_bf, preferred_element_type=F32) for t in reversed(_split3(a)))


def _seg_rmsnorm(x, seg, seg_t, gain):
    ss = jnp.dot(x * x, seg, precision=HIGHEST, preferred_element_type=F32)
    inv = lax.rsqrt(ss * (1.0 / NSA_HD) + EPS)
    return x * jnp.dot(inv, seg_t, precision=HIGHEST, preferred_element_type=F32) * gain


def _nsa_prep_kernel(q_ref, sel_ref, win_ref, seg_ref, segt_ref, qg_ref, kg_ref,
                     qh_ref, ksh_ref, vsh_ref, kwh_ref, vwh_ref, selo_ref, wino_ref):
    seg, segt = seg_ref[...], segt_ref[...]
    qn = _seg_rmsnorm(q_ref[...], seg, segt, qg_ref[...]) * (NSA_HD ** -0.5)
    for h in range(NSA_HEADS):
        qh_ref[h] = qn[:, h * NSA_HD:(h + 1) * NSA_HD]
    kg = kg_ref[...]
    for src, dst, kh_ref, vh_ref, row in ((sel_ref, selo_ref, ksh_ref, vsh_ref, 1), (win_ref, wino_ref, kwh_ref, vwh_ref, 2)):
        x = src[...]
        kn = _seg_rmsnorm(x[:, :NSA_KV_DIM], seg[:NSA_KV_DIM], segt[:, :NSA_KV_DIM], kg[row:row + 1])
        v = x[:, NSA_KV_DIM:]
        dst[:, :NSA_KV_DIM] = kn
        dst[:, NSA_KV_DIM:] = v
        ones_col = (lax.broadcasted_iota(jnp.int32, (x.shape[0], NSA_HD), 1) == 0).astype(BF16)
        for g in range(NSA_KV_HEADS):
            kh_ref[g] = kn[:, g * NSA_HD:(g + 1) * NSA_HD].astype(BF16)
            vh_ref[g] = jnp.concatenate([v[:, g * NSA_HD:(g + 1) * NSA_HD].astype(BF16), ones_col], axis=1)


def _seg_mats():
    c = jnp.arange(NSA_DIM)[:, None] // NSA_HD
    seg = (c == jnp.arange(LANE)[None, :]).astype(F32)
    return seg, seg.T


def _nsa_prep(q2, sel2, win2, q_norm, k_norm, B, T):
    tm = min(512, T)
    tpb = T // tm
    seg, segt = _seg_mats()
    row = lambda wd: pl.BlockSpec((tm, wd), lambda i: (i, 0))
    hm = lambda nh, wd=NSA_HD: pl.BlockSpec((None, nh, tm, wd), lambda i: (i // tpb, 0, i % tpb, 0))
    ks = jax.ShapeDtypeStruct((B, NSA_KV_HEADS, T, NSA_HD), BF16)
    vs = jax.ShapeDtypeStruct((B, NSA_KV_HEADS, T, 2 * NSA_HD), BF16)
    rows = jax.ShapeDtypeStruct((B * T, 2 * NSA_KV_DIM), F32)
    return pl.pallas_call(
        _nsa_prep_kernel,
        grid=(B * T // tm,),
        in_specs=[row(NSA_DIM), row(2 * NSA_KV_DIM), row(2 * NSA_KV_DIM), _full_spec(seg.shape), _full_spec(segt.shape),
                  _full_spec((1, NSA_DIM)), _full_spec((3, NSA_KV_DIM))],
        out_specs=[hm(NSA_HEADS), hm(NSA_KV_HEADS), hm(NSA_KV_HEADS, 2 * NSA_HD), hm(NSA_KV_HEADS),
                   hm(NSA_KV_HEADS, 2 * NSA_HD), row(2 * NSA_KV_DIM), row(2 * NSA_KV_DIM)],
        out_shape=[jax.ShapeDtypeStruct((B, NSA_HEADS, T, NSA_HD), F32), ks, vs, ks, vs, rows, rows],
        compiler_params=_cparams("arbitrary"),
        name="nsa_prep",
    )(q2, sel2, win2, seg, segt, jnp.tile(q_norm, NSA_HEADS).reshape(1, -1), jnp.tile(k_norm, (1, NSA_KV_HEADS)))


def _nsa_cmp_kernel(xk_ref, xv_ref, w1_ref, pos_ref, b1_ref, w2_ref, kn_ref, kc_ref, vc_ref, *, n_valid):
    half = CMP_STRIDE * NSA_HD
    for c, x_ref in ((0, xk_ref), (1, xv_ref)):
        x = x_ref[...]
        const = jnp.dot(pos_ref[c].astype(BF16), w1_ref[c], preferred_element_type=F32)[0:1] + b1_ref[c:c + 1]
        top = jnp.dot(x, w1_ref[c, :half, :], preferred_element_type=F32)
        bot = jnp.dot(x, w1_ref[c, half:, :], preferred_element_type=F32)
        bot_next = jnp.concatenate([bot[1:], jnp.zeros((1, CMP_HIDDEN), F32)], axis=0)
        hid = jax.nn.gelu(top + bot_next + const)
        kv = jnp.dot(hid.astype(BF16), w2_ref[c], preferred_element_type=F32)
        if c == 0:
            kv = kv * lax.rsqrt(jnp.mean(kv * kv, axis=-1, keepdims=True) + EPS) * kn_ref[...]
            for i, term in enumerate(_split3(kv)):
                kc_ref[i] = term
        else:
            vc_ref[...] = kv.astype(BF16)


def _nsa_compress(xh, w1_bf, pos, b1, w2_bf, k_norm0):
    B, _, nh, _ = xh.shape
    blk = lambda off: pl.BlockSpec((None, None, nh, CMP_STRIDE * NSA_HD), lambda b, g: (b, off + g, 0, 0))
    full = lambda shape: pl.BlockSpec(shape, lambda b, g: (0,) * len(shape))
    out = pl.BlockSpec((None, None, nh, NSA_HD), lambda b, g: (b, g, 0, 0))
    return pl.pallas_call(
        functools.partial(_nsa_cmp_kernel, n_valid=nh - 1),
        grid=(B, NSA_KV_HEADS),
        in_specs=[blk(0), blk(NSA_KV_HEADS), full(w1_bf.shape), full((2, 8, CMP_BLOCK * NSA_HD)), full((2, CMP_HIDDEN)),
                  full(w2_bf.shape), full((1, NSA_HD))],
        out_specs=[pl.BlockSpec((None, None, 3, nh, NSA_HD), lambda b, g: (b, g, 0, 0, 0)), out],
        out_shape=[jax.ShapeDtypeStruct((B, NSA_KV_HEADS, 3, nh, NSA_HD), BF16),
                   jax.ShapeDtypeStruct((B, NSA_KV_HEADS, nh, NSA_HD), BF16)],
        compiler_params=_cparams("arbitrary", "arbitrary"),
        name="nsa_compress",
    )(xh, xh, w1_bf, jnp.broadcast_to(pos.reshape(2, 1, -1), (2, 8, CMP_BLOCK * NSA_HD)), b1, w2_bf, k_norm0.reshape(1, -1))


NSA_TQ = 128
NSA_TK = 512


def _softmax_direct(s, mask):
    sm = jnp.where(mask[None], s, NEG_BIG)
    m = jnp.max(sm, axis=-1, keepdims=True)
    e = jnp.where(mask[None], jnp.exp(sm - m), 0.0)
    return e / jnp.maximum(jnp.sum(e, axis=-1, keepdims=True), F32_TINY)


def _top_blocks(score, n_top):
    ns = score.shape[-1]
    lane = lax.broadcasted_iota(jnp.int32, score.shape, 1).astype(F32)
    sel = jnp.zeros(score.shape, F32)
    for _ in range(n_top):
        m = jnp.max(score, axis=-1, keepdims=True)
        first = jnp.min(jnp.where(score == m, lane, float(ns)), axis=-1, keepdims=True)
        hit = lane == first
        sel = jnp.where(hit, 1.0, sel)
        score = jnp.where(hit, -jnp.inf, score)
    return sel


def _nsa_prompt_kernel(q_ref, kc_ref, vc_ref, c2s_ref, ks_ref, vs_ref, kw_ref, vw_ref, g_ref, o_ref, bias_ref):
    R, TQ, TK = NSA_RPG, NSA_TQ, NSA_TK
    T = ks_ref.shape[0]
    nc = kc_ref.shape[1]
    i = pl.program_id(2)
    q0 = i * TQ
    D = NSA_HD
    q = q_ref[...].reshape(R * TQ, D)
    qb = q.astype(BF16)
    t_row = q0 + lax.broadcasted_iota(jnp.int32, (TQ, 1), 0)

    wk = WINDOW + TQ
    ws = pl.multiple_of(jnp.maximum(q0 - WINDOW, 0), TQ)
    kw = kw_ref[pl.ds(ws, wk), :]
    vw = vw_ref[pl.ds(ws, wk), :]
    kpos = ws + lax.broadcasted_iota(jnp.int32, (1, wk), 1)
    sw = lax.dot_general(qb, kw, NT_DIMS, preferred_element_type=F32).reshape(R, TQ, wk)
    sw = sw + jnp.where((kpos <= t_row) & (kpos > t_row - WINDOW), 0.0, NEG_BIG)[None]
    ew = jnp.exp(sw - jnp.max(sw, axis=-1, keepdims=True))
    ow = jnp.dot(ew.reshape(R * TQ, wk).astype(BF16), vw, preferred_element_type=F32).reshape(R, TQ, 2 * D)
    o_win = ow[..., :D] / ow[..., D:D + 1]

    s = _dot_f32_nt(_split3(q), (kc_ref[0], kc_ref[1], kc_ref[2])).reshape(R, TQ, nc)
    cmp_end = lax.broadcasted_iota(jnp.int32, (1, nc), 1) * CMP_STRIDE + (CMP_BLOCK - 1)
    p = _softmax_direct(s, cmp_end <= t_row)
    o_cmp = jnp.dot(p.reshape(R * TQ, nc).astype(BF16), vc_ref[...], preferred_element_type=F32).reshape(R, TQ, D)
    imp = _dot_f32_exact_rhs(jnp.sum(p, axis=0), c2s_ref[...])
    gs = jax.nn.sigmoid(g_ref[...])
    o_cw = [gs[:, r:r + 1] * o_cmp[r] + gs[:, 2 * R + r:2 * R + r + 1] * o_win[r] for r in range(R)]
    ns = imp.shape[-1]
    blk = lax.broadcasted_iota(jnp.int32, (1, ns), 1)
    cur = t_row // SEL_BLOCK
    forced = (blk == 0) | (blk == cur) | (blk == cur - 1)
    score = jnp.where(forced, SEL_FORCE, jnp.where(blk * SEL_BLOCK <= t_row, imp, -SEL_FORCE))
    sel = _top_blocks(score, min(N_SELECT, ns)).astype(BF16)
    per_tile = TK // SEL_BLOCK
    key_blk = lax.broadcasted_iota(jnp.int32, (ns, TK), 1) // SEL_BLOCK
    row_blk = lax.broadcasted_iota(jnp.int32, (ns, TK), 0)
    for j in range(T // TK):
        @pl.when(j * TK < q0 + TQ)
        def _(j=j):
            expand = (key_blk + j * per_tile == row_blk).astype(BF16)
            member = jnp.dot(sel, expand, preferred_element_type=F32) > 0.5
            kpos_j = j * TK + lax.broadcasted_iota(jnp.int32, (1, TK), 1)
            bias_ref[j] = jnp.where(member & (kpos_j <= t_row), 0.0, NEG_BIG)

    def body(j, carry):
        m, acc = carry
        start = pl.multiple_of(j * TK, TK)
        k = ks_ref[pl.ds(start, TK), :]
        v = vs_ref[pl.ds(start, TK), :]
        sm = lax.dot_general(qb, k, NT_DIMS, preferred_element_type=F32).reshape(R, TQ, TK) + bias_ref[j][None]
        m_new = jnp.maximum(m, jnp.max(sm, axis=-1, keepdims=True))
        pj = jnp.exp(sm - m_new).astype(BF16)
        pv = jnp.dot(pj.reshape(R * TQ, TK), v, preferred_element_type=F32).reshape(R, TQ, 2 * D)
        return m_new, jnp.exp(m - m_new) * acc + pv

    carry0 = (jnp.full((R, TQ, 1), NEG_BIG, F32), jnp.zeros((R, TQ, 2 * D), F32))
    _, acc = lax.fori_loop(0, (q0 + TQ + TK - 1) // TK, body, carry0)
    o_sel = acc[..., :D] / acc[..., D:D + 1]
    for r in range(R):
        o_ref[:, r * D:(r + 1) * D] = o_cw[r] + gs[:, R + r:R + r + 1] * o_sel[r]


def _cmp_to_sel_padded(nh, ns):
    i = jnp.arange(nh)[:, None] * CMP_STRIDE
    j = jnp.arange(ns)[None, :] * SEL_BLOCK
    ov = jnp.minimum(i + CMP_BLOCK, j + SEL_BLOCK) - jnp.maximum(i, j)
    return (jnp.clip(ov, 0, None).astype(F32) / CMP_BLOCK).astype(BF16)


def _nsa_prompt_attn(qh, kc, vc, ksh, vsh, kwh, vwh, gates4):
    B, _, T, _ = qh.shape
    nh = kc.shape[3]
    ns = T // SEL_BLOCK
    c2s = _cmp_to_sel_padded(nh, ns)
    kv = lambda n, wd=NSA_HD: pl.BlockSpec((None, None, n, wd), lambda b, g, i: (b, g, 0, 0))
    return pl.pallas_call(
        _nsa_prompt_kernel,
        grid=(B, NSA_KV_HEADS, T // NSA_TQ),
        in_specs=[pl.BlockSpec((None, NSA_RPG, NSA_TQ, NSA_HD), lambda b, g, i: (b, g, i, 0)),
                  pl.BlockSpec((None, None, 3, nh, NSA_HD), lambda b, g, i: (b, g, 0, 0, 0)), kv(nh),
                  pl.BlockSpec(c2s.shape, lambda b, g, i: (0, 0)),
                  kv(T), kv(T, 2 * NSA_HD), kv(T), kv(T, 2 * NSA_HD),
                  pl.BlockSpec((None, None, NSA_TQ, 16), lambda b, g, i: (b, g, i, 0))],
        out_specs=pl.BlockSpec((None, NSA_TQ, NSA_RPG * NSA_HD), lambda b, g, i: (b, i, g)),
        out_shape=jax.ShapeDtypeStruct((B, T, NSA_DIM), F32),
        scratch_shapes=[pltpu.VMEM((T // NSA_TK, NSA_TQ, NSA_TK), F32)],
        compiler_params=_cparams("arbitrary", "arbitrary", "arbitrary"),
        name="nsa_prompt_attn",
    )(qh, kc, vc, c2s, ksh, vsh, kwh, vwh, gates4)


def _nsa_prompt_pallas(q2, cmp2, sel2, win2, gates2, p, wb, B, T):
    qh, ksh, vsh, kwh, vwh, sel_rows, win_rows = _nsa_prep(q2, sel2, win2, p['nsa_q_norm'], p['nsa_k_norm'], B, T)
    nh = T // CMP_STRIDE
    xh = cmp2.astype(BF16).reshape(B, nh, CMP_STRIDE, 2 * NSA_KV_HEADS, NSA_HD)
    xh = xh.transpose(0, 3, 1, 2, 4).reshape(B, 2 * NSA_KV_HEADS, nh, CMP_STRIDE * NSA_HD)
    kc, vc = _nsa_compress(xh, wb['cmp_w1'], p['nsa_cmp_pos'], p['nsa_cmp_b1'], wb['cmp_w2'], p['nsa_k_norm'][0])
    gates4 = gates2[:, :NSA_KV_HEADS * 16].reshape(B, T, NSA_KV_HEADS, 16).transpose(0, 2, 1, 3)
    o = _nsa_prompt_attn(qh, kc, vc, ksh, vsh, kwh, vwh, gates4)
    shp = (B, T, 2, NSA_KV_HEADS, NSA_HD)
    return o, (cmp2.reshape(shp), sel_rows.reshape(shp), win_rows.reshape(shp)[:, -WINDOW:])


HALF_COLS = CMP_STRIDE * 2 * NSA_KV_DIM
PAGE_HALVES = PAGE_SIZE // CMP_STRIDE
N_PAIR = 2 * NSA_KV_HEADS // 2
CMP_PAGES = 32
SEL_PAGES = 8


def _half_proj_body(x, w_ref, o_ref):
    for pr in range(N_PAIR):
        lhs = jnp.concatenate([x[:, s * 2 * NSA_KV_DIM + pr * LANE:s * 2 * NSA_KV_DIM + (pr + 1) * LANE]
                               for s in range(CMP_STRIDE)], axis=1)
        o_ref[:, pr * 4 * CMP_HIDDEN:(pr + 1) * 4 * CMP_HIDDEN] = jnp.dot(lhs, w_ref[pr], preferred_element_type=F32)


def _half_proj_paged_kernel(pt_ref, *refs):
    pages, (w_ref, o_ref) = refs[:CMP_PAGES], refs[CMP_PAGES:]
    kw = 2 * NSA_KV_DIM
    acc = [None] * N_PAIR
    for s in range(CMP_STRIDE):
        for pr in range(N_PAIR):
            x = jnp.concatenate([pg[pr, pl.ds(s, PAGE_HALVES, stride=CMP_STRIDE), :] for pg in pages], axis=0)
            part = jnp.dot(x.astype(BF16), w_ref[pr, s * LANE:(s + 1) * LANE, :], preferred_element_type=F32)
            acc[pr] = part if acc[pr] is None else acc[pr] + part
    for pr in range(N_PAIR):
        o_ref[:, pr * kw:(pr + 1) * kw] = acc[pr]


def _half_proj_rows_kernel(x_ref, w_ref, o_ref):
    _half_proj_body(x_ref[...].astype(BF16), w_ref, o_ref)


def _pair_weights(w1):
    top, bot = w1[:, :CMP_STRIDE], w1[:, CMP_STRIDE:]
    tb = jnp.concatenate([top, bot], axis=-1)
    z = jnp.zeros_like(tb)
    par0 = jnp.concatenate([tb, z], axis=-1)
    par1 = jnp.concatenate([z, tb], axis=-1)
    w = jnp.stack([par0, par1], axis=2)
    w = w.reshape(2, CMP_STRIDE * 2 * NSA_HD, 4 * CMP_HIDDEN)
    return jnp.stack([w[0], w[0], w[1], w[1]], axis=0).astype(BF16)


def _half_proj_paged(pool4, layer, page_table, w_pair):
    B, n_pages = page_table.shape
    m = CMP_PAGES * PAGE_HALVES
    page_spec = lambda n: pl.BlockSpec((None, None, N_PAIR, PAGE_SIZE, LANE),
                                       lambda b, j, pt: (layer, pt[b, j * CMP_PAGES + n], 0, 0, 0))
    grid_spec = pltpu.PrefetchScalarGridSpec(
        num_scalar_prefetch=1, grid=(B, n_pages // CMP_PAGES),
        in_specs=[page_spec(n) for n in range(CMP_PAGES)] + [pl.BlockSpec(w_pair.shape, lambda b, j, pt: (0, 0, 0))],
        out_specs=pl.BlockSpec((None, m, 8 * 2 * CMP_HIDDEN), lambda b, j, pt: (b, j, 0)))
    return pl.pallas_call(
        _half_proj_paged_kernel, grid_spec=grid_spec,
        out_shape=jax.ShapeDtypeStruct((B, n_pages * PAGE_HALVES, 8 * 2 * CMP_HIDDEN), F32),
        compiler_params=_cparams("arbitrary", "arbitrary"),
        name="nsa_half_proj_paged",
    )(page_table, *([pool4] * CMP_PAGES), w_pair)


def _half_proj_rows(x, w_pair):
    m = x.shape[0]
    return pl.pallas_call(
        _half_proj_rows_kernel, grid=(1,),
        in_specs=[_full_spec(x.shape), _full_spec(w_pair.shape)],
        out_specs=_full_spec((m, 8 * 2 * CMP_HIDDEN)),
        out_shape=jax.ShapeDtypeStruct((m, 8 * 2 * CMP_HIDDEN), F32),
        compiler_params=_cparams("arbitrary"),
        name="nsa_half_proj_new",
    )(x, w_pair)


def _cmp_finish_kernel(tk_ref, tv_ref, nk_ref, nv_ref, w1_ref, pos_ref, b1_ref, w2_ref, kn_ref, kc_ref, vc_ref):
    for c, t_ref, n_ref in ((0, tk_ref, nk_ref), (1, tv_ref, nv_ref)):
        const = jnp.dot(pos_ref[c].astype(BF16), w1_ref[c], preferred_element_type=F32)[0:1] + b1_ref[c:c + 1]
        tb = t_ref[...]
        top = tb[:, :CMP_HIDDEN]
        bot_next = jnp.concatenate([tb[1:, CMP_HIDDEN:], n_ref[0:1, CMP_HIDDEN:]], axis=0)
        hid = jax.nn.gelu(top + bot_next + const)
        kv = jnp.dot(hid.astype(BF16), w2_ref[c], preferred_element_type=F32)
        if c == 0:
            kn = kv * lax.rsqrt(jnp.mean(kv * kv, axis=-1, keepdims=True) + EPS) * kn_ref[...]
            for i, term in enumerate(_split3(kn)):
                kc_ref[i] = term
        else:
            vc_ref[...] = kv.astype(BF16)


def _cmp_finish(tb_past, tb_new, w1_bf, pos, b1, w2_bf, k_norm0):
    B, nh, _ = tb_past.shape
    wd = 2 * CMP_HIDDEN
    full = lambda shape: pl.BlockSpec(shape, lambda b, g: (0,) * len(shape))
    past = lambda off: pl.BlockSpec((None, nh, wd), lambda b, g: (b, 0, off + g))
    new = lambda off: pl.BlockSpec((None, 8, wd), lambda b, g: (b, 0, off + g))
    out = pl.BlockSpec((None, None, nh, NSA_HD), lambda b, g: (b, g, 0, 0))
    return pl.pallas_call(
        _cmp_finish_kernel, grid=(B, NSA_KV_HEADS),
        in_specs=[past(0), past(NSA_KV_HEADS), new(0), new(NSA_KV_HEADS), full(w1_bf.shape),
                  full((2, 8, CMP_BLOCK * NSA_HD)), full((2, CMP_HIDDEN)), full(w2_bf.shape), full((1, NSA_HD))],
        out_specs=[pl.BlockSpec((None, None, 3, nh, NSA_HD), lambda b, g: (b, g, 0, 0, 0)), out],
        out_shape=[jax.ShapeDtypeStruct((B, NSA_KV_HEADS, 3, nh, NSA_HD), BF16),
                   jax.ShapeDtypeStruct((B, NSA_KV_HEADS, nh, NSA_HD), BF16)],
        compiler_params=_cparams("arbitrary", "arbitrary"),
        name="nsa_cmp_finish",
    )(tb_past, tb_past, tb_new, tb_new, w1_bf, jnp.broadcast_to(pos.reshape(2, 1, -1), (2, 8, CMP_BLOCK * NSA_HD)),
      b1, w2_bf, k_norm0.reshape(1, -1))


def _nsa_sample_kernel(pt_ref, *refs, T, ns_pad):
    pages = refs[:SEL_PAGES]
    (q_ref, kc_ref, vc_ref, c2s_ref, newsel_ref, wincache_ref, newwin_ref, gate_ref, o_ref,
     sel_scr, m_scr, l_scr, acc_scr, ocmp_scr, owin_scr) = refs[SEL_PAGES:]
    H, G, R, D = NSA_HEADS, NSA_KV_HEADS, NSA_RPG, NSA_HD
    rows, gr = H * T, R * T
    j = pl.program_id(1)
    n_steps = pl.num_programs(1)
    tk = SEL_PAGES * PAGE_SIZE
    row_t = PAST_LEN + lax.broadcasted_iota(jnp.int32, (gr, 1), 0) % T
    rs = lambda g: slice(g * gr, (g + 1) * gr)
    kT = lambda x, g: x[g * D:(g + 1) * D]
    vT = lambda x, g: x[G * D + g * D:G * D + (g + 1) * D]

    @pl.when(j == 0)
    def _():
        q = q_ref[...].reshape(rows, D)
        qb = q.astype(BF16)
        nc = kc_ref.shape[2]
        t_q = PAST_LEN + lax.broadcasted_iota(jnp.int32, (T, 1), 0)
        cmp_end = lax.broadcasted_iota(jnp.int32, (1, nc), 1) * CMP_STRIDE + (CMP_BLOCK - 1)
        blk = lax.broadcasted_iota(jnp.int32, (1, ns_pad), 1)
        cur = t_q // SEL_BLOCK
        forced = (blk == 0) | (blk == cur) | (blk == cur - 1)
        valid = blk * SEL_BLOCK <= t_q
        wc = wincache_ref[...].astype(BF16)
        nw = newwin_ref[...].astype(BF16)
        kpos = jnp.concatenate([PAST_LEN - WINDOW + lax.broadcasted_iota(jnp.int32, (1, WINDOW), 1),
                                PAST_LEN + lax.broadcasted_iota(jnp.int32, (1, T), 1)], axis=1)
        win_mask = (kpos <= row_t) & (kpos > row_t - WINDOW)
        s_cmp = [_dot_f32_nt(_split3(q[rs(g)]), (kc_ref[g, 0], kc_ref[g, 1], kc_ref[g, 2])).reshape(R, T, nc)
                 for g in range(G)]
        s_win = [jnp.concatenate([jnp.dot(qb[rs(g)], kT(wc, g), preferred_element_type=F32),
                                  lax.dot_general(qb[rs(g)], nw[:, g * D:(g + 1) * D], NT_DIMS,
                                                  preferred_element_type=F32)], axis=1) for g in range(G)]
        p_cmp = [_softmax_direct(s_cmp[g], cmp_end <= t_q) for g in range(G)]
        p_win = [_softmax_direct(s_win[g][None], win_mask)[0].astype(BF16) for g in range(G)]
        scores = []
        for g in range(G):
            ocmp_scr[rs(g), :] = jnp.dot(p_cmp[g].reshape(gr, nc).astype(BF16), vc_ref[g], preferred_element_type=F32)
            imp = _dot_f32_exact_rhs(jnp.sum(p_cmp[g], axis=0), c2s_ref[...])
            scores.append(jnp.where(forced, SEL_FORCE, jnp.where(valid, imp, -SEL_FORCE)))
            owin_scr[rs(g), :] = (lax.dot_general(p_win[g][:, :WINDOW], vT(wc, g), NT_DIMS, preferred_element_type=F32)
                                  + jnp.dot(p_win[g][:, WINDOW:], nw[:, G * D + g * D:G * D + (g + 1) * D],
                                            preferred_element_type=F32))
        sel = _top_blocks(jnp.concatenate(scores, axis=0), N_SELECT).reshape(G, 1, T, ns_pad)
        sel_scr[...] = jnp.broadcast_to(sel, (G, R, T, ns_pad)).reshape(rows, ns_pad).T
        m_scr[...] = jnp.full(m_scr.shape, NEG_BIG, F32)
        l_scr[...] = jnp.zeros(l_scr.shape, F32)
        acc_scr[...] = jnp.zeros(acc_scr.shape, F32)

    def flash(scores, v_dots, masks):
        stats = []
        for g in range(G):
            sm = jnp.where(masks[g], scores[g], NEG_BIG)
            m_old = m_scr[rs(g), :]
            m_new = jnp.maximum(m_old, jnp.max(sm, axis=-1, keepdims=True))
            p = jnp.where(masks[g], jnp.exp(sm - m_new), 0.0)
            stats.append((m_new, jnp.exp(m_old - m_new), p))
        pvs = [v_dots[g](stats[g][2].astype(BF16)) for g in range(G)]
        for g in range(G):
            m_new, alpha, p = stats[g]
            l_scr[rs(g), :] = alpha * l_scr[rs(g), :] + jnp.sum(p, axis=-1, keepdims=True)
            acc_scr[rs(g), :] = alpha * acc_scr[rs(g), :] + pvs[g]
            m_scr[rs(g), :] = m_new

    qb = q_ref[...].reshape(rows, D).astype(BF16)
    kv = jnp.concatenate([pg[...].astype(BF16) for pg in pages], axis=1)
    per_step = tk // SEL_BLOCK
    tn_dims = (((0,), (0,)), ((), ()))
    expand = (lax.broadcasted_iota(jnp.int32, (per_step, tk), 1) // SEL_BLOCK
              == lax.broadcasted_iota(jnp.int32, (per_step, tk), 0)).astype(BF16)
    sel_step = sel_scr[pl.ds(pl.multiple_of(j * per_step, per_step), per_step), :].astype(BF16)
    member = lax.dot_general(sel_step, expand, tn_dims, preferred_element_type=F32) > 0.5
    flash([jnp.dot(qb[rs(g)], kT(kv, g), preferred_element_type=F32) for g in range(G)],
          [lambda p, g=g: lax.dot_general(p, vT(kv, g), NT_DIMS, preferred_element_type=F32) for g in range(G)],
          [member[rs(g)] for g in range(G)])

    @pl.when(j == n_steps - 1)
    def _():
        new_blk = PAST_LEN // SEL_BLOCK
        kpos = PAST_LEN + lax.broadcasted_iota(jnp.int32, (1, T), 1)
        ns_rows = newsel_ref[...].astype(BF16)
        base = new_blk - new_blk % 8
        pick = (lax.broadcasted_iota(jnp.int32, (8, T), 0) == new_blk % 8).astype(BF16)
        new_member = lax.dot_general(sel_scr[base:base + 8, :].astype(BF16), pick, tn_dims,
                                     preferred_element_type=F32) > 0.5
        flash([lax.dot_general(qb[rs(g)], ns_rows[:, g * D:(g + 1) * D], NT_DIMS, preferred_element_type=F32)
               for g in range(G)],
              [lambda p, g=g: jnp.dot(p, ns_rows[:, G * D + g * D:G * D + (g + 1) * D], preferred_element_type=F32)
               for g in range(G)],
              [new_member[rs(g)] & (kpos <= row_t) for g in range(G)])
        o_sel = acc_scr[...] / jnp.maximum(l_scr[...], F32_TINY)
        gs = jax.nn.sigmoid(gate_ref[...])
        o = gs[:, 0:1] * ocmp_scr[...] + gs[:, 1:2] * o_sel + gs[:, 2:3] * owin_scr[...]
        for h in range(H):
            o_ref[:, h * D:(h + 1) * D] = o[h * T:(h + 1) * T]


def _nsa_sample_attn(qh, kc, vc, pool_sel, layer, page_table, new_sel, win_cache, new_win, gate_rows):
    B, H, T, D = qh.shape
    n_pages = page_table.shape[1]
    nc = kc.shape[3]
    ns = -(-(PAST_LEN + T) // SEL_BLOCK)
    ns_pad = -(-ns // LANE) * LANE
    c2s = _cmp_to_sel_padded(nc, ns_pad)
    rows = H * T
    cst = lambda shape: pl.BlockSpec(shape, lambda b, j, pt: (0,) * len(shape))
    per_b = lambda shape: pl.BlockSpec((None,) + shape, lambda b, j, pt: (b,) + (0,) * len(shape))
    page_spec = lambda n: pl.BlockSpec((None, None, 2 * NSA_KV_DIM, PAGE_SIZE),
                                       lambda b, j, pt: (layer, pt[b, j * SEL_PAGES + n], 0, 0))
    grid_spec = pltpu.PrefetchScalarGridSpec(
        num_scalar_prefetch=1, grid=(B, n_pages // SEL_PAGES),
        in_specs=[page_spec(n) for n in range(SEL_PAGES)] + [
            per_b((H, T, D)), per_b((NSA_KV_HEADS, 3, nc, D)), per_b((NSA_KV_HEADS, nc, D)), cst(c2s.shape),
            per_b((T, 2 * NSA_KV_DIM)), per_b((2 * NSA_KV_DIM, WINDOW)), per_b((T, 2 * NSA_KV_DIM)), per_b((rows, 3))],
        out_specs=per_b((T, NSA_DIM)),
        scratch_shapes=[pltpu.VMEM((ns_pad, rows), F32), pltpu.VMEM((rows, 1), F32), pltpu.VMEM((rows, 1), F32),
                        pltpu.VMEM((rows, D), F32), pltpu.VMEM((rows, D), F32), pltpu.VMEM((rows, D), F32)])
    return pl.pallas_call(
        functools.partial(_nsa_sample_kernel, T=T, ns_pad=ns_pad), grid_spec=grid_spec,
        out_shape=jax.ShapeDtypeStruct((B, T, NSA_DIM), F32),
        compiler_params=_cparams("arbitrary", "arbitrary"),
        name="nsa_sample_attn",
    )(page_table, *([pool_sel] * SEL_PAGES), qh, kc, vc, c2s, new_sel, win_cache, new_win, gate_rows)


def _nsa_sample_pallas(q2, cmp2, sel2, win2, gates2, p, wb, B, T, layer, cache_cmp_kv, cache_sel_kv, win_buf, page_table):
    qh, _, _, _, _, sel_rows, win_rows = _nsa_prep(q2, sel2, win2, p['nsa_q_norm'], p['nsa_k_norm'], B, T)
    depth, n_pool = cache_cmp_kv.shape[:2]
    pool_cmp = cache_cmp_kv.transpose(0, 1, 3, 4, 5, 2).reshape(depth, n_pool, N_PAIR, LANE, PAGE_SIZE)
    pool_cmp = jnp.swapaxes(pool_cmp, 3, 4)
    tb_past = _half_proj_paged(pool_cmp, layer, page_table, wb['cmp_pair'])
    new_half = jnp.pad(cmp2.reshape(B, 1, T * 2 * NSA_KV_DIM), ((0, 0), (0, 7), (0, HALF_COLS - T * 2 * NSA_KV_DIM)))
    tb_new = _half_proj_rows(new_half.reshape(B * 8, HALF_COLS), wb['cmp_pair']).reshape(B, 8, -1)
    kc, vc = _cmp_finish(tb_past, tb_new, wb['cmp_w1'], p['nsa_cmp_pos'], p['nsa_cmp_b1'], wb['cmp_w2'], p['nsa_k_norm'][0])
    pool_sel = cache_sel_kv.transpose(0, 1, 3, 4, 5, 2).reshape(depth, n_pool, 2 * NSA_KV_DIM, PAGE_SIZE)
    win_cache = win_buf.transpose(0, 2, 3, 4, 1).reshape(B, 2 * NSA_KV_DIM, WINDOW)
    g = gates2[:, :NSA_KV_HEADS * 16].reshape(B, T, NSA_KV_HEADS, 4, NSA_RPG)[:, :, :, :3]
    gate_rows = g.transpose(0, 2, 4, 1, 3).reshape(B, NSA_HEADS * T, 3)
    sel3, win3 = sel_rows.reshape(B, T, -1), win_rows.reshape(B, T, -1)
    o = _nsa_sample_attn(qh, kc, vc, pool_sel, layer, page_table, sel3, win_cache, win3, gate_rows)
    shp = (B, T, 2, NSA_KV_HEADS, NSA_HD)
    win_out = jnp.concatenate([win_buf, win_rows.reshape(shp)], axis=1)[:, -WINDOW:]
    return o, (cmp2.reshape(shp), sel_rows.reshape(shp), win_out)


RW_CHUNK = 64


def _heads(x):
    return jnp.stack([x[:, h * RW_HD:(h + 1) * RW_HD] for h in range(x.shape[1] // RW_HD)], axis=0)


def _unheads(x):
    return jnp.concatenate([x[h] for h in range(x.shape[0])], axis=1)


def _bmm(a, b):
    return jnp.einsum('hlm,hmn->hln', a.astype(BF16), b.astype(BF16), preferred_element_type=F32)


def _bmm_nt(a, b):
    return jnp.einsum('hlk,hmk->hlm', a.astype(BF16), b.astype(BF16), preferred_element_type=F32)


def _bmm_tn(a, b):
    return jnp.einsum('hlv,hlk->hvk', a.astype(BF16), b.astype(BF16), preferred_element_type=F32)


def _rwkv_kernel(x_ref, shift_ref, s0_ref, mu_ref, w0_ref, a0_ref, w2_ref, a2_ref, g2_ref, kk_ref, ka_ref, rk_ref,
                 lng_ref, lnb_ref, o_ref, sfin_ref, prev_scr, s_scr):
    L = x_ref.shape[0]
    c = pl.program_id(1)

    @pl.when(c == 0)
    def _():
        prev_scr[...] = shift_ref[...]
        s_scr[...] = s0_ref[...]

    x = x_ref[...]
    prev = jnp.concatenate([prev_scr[...], x[:L - 1]], axis=0)
    prev_scr[...] = x[L - 1:]
    xs = x + mu_ref[...] * (prev - x)
    r, k, v = xs[:, :RW_DIM], xs[:, RW_DIM:2 * RW_DIM], xs[:, 2 * RW_DIM:3 * RW_DIM]
    wl = xs[:, RW_SPLITS[2]:RW_SPLITS[3]]
    al = xs[:, RW_SPLITS[3]:RW_SPLITS[4]]
    gl = xs[:, RW_SPLITS[4]:]
    w = -jax.nn.softplus(-(w0_ref[...] + jnp.dot(jnp.tanh(wl).astype(BF16), w2_ref[...], preferred_element_type=F32))) - 0.5
    logd = -jnp.exp(w)
    a = jax.nn.sigmoid(a0_ref[...] + jnp.dot(al.astype(BF16), a2_ref[...], preferred_element_type=F32))
    g = jnp.dot(jax.nn.sigmoid(gl).astype(BF16), g2_ref[...], preferred_element_type=F32)
    kk = k * kk_ref[...]
    k = k * (1.0 + (a - 1.0) * ka_ref[...])
    row = lax.broadcasted_iota(jnp.int32, (L, L), 0)
    col = lax.broadcasted_iota(jnp.int32, (L, L), 1)
    cs = jnp.dot((col <= row).astype(F32), logd, precision=HIGHEST, preferred_element_type=F32)
    g_incl, g_prev, g_inv = jnp.exp(cs), jnp.exp(cs - logd), jnp.exp(-cs)

    kk_h = _heads(kk)
    kk_h = kk_h * lax.rsqrt(jnp.sum(kk_h * kk_h, axis=-1, keepdims=True) + 1e-12)
    r_h, k_h, v_h = _heads(r), _heads(k), _heads(v)
    inv_h = _heads(g_inv)
    at = kk_h * _heads(g_prev)
    rt = r_h * _heads(g_incl)
    bt = -(kk_h * _heads(a)) * inv_h
    kt = k_h * inv_h
    ar = jnp.concatenate([at, rt], axis=1)
    bk = jnp.concatenate([bt, kt], axis=1)
    gram = _bmm_nt(ar, bk)
    strict, incl = (col < row)[None], (col <= row)[None]
    a_ab = jnp.where(strict, gram[:, :L, :L], 0.0)
    a_ak = jnp.where(strict, gram[:, :L, L:], 0.0)
    a_rbk = jnp.concatenate([jnp.where(incl, gram[:, L:, :L], 0.0), jnp.where(incl, gram[:, L:, L:], 0.0)], axis=2)
    tinv = jnp.where((col == row)[None], 1.0, a_ab)
    pw = a_ab
    n_sq, span = 0, 2
    while span < L:
        n_sq, span = n_sq + 1, span * 2
    for _ in range(n_sq):
        pw = _bmm(pw, pw)
        tinv = tinv + _bmm(tinv, pw)

    s0 = s_scr[...]
    ars = _bmm_nt(ar, s0)
    u = _bmm(tinv, ars[:, :L] + _bmm(a_ak, v_h))
    uv = jnp.concatenate([u, v_h], axis=1)
    o = ars[:, L:] + _bmm(a_rbk, uv)
    s_new = (s0 + _bmm_tn(uv, bk)) * _heads(g_incl[L - 1:])
    s_scr[...] = s_new

    @pl.when(c == pl.num_programs(1) - 1)
    def _():
        sfin_ref[...] = s_new

    mean = jnp.mean(o, axis=-1, keepdims=True)
    var = jnp.mean(jnp.square(o - mean), axis=-1, keepdims=True)
    on = _unheads((o - mean) * lax.rsqrt(var + RW_GN_EPS)) * lng_ref[...] + lnb_ref[...]
    bonus = _unheads(jnp.sum(r_h * k_h * _heads(rk_ref[...]), axis=-1, keepdims=True) * v_h)
    o_ref[...] = (on + bonus) * g


def _rwkv7_pallas(p_rw, shift_state, wkv_state, p, wb):
    B, T, _ = p_rw.shape
    L = min(RW_CHUNK, T)
    assert T % L == 0
    vec = lambda name: p[name].reshape(1, -1)
    full = lambda shape: pl.BlockSpec(shape, lambda b, c: (0,) * len(shape))
    st = pl.BlockSpec((None, RW_HEADS, RW_HD, RW_HD), lambda b, c: (b, 0, 0, 0))
    o, s_fin = pl.pallas_call(
        _rwkv_kernel,
        grid=(B, T // L),
        in_specs=[pl.BlockSpec((None, L, RW_IN), lambda b, c: (b, c, 0)),
                  pl.BlockSpec((None, 1, RW_IN), lambda b, c: (b, 0, 0)), st,
                  full((1, RW_IN)), full((1, RW_DIM)), full((1, RW_DIM)),
                  full((RW_W_LORA, RW_DIM)), full((RW_A_LORA, RW_DIM)), full((RW_G_LORA, RW_DIM)),
                  full((1, RW_DIM)), full((1, RW_DIM)), full((1, RW_DIM)), full((1, RW_DIM)), full((1, RW_DIM))],
        out_specs=[pl.BlockSpec((None, L, RW_DIM), lambda b, c: (b, c, 0)), st],
        out_shape=[jax.ShapeDtypeStruct((B, T, RW_DIM), F32),
                   jax.ShapeDtypeStruct((B, RW_HEADS, RW_HD, RW_HD), F32)],
        scratch_shapes=[pltpu.VMEM((1, RW_IN), F32), pltpu.VMEM((RW_HEADS, RW_HD, RW_HD), F32)],
        compiler_params=_cparams("arbitrary", "arbitrary"),
        name="rwkv7_mix",
    )(p_rw, shift_state.reshape(B, 1, RW_IN), wkv_state, vec('rw_mu'), vec('rw_w0'), vec('rw_a0'),
      wb['rw_w2'], wb['rw_a2'], wb['rw_g2'], vec('rw_k_k'), vec('rw_k_a'), vec('rw_r_k'), vec('rw_lnx_g'), vec('rw_lnx_b'))
    return o, p_rw[:, -1], s_fin


SSM_GW = SSM_RPG * SSM_HD


def _mamba_kernel(z_ref, xbc_ref, dt_ref, conv0_ref, s0_ref, cw_ref, cb_ref, dtb_ref, alog_ref, dvec_ref, ng_ref,
                  hexp_ref, o_ref, convf_ref, sfin_ref, tail_scr, s_scr):
    L = z_ref.shape[0]
    c = pl.program_id(1)
    last = pl.num_programs(1) - 1
    G, N, K = SSM_GROUPS, SSM_STATE, SSM_CONV

    @pl.when(c == 0)
    def _():
        tail_scr[...] = conv0_ref[...]
        s_scr[...] = s0_ref[...]

    ext = jnp.concatenate([tail_scr[...], xbc_ref[...]], axis=0)
    new_tail = ext[L:L + K - 1]
    tail_scr[...] = new_tail
    conv = cb_ref[...] + sum(ext[i:i + L] * cw_ref[i:i + 1] for i in range(K))
    act = jax.nn.silu(conv)
    xs, bm, cm = act[:, :SSM_DIM], act[:, SSM_DIM:SSM_DIM + G * N], act[:, SSM_DIM + G * N:]

    dt = jax.nn.softplus(dt_ref[...] + dtb_ref[...])
    da = dt * (-jnp.exp(alog_ref[...]))
    row = lax.broadcasted_iota(jnp.int32, (L, L), 0)
    col = lax.broadcasted_iota(jnp.int32, (L, L), 1)
    causal = col <= row
    a_cs = jnp.dot(causal.astype(F32), da, precision=HIGHEST, preferred_element_type=F32)
    a_last = a_cs[L - 1:]
    a_cs_t = a_cs.T
    per_head = jnp.concatenate([dt, dt * jnp.exp(a_last - a_cs), jnp.exp(a_cs)], axis=0)
    wide = jnp.dot(per_head, hexp_ref[...], precision=HIGHEST, preferred_element_type=F32)
    xdt_b = (xs * wide[:L]).astype(BF16)
    xds = (xs * wide[L:2 * L]).astype(BF16)
    dec_in = wide[2 * L:]
    st_head = lax.broadcasted_iota(jnp.int32, (SSM_GW, LANE), 0) // SSM_HD
    st_lane = lax.broadcasted_iota(jnp.int32, (SSM_GW, LANE), 1)
    dec_chunk = jnp.exp(a_last)

    ys = []
    for g in range(G):
        bg = bm[:, g * N:(g + 1) * N].astype(BF16)
        cg = cm[:, g * N:(g + 1) * N].astype(BF16)
        cb = lax.dot_general(cg, bg, NT_DIMS, preferred_element_type=F32)
        s_in = s_scr[g]
        y_off = lax.dot_general(cg, s_in.astype(BF16), NT_DIMS, preferred_element_type=F32)
        cols = slice(g * SSM_GW, (g + 1) * SSM_GW)
        y_diag = []
        for r in range(SSM_RPG):
            h = g * SSM_RPG + r
            seg = a_cs[:, h:h + 1] - a_cs_t[h:h + 1, :]
            w = (cb * jnp.exp(jnp.where(causal, seg, -jnp.inf))).astype(BF16)
            y_diag.append(jnp.dot(w, xdt_b[:, h * SSM_HD:(h + 1) * SSM_HD], preferred_element_type=F32))
        ys.append(jnp.concatenate(y_diag, axis=1) + y_off * dec_in[:, cols])
        states = lax.dot_general(xds[:, cols], bg, (((0,), (0,)), ((), ())), preferred_element_type=F32)
        dec_col = jnp.sum(jnp.where(st_head + g * SSM_RPG == st_lane, dec_chunk, 0.0), axis=-1, keepdims=True)
        s_scr[g] = s_in * dec_col + states
    y = jnp.concatenate(ys, axis=1) + dvec_ref[...] * xs
    y = y * jax.nn.silu(z_ref[...])
    o_ref[...] = y * lax.rsqrt(jnp.mean(y * y, axis=-1, keepdims=True) + EPS) * ng_ref[...]

    @pl.when(c == last)
    def _():
        convf_ref[...] = new_tail
        sfin_ref[...] = s_scr[...]


def _mamba2_pallas(z, xbc, dt, conv_state, ssm_state, p):
    B, T, _ = z.shape
    L = SSM_CHUNK if T % SSM_CHUNK == 0 else T
    pad = lambda v: jnp.pad(v, (0, LANE - SSM_HEADS)).reshape(1, LANE)
    hexp = (jnp.arange(LANE)[:, None] == jnp.arange(SSM_DIM)[None, :] // SSM_HD).astype(F32)
    full = lambda shape: pl.BlockSpec(shape, lambda b, c: (0,) * len(shape))
    tile = lambda wd: pl.BlockSpec((None, L, wd), lambda b, c: (b, c, 0))
    per_b = lambda shape: pl.BlockSpec((None,) + shape, lambda b, c: (b,) + (0,) * len(shape))
    st_shape = (SSM_GROUPS, SSM_GW, SSM_STATE)
    y, conv_f, s_fin = pl.pallas_call(
        _mamba_kernel,
        grid=(B, T // L),
        in_specs=[tile(SSM_DIM), tile(SSM_XBC), tile(LANE), per_b((SSM_CONV - 1, SSM_XBC)), per_b(st_shape),
                  full((SSM_CONV, SSM_XBC)), full((1, SSM_XBC)), full((1, LANE)), full((1, LANE)), full((1, SSM_DIM)),
                  full((1, SSM_DIM)), full((LANE, SSM_DIM))],
        out_specs=[tile(SSM_DIM), per_b((SSM_CONV - 1, SSM_XBC)), per_b(st_shape)],
        out_shape=[jax.ShapeDtypeStruct((B, T, SSM_DIM), F32),
                   jax.ShapeDtypeStruct((B, SSM_CONV - 1, SSM_XBC), F32),
                   jax.ShapeDtypeStruct((B,) + st_shape, F32)],
        scratch_shapes=[pltpu.VMEM((SSM_CONV - 1, SSM_XBC), F32), pltpu.VMEM(st_shape, F32)],
        compiler_params=_cparams("arbitrary", "arbitrary"),
        name="mamba2_mix",
    )(z, xbc, dt, conv_state, ssm_state.reshape((B,) + st_shape), p['ssm_conv_w'], p['ssm_conv_b'].reshape(1, -1),
      pad(p['ssm_dt_bias']), pad(p['ssm_a_log']), jnp.repeat(p['ssm_d'], SSM_HD).reshape(1, -1),
      p['ssm_norm_g'].reshape(1, -1), hexp)
    return y, conv_f, s_fin.reshape(B, SSM_HEADS, SSM_HD, SSM_STATE)


def _prep_weights(p):
    w_in = p['w_in']
    o_ssm = RW_IN
    o_nsa = RW_IN + SSM_IN
    o_gate = o_nsa + NSA_IN
    pad = lambda w, n: jnp.pad(w, ((0, 0), (0, n - w.shape[1])))
    w_rw = w_in[:, :RW_IN]
    w_ssm = jnp.concatenate([w_in[:, o_ssm:o_ssm + SSM_DIM + SSM_XBC],
                             pad(w_in[:, o_ssm + SSM_DIM + SSM_XBC:o_nsa], LANE)], axis=1)
    o_g = o_nsa + NSA_DIM + 6 * NSA_KV_DIM
    wg = w_in[:, o_g:o_gate].reshape(D_MODEL, 3, NSA_KV_HEADS, NSA_RPG).transpose(0, 2, 1, 3)
    wg = jnp.pad(wg.reshape(D_MODEL, NSA_KV_HEADS, 3 * NSA_RPG), ((0, 0), (0, 0), (0, 16 - 3 * NSA_RPG)))
    w_nsa = jnp.concatenate([w_in[:, o_nsa:o_g], pad(wg.reshape(D_MODEL, NSA_KV_HEADS * 16), LANE),
                             w_in[:, o_gate:]], axis=1)
    bf = lambda w: w.astype(BF16)
    return dict(w_ada=bf(p['w_ada']), w_rw=bf(w_rw), w_ssm=bf(w_ssm), w_nsa=bf(w_nsa),
                w_br_rw=bf(p['w_br_rw']), w_br_ssm=bf(p['w_br_ssm']), w_br_nsa=bf(p['w_br_nsa']),
                w_out=bf(p['w_out']), w_ffn_in=bf(p['w_ffn_in']), w_ffn_out=bf(p['w_ffn_out']),
                rw_w2=bf(p['rw_w2']), rw_a2=bf(p['rw_a2']), rw_g2=bf(p['rw_g2']),
                cmp_w1=bf(p['nsa_cmp_w1'].reshape(2, CMP_BLOCK * NSA_HD, CMP_HIDDEN)), cmp_w2=bf(p['nsa_cmp_w2']),
                cmp_pair=_pair_weights(p['nsa_cmp_w1']))


def _trunk_layer(x, c, p, wb, rw_shift, rw_wkv, conv_state, ssm_state, nsa_fn, per_batch):
    B, T, _ = x.shape
    n = B * T
    mod = _ada(c, wb['w_ada'], p['b_ada'])
    if per_batch:
        mods = [m.reshape(B, 1, D_MODEL) for m in jnp.split(mod, 6, axis=-1)]
    else:
        mods = [jnp.repeat(m, T, axis=0) for m in jnp.split(mod, 6, axis=-1)]
    sh1, sc1, gt1, sh2, sc2, gt2 = mods
    x2 = x.reshape(n, D_MODEL)
    (p_rw,) = _norm_proj(x2, sc1, sh1, p['ln1'], wb['w_rw'], (RW_IN,), per_batch, T, "proj_rw")
    z, xbc, dt = _norm_proj(x2, sc1, sh1, p['ln1'], wb['w_ssm'], (SSM_DIM, SSM_XBC, LANE), per_batch, T, "proj_ssm")
    kvw = 2 * NSA_KV_DIM
    q, cmp2, sel2, win2, gates, p_gate = _norm_proj(x2, sc1, sh1, p['ln1'], wb['w_nsa'],
                                                    (NSA_DIM, kvw, kvw, kvw, LANE, N_BRANCH * D_MODEL), per_batch, T,
                                                    "proj_nsa")
    r3 = lambda t: t.reshape(B, T, t.shape[-1])
    o_rw, rw_shift, rw_wkv = _rwkv7_pallas(r3(p_rw), rw_shift, rw_wkv, p, wb)
    o_ssm, conv_state, ssm_state = _mamba2_pallas(r3(z), r3(xbc), r3(dt), conv_state, ssm_state, p)
    o_nsa, (cmp_rows, sel_rows, win_buf) = nsa_fn(q, cmp2, sel2, win2, gates, p, wb, B, T)
    x2 = _merge(x2, gt1, o_rw.reshape(n, -1), o_ssm.reshape(n, -1), o_nsa.reshape(n, -1), p_gate,
                wb['w_br_rw'], wb['w_br_ssm'], wb['w_br_nsa'], wb['w_out'], per_batch, T)
    x2 = _ffn(x2, sc2, sh2, gt2, p['ln2'], wb['w_ffn_in'], wb['w_ffn_out'], per_batch, T)
    return x2.reshape(B, T, D_MODEL), (rw_shift, rw_wkv, conv_state, ssm_state, cmp_rows, sel_rows, win_buf)


def kernel(x_prompt, x_sample, cache_cmp_kv, cache_sel_kv, cache_win_kv, state_rwkv_shift, state_rwkv_wkv,
           state_ssm_conv, state_ssm, page_table, c_prompt, c_sample, w_ada, b_ada, ln1, ln2, w_in,
           rw_mu, rw_w0, rw_w2, rw_a0, rw_a2, rw_g2, rw_k_k, rw_k_a, rw_r_k, rw_lnx_g, rw_lnx_b,
           ssm_conv_w, ssm_conv_b, ssm_dt_bias, ssm_a_log, ssm_d, ssm_norm_g,
           nsa_q_norm, nsa_k_norm, nsa_cmp_pos, nsa_cmp_w1, nsa_cmp_b1, nsa_cmp_w2,
           w_br_rw, w_br_ssm, w_br_nsa, w_out, w_ffn_in, w_ffn_out):
    params = dict(w_ada=w_ada, b_ada=b_ada, ln1=ln1, ln2=ln2, w_in=w_in,
                  rw_mu=rw_mu, rw_w0=rw_w0, rw_w2=rw_w2, rw_a0=rw_a0, rw_a2=rw_a2, rw_g2=rw_g2,
                  rw_k_k=rw_k_k, rw_k_a=rw_k_a, rw_r_k=rw_r_k, rw_lnx_g=rw_lnx_g, rw_lnx_b=rw_lnx_b,
                  ssm_conv_w=ssm_conv_w, ssm_conv_b=ssm_conv_b, ssm_dt_bias=ssm_dt_bias, ssm_a_log=ssm_a_log,
                  ssm_d=ssm_d, ssm_norm_g=ssm_norm_g, nsa_q_norm=nsa_q_norm, nsa_k_norm=nsa_k_norm,
                  nsa_cmp_pos=nsa_cmp_pos, nsa_cmp_w1=nsa_cmp_w1, nsa_cmp_b1=nsa_cmp_b1, nsa_cmp_w2=nsa_cmp_w2,
                  w_br_rw=w_br_rw, w_br_ssm=w_br_ssm, w_br_nsa=w_br_nsa, w_out=w_out,
                  w_ffn_in=w_ffn_in, w_ffn_out=w_ffn_out)
    bp = x_prompt.shape[0]
    depth = w_in.shape[0]
    zero_shift = jnp.zeros((bp, RW_IN), F32)
    zero_wkv = jnp.zeros((bp, RW_HEADS, RW_HD, RW_HD), F32)
    zero_conv = jnp.zeros((bp, SSM_CONV - 1, SSM_XBC), F32)
    zero_ssm = jnp.zeros((bp, SSM_HEADS, SSM_HD, SSM_STATE), F32)
    xp, xs = x_prompt, x_sample
    st_p, st_s = [], []
    for l in range(depth):
        p = {name: arr[l] for name, arr in params.items()}
        wb = _prep_weights(p)
        xp, sp_l = _trunk_layer(xp, c_prompt, p, wb, zero_shift, zero_wkv, zero_conv, zero_ssm, _nsa_prompt_pallas, True)
        nsa_s = functools.partial(_nsa_sample_pallas, layer=l, cache_cmp_kv=cache_cmp_kv, cache_sel_kv=cache_sel_kv,
                                  win_buf=cache_win_kv[l], page_table=page_table)
        xs, ss_l = _trunk_layer(xs, c_sample, p, wb, state_rwkv_shift[l], state_rwkv_wkv[l], state_ssm_conv[l],
                                state_ssm[l], nsa_s, False)
        st_p.append(sp_l)
        st_s.append(ss_l)
    sp = [jnp.stack([s[i] for s in st_p]) for i in range(7)]
    ss = [jnp.stack([s[i] for s in st_s]) for i in range(7)]
    return (xp, xs, sp[4], ss[4], sp[5], ss[5], sp[6], ss[6], sp[0], ss[0], sp[1], ss[1], sp[2], ss[2], sp[3], ss[3])
```

```python
import functools

import jax
import jax.numpy as jnp
from jax import lax
from jax.experimental import pallas as pl
from jax.experimental.pallas import tpu as pltpu

F32 = jnp.float32
BF16 = jnp.bfloat16

D_MODEL = 1024
PAST_LEN = 16384
PAGE_SIZE = 128
RW_HEADS = 16
RW_HD = 64
RW_DIM = RW_HEADS * RW_HD
RW_W_LORA = 64
RW_A_LORA = 64
RW_G_LORA = 128
RW_IN = 3 * RW_DIM + RW_W_LORA + RW_A_LORA + RW_G_LORA
RW_SPLITS = (RW_DIM, 2 * RW_DIM, 3 * RW_DIM, 3 * RW_DIM + RW_W_LORA, 3 * RW_DIM + RW_W_LORA + RW_A_LORA)
RW_GN_EPS = 64e-5
SSM_DIM = 2 * D_MODEL
SSM_HD = 64
SSM_HEADS = SSM_DIM // SSM_HD
SSM_GROUPS = 4
SSM_RPG = SSM_HEADS // SSM_GROUPS
SSM_STATE = 128
SSM_CONV = 4
SSM_CHUNK = 128
SSM_XBC = SSM_DIM + 2 * SSM_GROUPS * SSM_STATE
SSM_IN = SSM_DIM + SSM_XBC + SSM_HEADS
NSA_HEADS = 16
NSA_HD = 64
NSA_KV_HEADS = 4
NSA_RPG = NSA_HEADS // NSA_KV_HEADS
NSA_DIM = NSA_HEADS * NSA_HD
NSA_KV_DIM = NSA_KV_HEADS * NSA_HD
CMP_STRIDE = 16
CMP_BLOCK = 2 * CMP_STRIDE
CMP_HIDDEN = 128
SEL_BLOCK = 64
N_SELECT = 16
WINDOW = 512
Q_BLOCK = 64
SEL_FORCE = 1e6
NSA_IN = NSA_DIM + 6 * NSA_KV_DIM + 3 * NSA_HEADS
N_BRANCH = 3
FFN_HIDDEN = ((8 * D_MODEL // 3 + 255) // 256) * 256
EPS = 1e-6

LANE = 128
VMEM_LIMIT = 56 * 1024 * 1024


def _cparams(*sem):
    return pltpu.CompilerParams(dimension_semantics=sem, vmem_limit_bytes=VMEM_LIMIT)


def _modnorm(x, g, sc, sh):
    y = x * lax.rsqrt(jnp.mean(x * x, axis=-1, keepdims=True) + EPS)
    return (y * g) * (1.0 + sc) + sh


def _mod_spec(per_batch, tm, tiles_per_batch):
    if per_batch:
        return pl.BlockSpec((None, 1, D_MODEL), lambda i: (i // tiles_per_batch, 0, 0))
    return pl.BlockSpec((tm, D_MODEL), lambda i: (i, 0))


def _full_spec(shape):
    return pl.BlockSpec(shape, lambda i: (0,) * len(shape))


def _ada_kernel(c_ref, w_ref, b_ref, o_ref):
    o_ref[...] = jnp.dot(c_ref[...].astype(BF16), w_ref[...], preferred_element_type=F32) + b_ref[...]


def _ada(c, w_bf, b):
    n, tn = c.shape[0], 1536
    return pl.pallas_call(
        _ada_kernel,
        grid=(6 * D_MODEL // tn,),
        in_specs=[pl.BlockSpec((n, D_MODEL), lambda j: (0, 0)),
                  pl.BlockSpec((D_MODEL, tn), lambda j: (0, j)),
                  pl.BlockSpec((1, tn), lambda j: (0, j))],
        out_specs=pl.BlockSpec((n, tn), lambda j: (0, j)),
        out_shape=jax.ShapeDtypeStruct((n, 6 * D_MODEL), F32),
        compiler_params=_cparams("arbitrary"),
        name="ada_mod",
    )(c, w_bf, b.reshape(1, -1))


def _norm_proj_kernel(x_ref, sc_ref, sh_ref, g_ref, w_ref, *o_refs, splits):
    h = _modnorm(x_ref[...], g_ref[...], sc_ref[...], sh_ref[...]).astype(BF16)
    for o_ref, (a, b) in zip(o_refs, splits):
        o_ref[...] = jnp.dot(h, w_ref[:, a:b], preferred_element_type=F32)


def _norm_proj(x2, sc, sh, g, w_bf, widths, per_batch, tokens_per_batch, name):
    n = x2.shape[0]
    tm = min(512, n)
    splits, a = [], 0
    for wd in widths:
        splits.append((a, a + wd))
        a += wd
    assert a == w_bf.shape[1] and n % tm == 0
    tpb = max(tokens_per_batch // tm, 1)
    return pl.pallas_call(
        functools.partial(_norm_proj_kernel, splits=tuple(splits)),
        grid=(n // tm,),
        in_specs=[pl.BlockSpec((tm, D_MODEL), lambda i: (i, 0)),
                  _mod_spec(per_batch, tm, tpb), _mod_spec(per_batch, tm, tpb),
                  _full_spec((1, D_MODEL)), _full_spec(w_bf.shape)],
        out_specs=[pl.BlockSpec((tm, wd), lambda i: (i, 0)) for wd in widths],
        out_shape=[jax.ShapeDtypeStruct((n, wd), F32) for wd in widths],
        compiler_params=_cparams("arbitrary"),
        name=name,
    )(x2, sc, sh, g.reshape(1, -1), w_bf)


def _merge_kernel(x_ref, gt_ref, orw_ref, ossm_ref, onsa_ref, gate_ref, wrw_ref, wssm_ref, wnsa_ref, wout_ref,
                  o_ref):
    gate = jax.nn.sigmoid(gate_ref[...])
    br = (jnp.dot(orw_ref[...].astype(BF16), wrw_ref[...], preferred_element_type=F32),
          jnp.dot(ossm_ref[...].astype(BF16), wssm_ref[...], preferred_element_type=F32),
          jnp.dot(onsa_ref[...].astype(BF16), wnsa_ref[...], preferred_element_type=F32))
    merged = sum(gate[:, i * D_MODEL:(i + 1) * D_MODEL] * br[i] for i in range(N_BRANCH))
    y = jnp.dot(merged.astype(BF16), wout_ref[...], preferred_element_type=F32)
    o_ref[...] = x_ref[...] + gt_ref[...] * y


def _merge(x2, gt, o_rw, o_ssm, o_nsa, gate, w_rw, w_ssm, w_nsa, w_out, per_batch, tokens_per_batch):
    n = x2.shape[0]
    tm = min(512, n)
    tpb = max(tokens_per_batch // tm, 1)
    row = lambda wd: pl.BlockSpec((tm, wd), lambda i: (i, 0))
    return pl.pallas_call(
        _merge_kernel,
        grid=(n // tm,),
        in_specs=[row(D_MODEL), _mod_spec(per_batch, tm, tpb), row(RW_DIM), row(SSM_DIM), row(NSA_DIM),
                  row(N_BRANCH * D_MODEL), _full_spec(w_rw.shape), _full_spec(w_ssm.shape),
                  _full_spec(w_nsa.shape), _full_spec(w_out.shape)],
        out_specs=row(D_MODEL),
        out_shape=jax.ShapeDtypeStruct((n, D_MODEL), F32),
        compiler_params=_cparams("arbitrary"),
        name="branch_merge",
    )(x2, gt, o_rw, o_ssm, o_nsa, gate, w_rw, w_ssm, w_nsa, w_out)


FFN_CHUNK = 256


def _ffn_kernel(x_ref, sc_ref, sh_ref, gt_ref, g_ref, win_ref, wout_ref, o_ref):
    x = x_ref[...]
    h = _modnorm(x, g_ref[...], sc_ref[...], sh_ref[...]).astype(BF16)
    acc = jnp.zeros(x.shape, F32)
    for c in range(FFN_HIDDEN // FFN_CHUNK):
        a = c * FFN_CHUNK
        up = jnp.dot(h, win_ref[:, a:a + FFN_CHUNK], preferred_element_type=F32)
        gf = jnp.dot(h, win_ref[:, FFN_HIDDEN + a:FFN_HIDDEN + a + FFN_CHUNK], preferred_element_type=F32)
        act = (jax.nn.silu(gf) * up).astype(BF16)
        acc = acc + jnp.dot(act, wout_ref[a:a + FFN_CHUNK, :], preferred_element_type=F32)
    o_ref[...] = x + gt_ref[...] * acc


def _ffn(x2, sc, sh, gt, g, w_in, w_out, per_batch, tokens_per_batch):
    n = x2.shape[0]
    tm = min(512, n)
    tpb = max(tokens_per_batch // tm, 1)
    row = pl.BlockSpec((tm, D_MODEL), lambda i: (i, 0))
    mod = _mod_spec(per_batch, tm, tpb)
    return pl.pallas_call(
        _ffn_kernel,
        grid=(n // tm,),
        in_specs=[row, mod, mod, mod, _full_spec((1, D_MODEL)), _full_spec(w_in.shape), _full_spec(w_out.shape)],
        out_specs=row,
        out_shape=jax.ShapeDtypeStruct((n, D_MODEL), F32),
        compiler_params=_cparams("arbitrary"),
        name="ffn",
    )(x2, sc, sh, gt, g.reshape(1, -1), w_in, w_out)


HIGHEST = lax.Precision.HIGHEST
NT_DIMS = (((1,), (1,)), ((), ()))
NEG_BIG = -1e30
F32_TINY = float(jnp.finfo(jnp.float32).tiny)


def _split3(x):
    hi = x.astype(BF16)
    r = x - hi.astype(F32)
    mid = r.astype(BF16)
    return hi, mid, (r - mid.astype(F32)).astype(BF16)


def _dot_f32_nt(a3, b3):
    pairs = ((2, 0), (0, 2), (1, 1), (1, 0), (0, 1), (0, 0))
    return sum(lax.dot_general(a3[i], b3[j], NT_DIMS, preferred_element_type=F32) for i, j in pairs)


def _dot_f32_exact_rhs(a, b_bf):
    return sum(jnp.dot(t, b_bf, preferred_element_type=F32) for t in reversed(_split3(a)))


def _seg_rmsnorm(x, seg, seg_t, gain):
    ss = jnp.dot(x * x, seg, precision=HIGHEST, preferred_element_type=F32)
    inv = lax.rsqrt(ss * (1.0 / NSA_HD) + EPS)
    return x * jnp.dot(inv, seg_t, precision=HIGHEST, preferred_element_type=F32) * gain


def _nsa_prep_kernel(q_ref, sel_ref, win_ref, seg_ref, segt_ref, qg_ref, kg_ref,
                     qh_ref, ksh_ref, vsh_ref, kwh_ref, vwh_ref, selo_ref, wino_ref):
    seg, segt = seg_ref[...], segt_ref[...]
    qn = _seg_rmsnorm(q_ref[...], seg, segt, qg_ref[...]) * (NSA_HD ** -0.5)
    for h in range(NSA_HEADS):
        qh_ref[h] = qn[:, h * NSA_HD:(h + 1) * NSA_HD]
    kg = kg_ref[...]
    for src, dst, kh_ref, vh_ref, row in ((sel_ref, selo_ref, ksh_ref, vsh_ref, 1), (win_ref, wino_ref, kwh_ref, vwh_ref, 2)):
        x = src[...]
        kn = _seg_rmsnorm(x[:, :NSA_KV_DIM], seg[:NSA_KV_DIM], segt[:, :NSA_KV_DIM], kg[row:row + 1])
        v = x[:, NSA_KV_DIM:]
        dst[:, :NSA_KV_DIM] = kn
        dst[:, NSA_KV_DIM:] = v
        ones_col = (lax.broadcasted_iota(jnp.int32, (x.shape[0], NSA_HD), 1) == 0).astype(BF16)
        for g in range(NSA_KV_HEADS):
            kh_ref[g] = kn[:, g * NSA_HD:(g + 1) * NSA_HD].astype(BF16)
            vh_ref[g] = jnp.concatenate([v[:, g * NSA_HD:(g + 1) * NSA_HD].astype(BF16), ones_col], axis=1)


def _seg_mats():
    c = jnp.arange(NSA_DIM)[:, None] // NSA_HD
    seg = (c == jnp.arange(LANE)[None, :]).astype(F32)
    return seg, seg.T


def _nsa_prep(q2, sel2, win2, q_norm, k_norm, B, T):
    tm = min(512, T)
    tpb = T // tm
    seg, segt = _seg_mats()
    row = lambda wd: pl.BlockSpec((tm, wd), lambda i: (i, 0))
    hm = lambda nh, wd=NSA_HD: pl.BlockSpec((None, nh, tm, wd), lambda i: (i // tpb, 0, i % tpb, 0))
    ks = jax.ShapeDtypeStruct((B, NSA_KV_HEADS, T, NSA_HD), BF16)
    vs = jax.ShapeDtypeStruct((B, NSA_KV_HEADS, T, 2 * NSA_HD), BF16)
    rows = jax.ShapeDtypeStruct((B * T, 2 * NSA_KV_DIM), F32)
    return pl.pallas_call(
        _nsa_prep_kernel,
        grid=(B * T // tm,),
        in_specs=[row(NSA_DIM), row(2 * NSA_KV_DIM), row(2 * NSA_KV_DIM), _full_spec(seg.shape), _full_spec(segt.shape),
                  _full_spec((1, NSA_DIM)), _full_spec((3, NSA_KV_DIM))],
        out_specs=[hm(NSA_HEADS), hm(NSA_KV_HEADS), hm(NSA_KV_HEADS, 2 * NSA_HD), hm(NSA_KV_HEADS),
                   hm(NSA_KV_HEADS, 2 * NSA_HD), row(2 * NSA_KV_DIM), row(2 * NSA_KV_DIM)],
        out_shape=[jax.ShapeDtypeStruct((B, NSA_HEADS, T, NSA_HD), F32), ks, vs, ks, vs, rows, rows],
        compiler_params=_cparams("arbitrary"),
        name="nsa_prep",
    )(q2, sel2, win2, seg, segt, jnp.tile(q_norm, NSA_HEADS).reshape(1, -1), jnp.tile(k_norm, (1, NSA_KV_HEADS)))


def _nsa_cmp_kernel(xk_ref, xv_ref, w1_ref, pos_ref, b1_ref, w2_ref, kn_ref, kc_ref, vc_ref, *, n_valid):
    half = CMP_STRIDE * NSA_HD
    for c, x_ref in ((0, xk_ref), (1, xv_ref)):
        x = x_ref[...]
        const = jnp.dot(pos_ref[c].astype(BF16), w1_ref[c], preferred_element_type=F32)[0:1] + b1_ref[c:c + 1]
        top = jnp.dot(x, w1_ref[c, :half, :], preferred_element_type=F32)
        bot = jnp.dot(x, w1_ref[c, half:, :], preferred_element_type=F32)
        bot_next = jnp.concatenate([bot[1:], jnp.zeros((1, CMP_HIDDEN), F32)], axis=0)
        hid = jax.nn.gelu(top + bot_next + const)
        kv = jnp.dot(hid.astype(BF16), w2_ref[c], preferred_element_type=F32)
        if c == 0:
            kv = kv * lax.rsqrt(jnp.mean(kv * kv, axis=-1, keepdims=True) + EPS) * kn_ref[...]
            for i, term in enumerate(_split3(kv)):
                kc_ref[i] = term
        else:
            vc_ref[...] = kv.astype(BF16)


def _nsa_compress(xh, w1_bf, pos, b1, w2_bf, k_norm0):
    B, _, nh, _ = xh.shape
    blk = lambda off: pl.BlockSpec((None, None, nh, CMP_STRIDE * NSA_HD), lambda b, g: (b, off + g, 0, 0))
    full = lambda shape: pl.BlockSpec(shape, lambda b, g: (0,) * len(shape))
    out = pl.BlockSpec((None, None, nh, NSA_HD), lambda b, g: (b, g, 0, 0))
    return pl.pallas_call(
        functools.partial(_nsa_cmp_kernel, n_valid=nh - 1),
        grid=(B, NSA_KV_HEADS),
        in_specs=[blk(0), blk(NSA_KV_HEADS), full(w1_bf.shape), full((2, 8, CMP_BLOCK * NSA_HD)), full((2, CMP_HIDDEN)),
                  full(w2_bf.shape), full((1, NSA_HD))],
        out_specs=[pl.BlockSpec((None, None, 3, nh, NSA_HD), lambda b, g: (b, g, 0, 0, 0)), out],
        out_shape=[jax.ShapeDtypeStruct((B, NSA_KV_HEADS, 3, nh, NSA_HD), BF16),
                   jax.ShapeDtypeStruct((B, NSA_KV_HEADS, nh, NSA_HD), BF16)],
        compiler_params=_cparams("arbitrary", "arbitrary"),
        name="nsa_compress",
    )(xh, xh, w1_bf, jnp.broadcast_to(pos.reshape(2, 1, -1), (2, 8, CMP_BLOCK * NSA_HD)), b1, w2_bf, k_norm0.reshape(1, -1))


NSA_TQ = 512
NSA_TK = 512


def _softmax_direct(s, mask):
    sm = jnp.where(mask[None], s, NEG_BIG)
    m = jnp.max(sm, axis=-1, keepdims=True)
    e = jnp.where(mask[None], jnp.exp(sm - m), 0.0)
    return e / jnp.maximum(jnp.sum(e, axis=-1, keepdims=True), F32_TINY)


def _top_blocks(score, n_top):
    ns = score.shape[-1]
    lane = lax.broadcasted_iota(jnp.int32, score.shape, 1).astype(F32)
    sel = jnp.zeros(score.shape, F32)
    for _ in range(n_top):
        m = jnp.max(score, axis=-1, keepdims=True)
        first = jnp.min(jnp.where(score == m, lane, float(ns)), axis=-1, keepdims=True)
        hit = lane == first
        sel = jnp.where(hit, 1.0, sel)
        score = jnp.where(hit, -jnp.inf, score)
    return sel


def _nsa_prompt_kernel(q_ref, kc_ref, vc_ref, c2s_ref, ks_ref, vs_ref, kw_ref, vw_ref, g_ref, o_ref, bias_ref):
    R, TQ, TK = NSA_RPG, NSA_TQ, NSA_TK
    T = ks_ref.shape[0]
    nc = kc_ref.shape[1]
    i = pl.program_id(2)
    q0 = i * TQ
    D = NSA_HD
    q = q_ref[...].reshape(R * TQ, D)
    qb = q.astype(BF16)
    t_row = q0 + lax.broadcasted_iota(jnp.int32, (TQ, 1), 0)

    wk = WINDOW + TQ
    ws = pl.multiple_of(jnp.maximum(q0 - WINDOW, 0), TQ)
    kw = kw_ref[pl.ds(ws, wk), :]
    vw = vw_ref[pl.ds(ws, wk), :]
    kpos = ws + lax.broadcasted_iota(jnp.int32, (1, wk), 1)
    sw = lax.dot_general(qb, kw, NT_DIMS, preferred_element_type=F32).reshape(R, TQ, wk)
    sw = sw + jnp.where((kpos <= t_row) & (kpos > t_row - WINDOW), 0.0, NEG_BIG)[None]
    ew = jnp.exp(sw - jnp.max(sw, axis=-1, keepdims=True))
    ow = jnp.dot(ew.reshape(R * TQ, wk).astype(BF16), vw, preferred_element_type=F32).reshape(R, TQ, 2 * D)
    o_win = ow[..., :D] / ow[..., D:D + 1]

    s = _dot_f32_nt(_split3(q), (kc_ref[0], kc_ref[1], kc_ref[2])).reshape(R, TQ, nc)
    cmp_end = lax.broadcasted_iota(jnp.int32, (1, nc), 1) * CMP_STRIDE + (CMP_BLOCK - 1)
    p = _softmax_direct(s, cmp_end <= t_row)
    o_cmp = jnp.dot(p.reshape(R * TQ, nc).astype(BF16), vc_ref[...], preferred_element_type=F32).reshape(R, TQ, D)
    imp = _dot_f32_exact_rhs(jnp.sum(p, axis=0), c2s_ref[...])
    gs = jax.nn.sigmoid(g_ref[...])
    o_cw = [gs[:, r:r + 1] * o_cmp[r] + gs[:, 2 * R + r:2 * R + r + 1] * o_win[r] for r in range(R)]
    ns = imp.shape[-1]
    blk = lax.broadcasted_iota(jnp.int32, (1, ns), 1)
    cur = t_row // SEL_BLOCK
    forced = (blk == 0) | (blk == cur) | (blk == cur - 1)
    score = jnp.where(forced, SEL_FORCE, jnp.where(blk * SEL_BLOCK <= t_row, imp, -SEL_FORCE))
    sel = _top_blocks(score, min(N_SELECT, ns)).astype(BF16)
    per_tile = TK // SEL_BLOCK
    key_blk = lax.broadcasted_iota(jnp.int32, (ns, TK), 1) // SEL_BLOCK
    row_blk = lax.broadcasted_iota(jnp.int32, (ns, TK), 0)
    for j in range(T // TK):
        @pl.when(j * TK < q0 + TQ)
        def _(j=j):
            expand = (key_blk + j * per_tile == row_blk).astype(BF16)
            member = jnp.dot(sel, expand, preferred_element_type=F32) > 0.5
            kpos_j = j * TK + lax.broadcasted_iota(jnp.int32, (1, TK), 1)
            bias_ref[j] = jnp.where(member & (kpos_j <= t_row), 0.0, NEG_BIG)

    def body(j, carry):
        m, acc = carry
        start = pl.multiple_of(j * TK, TK)
        k = ks_ref[pl.ds(start, TK), :]
        v = vs_ref[pl.ds(start, TK), :]
        sm = lax.dot_general(qb, k, NT_DIMS, preferred_element_type=F32).reshape(R, TQ, TK) + bias_ref[j][None]
        m_new = jnp.maximum(m, jnp.max(sm, axis=-1, keepdims=True))
        pj = jnp.exp(sm - m_new).astype(BF16)
        pv = jnp.dot(pj.reshape(R * TQ, TK), v, preferred_element_type=F32).reshape(R, TQ, 2 * D)
        return m_new, jnp.exp(m - m_new) * acc + pv

    carry0 = (jnp.full((R, TQ, 1), NEG_BIG, F32), jnp.zeros((R, TQ, 2 * D), F32))
    _, acc = lax.fori_loop(0, (q0 + TQ + TK - 1) // TK, body, carry0)
    o_sel = acc[..., :D] / acc[..., D:D + 1]
    for r in range(R):
        o_ref[:, r * D:(r + 1) * D] = o_cw[r] + gs[:, R + r:R + r + 1] * o_sel[r]


def _cmp_to_sel_padded(nh, ns):
    i = jnp.arange(nh)[:, None] * CMP_STRIDE
    j = jnp.arange(ns)[None, :] * SEL_BLOCK
    ov = jnp.minimum(i + CMP_BLOCK, j + SEL_BLOCK) - jnp.maximum(i, j)
    return (jnp.clip(ov, 0, None).astype(F32) / CMP_BLOCK).astype(BF16)


def _nsa_prompt_attn(qh, kc, vc, ksh, vsh, kwh, vwh, gates4):
    B, _, T, _ = qh.shape
    nh = kc.shape[3]
    ns = T // SEL_BLOCK
    c2s = _cmp_to_sel_padded(nh, ns)
    kv = lambda n, wd=NSA_HD: pl.BlockSpec((None, None, n, wd), lambda b, g, i: (b, g, 0, 0))
    return pl.pallas_call(
        _nsa_prompt_kernel,
        grid=(B, NSA_KV_HEADS, T // NSA_TQ),
        in_specs=[pl.BlockSpec((None, NSA_RPG, NSA_TQ, NSA_HD), lambda b, g, i: (b, g, i, 0)),
                  pl.BlockSpec((None, None, 3, nh, NSA_HD), lambda b, g, i: (b, g, 0, 0, 0)), kv(nh),
                  pl.BlockSpec(c2s.shape, lambda b, g, i: (0, 0)),
                  kv(T), kv(T, 2 * NSA_HD), kv(T), kv(T, 2 * NSA_HD),
                  pl.BlockSpec((None, None, NSA_TQ, 16), lambda b, g, i: (b, g, i, 0))],
        out_specs=pl.BlockSpec((None, NSA_TQ, NSA_RPG * NSA_HD), lambda b, g, i: (b, i, g)),
        out_shape=jax.ShapeDtypeStruct((B, T, NSA_DIM), F32),
        scratch_shapes=[pltpu.VMEM((T // NSA_TK, NSA_TQ, NSA_TK), F32)],
        compiler_params=_cparams("arbitrary", "arbitrary", "arbitrary"),
        name="nsa_prompt_attn",
    )(qh, kc, vc, c2s, ksh, vsh, kwh, vwh, gates4)


def _nsa_prompt_pallas(q2, cmp2, sel2, win2, gates2, p, wb, B, T):
    qh, ksh, vsh, kwh, vwh, sel_rows, win_rows = _nsa_prep(q2, sel2, win2, p['nsa_q_norm'], p['nsa_k_norm'], B, T)
    nh = T // CMP_STRIDE
    xh = cmp2.astype(BF16).reshape(B, nh, CMP_STRIDE, 2 * NSA_KV_HEADS, NSA_HD)
    xh = xh.transpose(0, 3, 1, 2, 4).reshape(B, 2 * NSA_KV_HEADS, nh, CMP_STRIDE * NSA_HD)
    kc, vc = _nsa_compress(xh, wb['cmp_w1'], p['nsa_cmp_pos'], p['nsa_cmp_b1'], wb['cmp_w2'], p['nsa_k_norm'][0])
    gates4 = gates2[:, :NSA_KV_HEADS * 16].reshape(B, T, NSA_KV_HEADS, 16).transpose(0, 2, 1, 3)
    o = _nsa_prompt_attn(qh, kc, vc, ksh, vsh, kwh, vwh, gates4)
    shp = (B, T, 2, NSA_KV_HEADS, NSA_HD)
    return o, (cmp2.reshape(shp), sel_rows.reshape(shp), win_rows.reshape(shp)[:, -WINDOW:])


HALF_COLS = CMP_STRIDE * 2 * NSA_KV_DIM
PAGE_HALVES = PAGE_SIZE // CMP_STRIDE
N_PAIR = 2 * NSA_KV_HEADS // 2
CMP_PAGES = 32
SEL_PAGES = 16


def _half_proj_body(x, w_ref, o_ref):
    for pr in range(N_PAIR):
        lhs = jnp.concatenate([x[:, s * 2 * NSA_KV_DIM + pr * LANE:s * 2 * NSA_KV_DIM + (pr + 1) * LANE]
                               for s in range(CMP_STRIDE)], axis=1)
        o_ref[:, pr * 4 * CMP_HIDDEN:(pr + 1) * 4 * CMP_HIDDEN] = jnp.dot(lhs, w_ref[pr], preferred_element_type=F32)


def _half_proj_paged_kernel(pt_ref, *refs):
    pages, (w_ref, o_ref) = refs[:CMP_PAGES], refs[CMP_PAGES:]
    kw = 2 * NSA_KV_DIM
    acc = [None] * N_PAIR
    for s in range(CMP_STRIDE):
        for pr in range(N_PAIR):
            x = jnp.concatenate([pg[pr, pl.ds(s, PAGE_HALVES, stride=CMP_STRIDE), :] for pg in pages], axis=0)
            part = jnp.dot(x.astype(BF16), w_ref[pr, s * LANE:(s + 1) * LANE, :], preferred_element_type=F32)
            acc[pr] = part if acc[pr] is None else acc[pr] + part
    for pr in range(N_PAIR):
        o_ref[:, pr * kw:(pr + 1) * kw] = acc[pr]


def _half_proj_rows_kernel(x_ref, w_ref, o_ref):
    _half_proj_body(x_ref[...].astype(BF16), w_ref, o_ref)


def _pair_weights(w1):
    top, bot = w1[:, :CMP_STRIDE], w1[:, CMP_STRIDE:]
    tb = jnp.concatenate([top, bot], axis=-1)
    z = jnp.zeros_like(tb)
    par0 = jnp.concatenate([tb, z], axis=-1)
    par1 = jnp.concatenate([z, tb], axis=-1)
    w = jnp.stack([par0, par1], axis=2)
    w = w.reshape(2, CMP_STRIDE * 2 * NSA_HD, 4 * CMP_HIDDEN)
    return jnp.stack([w[0], w[0], w[1], w[1]], axis=0).astype(BF16)


def _half_proj_paged(pool4, layer, page_table, w_pair):
    B, n_pages = page_table.shape
    m = CMP_PAGES * PAGE_HALVES
    page_spec = lambda n: pl.BlockSpec((None, None, N_PAIR, PAGE_SIZE, LANE),
                                       lambda b, j, pt: (layer, pt[b, j * CMP_PAGES + n], 0, 0, 0))
    grid_spec = pltpu.PrefetchScalarGridSpec(
        num_scalar_prefetch=1, grid=(B, n_pages // CMP_PAGES),
        in_specs=[page_spec(n) for n in range(CMP_PAGES)] + [pl.BlockSpec(w_pair.shape, lambda b, j, pt: (0, 0, 0))],
        out_specs=pl.BlockSpec((None, m, 8 * 2 * CMP_HIDDEN), lambda b, j, pt: (b, j, 0)))
    return pl.pallas_call(
        _half_proj_paged_kernel, grid_spec=grid_spec,
        out_shape=jax.ShapeDtypeStruct((B, n_pages * PAGE_HALVES, 8 * 2 * CMP_HIDDEN), F32),
        compiler_params=_cparams("arbitrary", "arbitrary"),
        name="nsa_half_proj_paged",
    )(page_table, *([pool4] * CMP_PAGES), w_pair)


def _half_proj_rows(x, w_pair):
    m = x.shape[0]
    return pl.pallas_call(
        _half_proj_rows_kernel, grid=(1,),
        in_specs=[_full_spec(x.shape), _full_spec(w_pair.shape)],
        out_specs=_full_spec((m, 8 * 2 * CMP_HIDDEN)),
        out_shape=jax.ShapeDtypeStruct((m, 8 * 2 * CMP_HIDDEN), F32),
        compiler_params=_cparams("arbitrary"),
        name="nsa_half_proj_new",
    )(x, w_pair)


def _cmp_finish_kernel(tk_ref, tv_ref, nk_ref, nv_ref, w1_ref, pos_ref, b1_ref, w2_ref, kn_ref, kc_ref, vc_ref):
    for c, t_ref, n_ref in ((0, tk_ref, nk_ref), (1, tv_ref, nv_ref)):
        const = jnp.dot(pos_ref[c].astype(BF16), w1_ref[c], preferred_element_type=F32)[0:1] + b1_ref[c:c + 1]
        tb = t_ref[...]
        top = tb[:, :CMP_HIDDEN]
        bot_next = jnp.concatenate([tb[1:, CMP_HIDDEN:], n_ref[0:1, CMP_HIDDEN:]], axis=0)
        hid = jax.nn.gelu(top + bot_next + const)
        kv = jnp.dot(hid.astype(BF16), w2_ref[c], preferred_element_type=F32)
        if c == 0:
            kn = kv * lax.rsqrt(jnp.mean(kv * kv, axis=-1, keepdims=True) + EPS) * kn_ref[...]
            for i, term in enumerate(_split3(kn)):
                kc_ref[i] = term
        else:
            vc_ref[...] = kv.astype(BF16)


def _cmp_finish(tb_past, tb_new, w1_bf, pos, b1, w2_bf, k_norm0):
    B, nh, _ = tb_past.shape
    wd = 2 * CMP_HIDDEN
    full = lambda shape: pl.BlockSpec(shape, lambda b, g: (0,) * len(shape))
    past = lambda off: pl.BlockSpec((None, nh, wd), lambda b, g: (b, 0, off + g))
    new = lambda off: pl.BlockSpec((None, 8, wd), lambda b, g: (b, 0, off + g))
    out = pl.BlockSpec((None, None, nh, NSA_HD), lambda b, g: (b, g, 0, 0))
    return pl.pallas_call(
        _cmp_finish_kernel, grid=(B, NSA_KV_HEADS),
        in_specs=[past(0), past(NSA_KV_HEADS), new(0), new(NSA_KV_HEADS), full(w1_bf.shape),
                  full((2, 8, CMP_BLOCK * NSA_HD)), full((2, CMP_HIDDEN)), full(w2_bf.shape), full((1, NSA_HD))],
        out_specs=[pl.BlockSpec((None, None, 3, nh, NSA_HD), lambda b, g: (b, g, 0, 0, 0)), out],
        out_shape=[jax.ShapeDtypeStruct((B, NSA_KV_HEADS, 3, nh, NSA_HD), BF16),
                   jax.ShapeDtypeStruct((B, NSA_KV_HEADS, nh, NSA_HD), BF16)],
        compiler_params=_cparams("arbitrary", "arbitrary"),
        name="nsa_cmp_finish",
    )(tb_past, tb_past, tb_new, tb_new, w1_bf, jnp.broadcast_to(pos.reshape(2, 1, -1), (2, 8, CMP_BLOCK * NSA_HD)),
      b1, w2_bf, k_norm0.reshape(1, -1))


def _nsa_sample_kernel(pt_ref, *refs, T, ns_pad):
    pages = refs[:SEL_PAGES]
    (q_ref, kc_ref, vc_ref, c2s_ref, newsel_ref, wincache_ref, newwin_ref, gate_ref, o_ref,
     sel_scr, m_scr, l_scr, acc_scr, ocmp_scr, owin_scr) = refs[SEL_PAGES:]
    H, G, R, D = NSA_HEADS, NSA_KV_HEADS, NSA_RPG, NSA_HD
    rows, gr = H * T, R * T
    j = pl.program_id(1)
    n_steps = pl.num_programs(1)
    tk = SEL_PAGES * PAGE_SIZE
    row_t = PAST_LEN + lax.broadcasted_iota(jnp.int32, (gr, 1), 0) % T
    rs = lambda g: slice(g * gr, (g + 1) * gr)
    kT = lambda x, g: x[g * D:(g + 1) * D]
    vT = lambda x, g: x[G * D + g * D:G * D + (g + 1) * D]

    @pl.when(j == 0)
    def _():
        q = q_ref[...].reshape(rows, D)
        qb = q.astype(BF16)
        nc = kc_ref.shape[2]
        t_q = PAST_LEN + lax.broadcasted_iota(jnp.int32, (T, 1), 0)
        cmp_end = lax.broadcasted_iota(jnp.int32, (1, nc), 1) * CMP_STRIDE + (CMP_BLOCK - 1)
        blk = lax.broadcasted_iota(jnp.int32, (1, ns_pad), 1)
        cur = t_q // SEL_BLOCK
        forced = (blk == 0) | (blk == cur) | (blk == cur - 1)
        valid = blk * SEL_BLOCK <= t_q
        wc = wincache_ref[...].astype(BF16)
        nw = newwin_ref[...].astype(BF16)
        kpos = jnp.concatenate([PAST_LEN - WINDOW + lax.broadcasted_iota(jnp.int32, (1, WINDOW), 1),
                                PAST_LEN + lax.broadcasted_iota(jnp.int32, (1, T), 1)], axis=1)
        win_mask = (kpos <= row_t) & (kpos > row_t - WINDOW)
        s_cmp = [_dot_f32_nt(_split3(q[rs(g)]), (kc_ref[g, 0], kc_ref[g, 1], kc_ref[g, 2])).reshape(R, T, nc)
                 for g in range(G)]
        s_win = [jnp.concatenate([jnp.dot(qb[rs(g)], kT(wc, g), preferred_element_type=F32),
                                  lax.dot_general(qb[rs(g)], nw[:, g * D:(g + 1) * D], NT_DIMS,
                                                  preferred_element_type=F32)], axis=1) for g in range(G)]
        p_cmp = [_softmax_direct(s_cmp[g], cmp_end <= t_q) for g in range(G)]
        p_win = [_softmax_direct(s_win[g][None], win_mask)[0].astype(BF16) for g in range(G)]
        scores = []
        for g in range(G):
            ocmp_scr[rs(g), :] = jnp.dot(p_cmp[g].reshape(gr, nc).astype(BF16), vc_ref[g], preferred_element_type=F32)
            imp = _dot_f32_exact_rhs(jnp.sum(p_cmp[g], axis=0), c2s_ref[...])
            scores.append(jnp.where(forced, SEL_FORCE, jnp.where(valid, imp, -SEL_FORCE)))
            owin_scr[rs(g), :] = (lax.dot_general(p_win[g][:, :WINDOW], vT(wc, g), NT_DIMS, preferred_element_type=F32)
                                  + jnp.dot(p_win[g][:, WINDOW:], nw[:, G * D + g * D:G * D + (g + 1) * D],
                                            preferred_element_type=F32))
        sel = _top_blocks(jnp.concatenate(scores, axis=0), N_SELECT).reshape(G, 1, T, ns_pad)
        sel_scr[...] = jnp.broadcast_to(sel, (G, R, T, ns_pad)).reshape(rows, ns_pad).T
        m_scr[...] = jnp.full(m_scr.shape, NEG_BIG, F32)
        l_scr[...] = jnp.zeros(l_scr.shape, F32)
        acc_scr[...] = jnp.zeros(acc_scr.shape, F32)

    def flash(scores, v_dots, masks):
        stats = []
        for g in range(G):
            sm = jnp.where(masks[g], scores[g], NEG_BIG)
            m_old = m_scr[rs(g), :]
            m_new = jnp.maximum(m_old, jnp.max(sm, axis=-1, keepdims=True))
            p = jnp.where(masks[g], jnp.exp(sm - m_new), 0.0)
            stats.append((m_new, jnp.exp(m_old - m_new), p))
        pvs = [v_dots[g](stats[g][2].astype(BF16)) for g in range(G)]
        for g in range(G):
            m_new, alpha, p = stats[g]
            l_scr[rs(g), :] = alpha * l_scr[rs(g), :] + jnp.sum(p, axis=-1, keepdims=True)
            acc_scr[rs(g), :] = alpha * acc_scr[rs(g), :] + pvs[g]
            m_scr[rs(g), :] = m_new

    qb = q_ref[...].reshape(rows, D).astype(BF16)
    kv = jnp.concatenate([pg[...].astype(BF16) for pg in pages], axis=1)
    per_step = tk // SEL_BLOCK
    tn_dims = (((0,), (0,)), ((), ()))
    expand = (lax.broadcasted_iota(jnp.int32, (per_step, tk), 1) // SEL_BLOCK
              == lax.broadcasted_iota(jnp.int32, (per_step, tk), 0)).astype(BF16)
    sel_step = sel_scr[pl.ds(pl.multiple_of(j * per_step, per_step), per_step), :].astype(BF16)
    member = lax.dot_general(sel_step, expand, tn_dims, preferred_element_type=F32) > 0.5
    flash([jnp.dot(qb[rs(g)], kT(kv, g), preferred_element_type=F32) for g in range(G)],
          [lambda p, g=g: lax.dot_general(p, vT(kv, g), NT_DIMS, preferred_element_type=F32) for g in range(G)],
          [member[rs(g)] for g in range(G)])

    @pl.when(j == n_steps - 1)
    def _():
        new_blk = PAST_LEN // SEL_BLOCK
        kpos = PAST_LEN + lax.broadcasted_iota(jnp.int32, (1, T), 1)
        ns_rows = newsel_ref[...].astype(BF16)
        base = new_blk - new_blk % 8
        pick = (lax.broadcasted_iota(jnp.int32, (8, T), 0) == new_blk % 8).astype(BF16)
        new_member = lax.dot_general(sel_scr[base:base + 8, :].astype(BF16), pick, tn_dims,
                                     preferred_element_type=F32) > 0.5
        flash([lax.dot_general(qb[rs(g)], ns_rows[:, g * D:(g + 1) * D], NT_DIMS, preferred_element_type=F32)
               for g in range(G)],
              [lambda p, g=g: jnp.dot(p, ns_rows[:, G * D + g * D:G * D + (g + 1) * D], preferred_element_type=F32)
               for g in range(G)],
              [new_member[rs(g)] & (kpos <= row_t) for g in range(G)])
        o_sel = acc_scr[...] / jnp.maximum(l_scr[...], F32_TINY)
        gs = jax.nn.sigmoid(gate_ref[...])
        o = gs[:, 0:1] * ocmp_scr[...] + gs[:, 1:2] * o_sel + gs[:, 2:3] * owin_scr[...]
        for h in range(H):
            o_ref[:, h * D:(h + 1) * D] = o[h * T:(h + 1) * T]


def _nsa_sample_attn(qh, kc, vc, pool_sel, layer, page_table, new_sel, win_cache, new_win, gate_rows):
    B, H, T, D = qh.shape
    n_pages = page_table.shape[1]
    nc = kc.shape[3]
    ns = -(-(PAST_LEN + T) // SEL_BLOCK)
    ns_pad = -(-ns // LANE) * LANE
    c2s = _cmp_to_sel_padded(nc, ns_pad)
    rows = H * T
    cst = lambda shape: pl.BlockSpec(shape, lambda b, j, pt: (0,) * len(shape))
    per_b = lambda shape: pl.BlockSpec((None,) + shape, lambda b, j, pt: (b,) + (0,) * len(shape))
    page_spec = lambda n: pl.BlockSpec((None, None, 2 * NSA_KV_DIM, PAGE_SIZE),
                                       lambda b, j, pt: (layer, pt[b, j * SEL_PAGES + n], 0, 0))
    grid_spec = pltpu.PrefetchScalarGridSpec(
        num_scalar_prefetch=1, grid=(B, n_pages // SEL_PAGES),
        in_specs=[page_spec(n) for n in range(SEL_PAGES)] + [
            per_b((H, T, D)), per_b((NSA_KV_HEADS, 3, nc, D)), per_b((NSA_KV_HEADS, nc, D)), cst(c2s.shape),
            per_b((T, 2 * NSA_KV_DIM)), per_b((2 * NSA_KV_DIM, WINDOW)), per_b((T, 2 * NSA_KV_DIM)), per_b((rows, 3))],
        out_specs=per_b((T, NSA_DIM)),
        scratch_shapes=[pltpu.VMEM((ns_pad, rows), F32), pltpu.VMEM((rows, 1), F32), pltpu.VMEM((rows, 1), F32),
                        pltpu.VMEM((rows, D), F32), pltpu.VMEM((rows, D), F32), pltpu.VMEM((rows, D), F32)])
    return pl.pallas_call(
        functools.partial(_nsa_sample_kernel, T=T, ns_pad=ns_pad), grid_spec=grid_spec,
        out_shape=jax.ShapeDtypeStruct((B, T, NSA_DIM), F32),
        compiler_params=_cparams("arbitrary", "arbitrary"),
        name="nsa_sample_attn",
    )(page_table, *([pool_sel] * SEL_PAGES), qh, kc, vc, c2s, new_sel, win_cache, new_win, gate_rows)


def _nsa_sample_pallas(q2, cmp2, sel2, win2, gates2, p, wb, B, T, layer, cache_cmp_kv, cache_sel_kv, win_buf, page_table):
    qh, _, _, _, _, sel_rows, win_rows = _nsa_prep(q2, sel2, win2, p['nsa_q_norm'], p['nsa_k_norm'], B, T)
    depth, n_pool = cache_cmp_kv.shape[:2]
    pool_cmp = cache_cmp_kv.transpose(0, 1, 3, 4, 5, 2).reshape(depth, n_pool, N_PAIR, LANE, PAGE_SIZE)
    pool_cmp = jnp.swapaxes(pool_cmp, 3, 4)
    tb_past = _half_proj_paged(pool_cmp, layer, page_table, wb['cmp_pair'])
    new_half = jnp.pad(cmp2.reshape(B, 1, T * 2 * NSA_KV_DIM), ((0, 0), (0, 7), (0, HALF_COLS - T * 2 * NSA_KV_DIM)))
    tb_new = _half_proj_rows(new_half.reshape(B * 8, HALF_COLS), wb['cmp_pair']).reshape(B, 8, -1)
    kc, vc = _cmp_finish(tb_past, tb_new, wb['cmp_w1'], p['nsa_cmp_pos'], p['nsa_cmp_b1'], wb['cmp_w2'], p['nsa_k_norm'][0])
    pool_sel = cache_sel_kv.transpose(0, 1, 3, 4, 5, 2).reshape(depth, n_pool, 2 * NSA_KV_DIM, PAGE_SIZE)
    win_cache = win_buf.transpose(0, 2, 3, 4, 1).reshape(B, 2 * NSA_KV_DIM, WINDOW)
    g = gates2[:, :NSA_KV_HEADS * 16].reshape(B, T, NSA_KV_HEADS, 4, NSA_RPG)[:, :, :, :3]
    gate_rows = g.transpose(0, 2, 4, 1, 3).reshape(B, NSA_HEADS * T, 3)
    sel3, win3 = sel_rows.reshape(B, T, -1), win_rows.reshape(B, T, -1)
    o = _nsa_sample_attn(qh, kc, vc, pool_sel, layer, page_table, sel3, win_cache, win3, gate_rows)
    shp = (B, T, 2, NSA_KV_HEADS, NSA_HD)
    win_out = jnp.concatenate([win_buf, win_rows.reshape(shp)], axis=1)[:, -WINDOW:]
    return o, (cmp2.reshape(shp), sel_rows.reshape(shp), win_out)


RW_CHUNK = 64


def _heads(x):
    return jnp.stack([x[:, h * RW_HD:(h + 1) * RW_HD] for h in range(x.shape[1] // RW_HD)], axis=0)


def _unheads(x):
    return jnp.concatenate([x[h] for h in range(x.shape[0])], axis=1)


def _bmm(a, b):
    return jnp.einsum('hlm,hmn->hln', a.astype(BF16), b.astype(BF16), preferred_element_type=F32)


def _bmm_nt(a, b):
    return jnp.einsum('hlk,hmk->hlm', a.astype(BF16), b.astype(BF16), preferred_element_type=F32)


def _bmm_tn(a, b):
    return jnp.einsum('hlv,hlk->hvk', a.astype(BF16), b.astype(BF16), preferred_element_type=F32)


def _rwkv_kernel(x_ref, shift_ref, s0_ref, mu_ref, w0_ref, a0_ref, w2_ref, a2_ref, g2_ref, kk_ref, ka_ref, rk_ref,
                 lng_ref, lnb_ref, o_ref, sfin_ref, prev_scr, s_scr):
    L = x_ref.shape[0]
    c = pl.program_id(1)

    @pl.when(c == 0)
    def _():
        prev_scr[...] = shift_ref[...]
        s_scr[...] = s0_ref[...]

    x = x_ref[...]
    prev = jnp.concatenate([prev_scr[...], x[:L - 1]], axis=0)
    prev_scr[...] = x[L - 1:]
    xs = x + mu_ref[...] * (prev - x)
    r, k, v = xs[:, :RW_DIM], xs[:, RW_DIM:2 * RW_DIM], xs[:, 2 * RW_DIM:3 * RW_DIM]
    wl = xs[:, RW_SPLITS[2]:RW_SPLITS[3]]
    al = xs[:, RW_SPLITS[3]:RW_SPLITS[4]]
    gl = xs[:, RW_SPLITS[4]:]
    w = -jax.nn.softplus(-(w0_ref[...] + jnp.dot(jnp.tanh(wl).astype(BF16), w2_ref[...], preferred_element_type=F32))) - 0.5
    logd = -jnp.exp(w)
    a = jax.nn.sigmoid(a0_ref[...] + jnp.dot(al.astype(BF16), a2_ref[...], preferred_element_type=F32))
    g = jnp.dot(jax.nn.sigmoid(gl).astype(BF16), g2_ref[...], preferred_element_type=F32)
    kk = k * kk_ref[...]
    k = k * (1.0 + (a - 1.0) * ka_ref[...])
    row = lax.broadcasted_iota(jnp.int32, (L, L), 0)
    col = lax.broadcasted_iota(jnp.int32, (L, L), 1)
    cs = jnp.dot((col <= row).astype(F32), logd, precision=HIGHEST, preferred_element_type=F32)
    g_incl, g_prev, g_inv = jnp.exp(cs), jnp.exp(cs - logd), jnp.exp(-cs)

    kk_h = _heads(kk)
    kk_h = kk_h * lax.rsqrt(jnp.sum(kk_h * kk_h, axis=-1, keepdims=True) + 1e-12)
    r_h, k_h, v_h = _heads(r), _heads(k), _heads(v)
    inv_h = _heads(g_inv)
    at = kk_h * _heads(g_prev)
    rt = r_h * _heads(g_incl)
    bt = -(kk_h * _heads(a)) * inv_h
    kt = k_h * inv_h
    ar = jnp.concatenate([at, rt], axis=1)
    bk = jnp.concatenate([bt, kt], axis=1)
    gram = _bmm_nt(ar, bk)
    strict, incl = (col < row)[None], (col <= row)[None]
    a_ab = jnp.where(strict, gram[:, :L, :L], 0.0)
    a_ak = jnp.where(strict, gram[:, :L, L:], 0.0)
    a_rbk = jnp.concatenate([jnp.where(incl, gram[:, L:, :L], 0.0), jnp.where(incl, gram[:, L:, L:], 0.0)], axis=2)
    tinv = jnp.where((col == row)[None], 1.0, a_ab)
    pw = a_ab
    n_sq, span = 0, 2
    while span < L:
        n_sq, span = n_sq + 1, span * 2
    for _ in range(n_sq):
        pw = _bmm(pw, pw)
        tinv = tinv + _bmm(tinv, pw)

    s0 = s_scr[...]
    ars = _bmm_nt(ar, s0)
    u = _bmm(tinv, ars[:, :L] + _bmm(a_ak, v_h))
    uv = jnp.concatenate([u, v_h], axis=1)
    o = ars[:, L:] + _bmm(a_rbk, uv)
    s_new = (s0 + _bmm_tn(uv, bk)) * _heads(g_incl[L - 1:])
    s_scr[...] = s_new

    @pl.when(c == pl.num_programs(1) - 1)
    def _():
        sfin_ref[...] = s_new

    mean = jnp.mean(o, axis=-1, keepdims=True)
    var = jnp.mean(jnp.square(o - mean), axis=-1, keepdims=True)
    on = _unheads((o - mean) * lax.rsqrt(var + RW_GN_EPS)) * lng_ref[...] + lnb_ref[...]
    bonus = _unheads(jnp.sum(r_h * k_h * _heads(rk_ref[...]), axis=-1, keepdims=True) * v_h)
    o_ref[...] = (on + bonus) * g


def _rwkv7_pallas(p_rw, shift_state, wkv_state, p, wb):
    B, T, _ = p_rw.shape
    L = min(RW_CHUNK, T)
    assert T % L == 0
    vec = lambda name: p[name].reshape(1, -1)
    full = lambda shape: pl.BlockSpec(shape, lambda b, c: (0,) * len(shape))
    st = pl.BlockSpec((None, RW_HEADS, RW_HD, RW_HD), lambda b, c: (b, 0, 0, 0))
    o, s_fin = pl.pallas_call(
        _rwkv_kernel,
        grid=(B, T // L),
        in_specs=[pl.BlockSpec((None, L, RW_IN), lambda b, c: (b, c, 0)),
                  pl.BlockSpec((None, 1, RW_IN), lambda b, c: (b, 0, 0)), st,
                  full((1, RW_IN)), full((1, RW_DIM)), full((1, RW_DIM)),
                  full((RW_W_LORA, RW_DIM)), full((RW_A_LORA, RW_DIM)), full((RW_G_LORA, RW_DIM)),
                  full((1, RW_DIM)), full((1, RW_DIM)), full((1, RW_DIM)), full((1, RW_DIM)), full((1, RW_DIM))],
        out_specs=[pl.BlockSpec((None, L, RW_DIM), lambda b, c: (b, c, 0)), st],
        out_shape=[jax.ShapeDtypeStruct((B, T, RW_DIM), F32),
                   jax.ShapeDtypeStruct((B, RW_HEADS, RW_HD, RW_HD), F32)],
        scratch_shapes=[pltpu.VMEM((1, RW_IN), F32), pltpu.VMEM((RW_HEADS, RW_HD, RW_HD), F32)],
        compiler_params=_cparams("arbitrary", "arbitrary"),
        name="rwkv7_mix",
    )(p_rw, shift_state.reshape(B, 1, RW_IN), wkv_state, vec('rw_mu'), vec('rw_w0'), vec('rw_a0'),
      wb['rw_w2'], wb['rw_a2'], wb['rw_g2'], vec('rw_k_k'), vec('rw_k_a'), vec('rw_r_k'), vec('rw_lnx_g'), vec('rw_lnx_b'))
    return o, p_rw[:, -1], s_fin


SSM_GW = SSM_RPG * SSM_HD


def _mamba_kernel(z_ref, xbc_ref, dt_ref, conv0_ref, s0_ref, cw_ref, cb_ref, dtb_ref, alog_ref, dvec_ref, ng_ref,
                  hexp_ref, o_ref, convf_ref, sfin_ref, tail_scr, s_scr):
    L = z_ref.shape[0]
    c = pl.program_id(1)
    last = pl.num_programs(1) - 1
    G, N, K = SSM_GROUPS, SSM_STATE, SSM_CONV

    @pl.when(c == 0)
    def _():
        tail_scr[...] = conv0_ref[...]
        s_scr[...] = s0_ref[...]

    ext = jnp.concatenate([tail_scr[...], xbc_ref[...]], axis=0)
    new_tail = ext[L:L + K - 1]
    tail_scr[...] = new_tail
    conv = cb_ref[...] + sum(ext[i:i + L] * cw_ref[i:i + 1] for i in range(K))
    act = jax.nn.silu(conv)
    xs, bm, cm = act[:, :SSM_DIM], act[:, SSM_DIM:SSM_DIM + G * N], act[:, SSM_DIM + G * N:]

    dt = jax.nn.softplus(dt_ref[...] + dtb_ref[...])
    da = dt * (-jnp.exp(alog_ref[...]))
    row = lax.broadcasted_iota(jnp.int32, (L, L), 0)
    col = lax.broadcasted_iota(jnp.int32, (L, L), 1)
    causal = col <= row
    a_cs = jnp.dot(causal.astype(F32), da, precision=HIGHEST, preferred_element_type=F32)
    a_last = a_cs[L - 1:]
    a_cs_t = a_cs.T
    per_head = jnp.concatenate([dt, dt * jnp.exp(a_last - a_cs), jnp.exp(a_cs)], axis=0)
    wide = jnp.dot(per_head, hexp_ref[...], precision=HIGHEST, preferred_element_type=F32)
    xdt_b = (xs * wide[:L]).astype(BF16)
    xds = (xs * wide[L:2 * L]).astype(BF16)
    dec_in = wide[2 * L:]
    st_head = lax.broadcasted_iota(jnp.int32, (SSM_GW, LANE), 0) // SSM_HD
    st_lane = lax.broadcasted_iota(jnp.int32, (SSM_GW, LANE), 1)
    dec_chunk = jnp.exp(a_last)

    ys = []
    for g in range(G):
        bg = bm[:, g * N:(g + 1) * N].astype(BF16)
        cg = cm[:, g * N:(g + 1) * N].astype(BF16)
        cb = lax.dot_general(cg, bg, NT_DIMS, preferred_element_type=F32)
        s_in = s_scr[g]
        y_off = lax.dot_general(cg, s_in.astype(BF16), NT_DIMS, preferred_element_type=F32)
        cols = slice(g * SSM_GW, (g + 1) * SSM_GW)
        y_diag = []
        for r in range(SSM_RPG):
            h = g * SSM_RPG + r
            seg = a_cs[:, h:h + 1] - a_cs_t[h:h + 1, :]
            w = (cb * jnp.exp(jnp.where(causal, seg, -jnp.inf))).astype(BF16)
            y_diag.append(jnp.dot(w, xdt_b[:, h * SSM_HD:(h + 1) * SSM_HD], preferred_element_type=F32))
        ys.append(jnp.concatenate(y_diag, axis=1) + y_off * dec_in[:, cols])
        states = lax.dot_general(xds[:, cols], bg, (((0,), (0,)), ((), ())), preferred_element_type=F32)
        dec_col = jnp.sum(jnp.where(st_head + g * SSM_RPG == st_lane, dec_chunk, 0.0), axis=-1, keepdims=True)
        s_scr[g] = s_in * dec_col + states
    y = jnp.concatenate(ys, axis=1) + dvec_ref[...] * xs
    y = y * jax.nn.silu(z_ref[...])
    o_ref[...] = y * lax.rsqrt(jnp.mean(y * y, axis=-1, keepdims=True) + EPS) * ng_ref[...]

    @pl.when(c == last)
    def _():
        convf_ref[...] = new_tail
        sfin_ref[...] = s_scr[...]


def _mamba2_pallas(z, xbc, dt, conv_state, ssm_state, p):
    B, T, _ = z.shape
    L = SSM_CHUNK if T % SSM_CHUNK == 0 else T
    pad = lambda v: jnp.pad(v, (0, LANE - SSM_HEADS)).reshape(1, LANE)
    hexp = (jnp.arange(LANE)[:, None] == jnp.arange(SSM_DIM)[None, :] // SSM_HD).astype(F32)
    full = lambda shape: pl.BlockSpec(shape, lambda b, c: (0,) * len(shape))
    tile = lambda wd: pl.BlockSpec((None, L, wd), lambda b, c: (b, c, 0))
    per_b = lambda shape: pl.BlockSpec((None,) + shape, lambda b, c: (b,) + (0,) * len(shape))
    st_shape = (SSM_GROUPS, SSM_GW, SSM_STATE)
    y, conv_f, s_fin = pl.pallas_call(
        _mamba_kernel,
        grid=(B, T // L),
        in_specs=[tile(SSM_DIM), tile(SSM_XBC), tile(LANE), per_b((SSM_CONV - 1, SSM_XBC)), per_b(st_shape),
                  full((SSM_CONV, SSM_XBC)), full((1, SSM_XBC)), full((1, LANE)), full((1, LANE)), full((1, SSM_DIM)),
                  full((1, SSM_DIM)), full((LANE, SSM_DIM))],
        out_specs=[tile(SSM_DIM), per_b((SSM_CONV - 1, SSM_XBC)), per_b(st_shape)],
        out_shape=[jax.ShapeDtypeStruct((B, T, SSM_DIM), F32),
                   jax.ShapeDtypeStruct((B, SSM_CONV - 1, SSM_XBC), F32),
                   jax.ShapeDtypeStruct((B,) + st_shape, F32)],
        scratch_shapes=[pltpu.VMEM((SSM_CONV - 1, SSM_XBC), F32), pltpu.VMEM(st_shape, F32)],
        compiler_params=_cparams("arbitrary", "arbitrary"),
        name="mamba2_mix",
    )(z, xbc, dt, conv_state, ssm_state.reshape((B,) + st_shape), p['ssm_conv_w'], p['ssm_conv_b'].reshape(1, -1),
      pad(p['ssm_dt_bias']), pad(p['ssm_a_log']), jnp.repeat(p['ssm_d'], SSM_HD).reshape(1, -1),
      p['ssm_norm_g'].reshape(1, -1), hexp)
    return y, conv_f, s_fin.reshape(B, SSM_HEADS, SSM_HD, SSM_STATE)


def _prep_weights(p):
    w_in = p['w_in']
    o_ssm = RW_IN
    o_nsa = RW_IN + SSM_IN
    o_gate = o_nsa + NSA_IN
    pad = lambda w, n: jnp.pad(w, ((0, 0), (0, n - w.shape[1])))
    w_rw = w_in[:, :RW_IN]
    w_ssm = jnp.concatenate([w_in[:, o_ssm:o_ssm + SSM_DIM + SSM_XBC],
                             pad(w_in[:, o_ssm + SSM_DIM + SSM_XBC:o_nsa], LANE)], axis=1)
    o_g = o_nsa + NSA_DIM + 6 * NSA_KV_DIM
    wg = w_in[:, o_g:o_gate].reshape(D_MODEL, 3, NSA_KV_HEADS, NSA_RPG).transpose(0, 2, 1, 3)
    wg = jnp.pad(wg.reshape(D_MODEL, NSA_KV_HEADS, 3 * NSA_RPG), ((0, 0), (0, 0), (0, 16 - 3 * NSA_RPG)))
    w_nsa = jnp.concatenate([w_in[:, o_nsa:o_g], pad(wg.reshape(D_MODEL, NSA_KV_HEADS * 16), LANE),
                             w_in[:, o_gate:]], axis=1)
    bf = lambda w: w.astype(BF16)
    return dict(w_ada=bf(p['w_ada']), w_rw=bf(w_rw), w_ssm=bf(w_ssm), w_nsa=bf(w_nsa),
                w_br_rw=bf(p['w_br_rw']), w_br_ssm=bf(p['w_br_ssm']), w_br_nsa=bf(p['w_br_nsa']),
                w_out=bf(p['w_out']), w_ffn_in=bf(p['w_ffn_in']), w_ffn_out=bf(p['w_ffn_out']),
                rw_w2=bf(p['rw_w2']), rw_a2=bf(p['rw_a2']), rw_g2=bf(p['rw_g2']),
                cmp_w1=bf(p['nsa_cmp_w1'].reshape(2, CMP_BLOCK * NSA_HD, CMP_HIDDEN)), cmp_w2=bf(p['nsa_cmp_w2']),
                cmp_pair=_pair_weights(p['nsa_cmp_w1']))


def _trunk_layer(x, c, p, wb, rw_shift, rw_wkv, conv_state, ssm_state, nsa_fn, per_batch):
    B, T, _ = x.shape
    n = B * T
    mod = _ada(c, wb['w_ada'], p['b_ada'])
    if per_batch:
        mods = [m.reshape(B, 1, D_MODEL) for m in jnp.split(mod, 6, axis=-1)]
    else:
        mods = [jnp.repeat(m, T, axis=0) for m in jnp.split(mod, 6, axis=-1)]
    sh1, sc1, gt1, sh2, sc2, gt2 = mods
    x2 = x.reshape(n, D_MODEL)
    (p_rw,) = _norm_proj(x2, sc1, sh1, p['ln1'], wb['w_rw'], (RW_IN,), per_batch, T, "proj_rw")
    z, xbc, dt = _norm_proj(x2, sc1, sh1, p['ln1'], wb['w_ssm'], (SSM_DIM, SSM_XBC, LANE), per_batch, T, "proj_ssm")
    kvw = 2 * NSA_KV_DIM
    q, cmp2, sel2, win2, gates, p_gate = _norm_proj(x2, sc1, sh1, p['ln1'], wb['w_nsa'],
                                                    (NSA_DIM, kvw, kvw, kvw, LANE, N_BRANCH * D_MODEL), per_batch, T,
                                                    "proj_nsa")
    r3 = lambda t: t.reshape(B, T, t.shape[-1])
    o_rw, rw_shift, rw_wkv = _rwkv7_pallas(r3(p_rw), rw_shift, rw_wkv, p, wb)
    o_ssm, conv_state, ssm_state = _mamba2_pallas(r3(z), r3(xbc), r3(dt), conv_state, ssm_state, p)
    o_nsa, (cmp_rows, sel_rows, win_buf) = nsa_fn(q, cmp2, sel2, win2, gates, p, wb, B, T)
    x2 = _merge(x2, gt1, o_rw.reshape(n, -1), o_ssm.reshape(n, -1), o_nsa.reshape(n, -1), p_gate,
                wb['w_br_rw'], wb['w_br_ssm'], wb['w_br_nsa'], wb['w_out'], per_batch, T)
    x2 = _ffn(x2, sc2, sh2, gt2, p['ln2'], wb['w_ffn_in'], wb['w_ffn_out'], per_batch, T)
    return x2.reshape(B, T, D_MODEL), (rw_shift, rw_wkv, conv_state, ssm_state, cmp_rows, sel_rows, win_buf)


def kernel(x_prompt, x_sample, cache_cmp_kv, cache_sel_kv, cache_win_kv, state_rwkv_shift, state_rwkv_wkv,
           state_ssm_conv, state_ssm, page_table, c_prompt, c_sample, w_ada, b_ada, ln1, ln2, w_in,
           rw_mu, rw_w0, rw_w2, rw_a0, rw_a2, rw_g2, rw_k_k, rw_k_a, rw_r_k, rw_lnx_g, rw_lnx_b,
           ssm_conv_w, ssm_conv_b, ssm_dt_bias, ssm_a_log, ssm_d, ssm_norm_g,
           nsa_q_norm, nsa_k_norm, nsa_cmp_pos, nsa_cmp_w1, nsa_cmp_b1, nsa_cmp_w2,
           w_br_rw, w_br_ssm, w_br_nsa, w_out, w_ffn_in, w_ffn_out):
    params = dict(w_ada=w_ada, b_ada=b_ada, ln1=ln1, ln2=ln2, w_in=w_in,
                  rw_mu=rw_mu, rw_w0=rw_w0, rw_w2=rw_w2, rw_a0=rw_a0, rw_a2=rw_a2, rw_g2=rw_g2,
                  rw_k_k=rw_k_k, rw_k_a=rw_k_a, rw_r_k=rw_r_k, rw_lnx_g=rw_lnx_g, rw_lnx_b=rw_lnx_b,
                  ssm_conv_w=ssm_conv_w, ssm_conv_b=ssm_conv_b, ssm_dt_bias=ssm_dt_bias, ssm_a_log=ssm_a_log,
                  ssm_d=ssm_d, ssm_norm_g=ssm_norm_g, nsa_q_norm=nsa_q_norm, nsa_k_norm=nsa_k_norm,
                  nsa_cmp_pos=nsa_cmp_pos, nsa_cmp_w1=nsa_cmp_w1, nsa_cmp_b1=nsa_cmp_b1, nsa_cmp_w2=nsa_cmp_w2,
                  w_br_rw=w_br_rw, w_br_ssm=w_br_ssm, w_br_nsa=w_br_nsa, w_out=w_out,
                  w_ffn_in=w_ffn_in, w_ffn_out=w_ffn_out)
    bp = x_prompt.shape[0]
    depth = w_in.shape[0]
    zero_shift = jnp.zeros((bp, RW_IN), F32)
    zero_wkv = jnp.zeros((bp, RW_HEADS, RW_HD, RW_HD), F32)
    zero_conv = jnp.zeros((bp, SSM_CONV - 1, SSM_XBC), F32)
    zero_ssm = jnp.zeros((bp, SSM_HEADS, SSM_HD, SSM_STATE), F32)
    xp, xs = x_prompt, x_sample
    st_p, st_s = [], []
    for l in range(depth):
        p = {name: arr[l] for name, arr in params.items()}
        wb = _prep_weights(p)
        xp, sp_l = _trunk_layer(xp, c_prompt, p, wb, zero_shift, zero_wkv, zero_conv, zero_ssm, _nsa_prompt_pallas, True)
        nsa_s = functools.partial(_nsa_sample_pallas, layer=l, cache_cmp_kv=cache_cmp_kv, cache_sel_kv=cache_sel_kv,
                                  win_buf=cache_win_kv[l], page_table=page_table)
        xs, ss_l = _trunk_layer(xs, c_sample, p, wb, state_rwkv_shift[l], state_rwkv_wkv[l], state_ssm_conv[l],
                                state_ssm[l], nsa_s, False)
        st_p.append(sp_l)
        st_s.append(ss_l)
    sp = [jnp.stack([s[i] for s in st_p]) for i in range(7)]
    ss = [jnp.stack([s[i] for s in st_s]) for i in range(7)]
    return (xp, xs, sp[4], ss[4], sp[5], ss[5], sp[6], ss[6], sp[0], ss[0], sp[1], ss[1], sp[2], ss[2], sp[3], ss[3])
```

```python
import functools

import jax
import jax.numpy as jnp
from jax import lax
from jax.experimental import pallas as pl
from jax.experimental.pallas import tpu as pltpu

F32 = jnp.float32
BF16 = jnp.bfloat16

D_MODEL = 1024
PAST_LEN = 16384
PAGE_SIZE = 128
RW_HEADS = 16
RW_HD = 64
RW_DIM = RW_HEADS * RW_HD
RW_W_LORA = 64
RW_A_LORA = 64
RW_G_LORA = 128
RW_IN = 3 * RW_DIM + RW_W_LORA + RW_A_LORA + RW_G_LORA
RW_SPLITS = (RW_DIM, 2 * RW_DIM, 3 * RW_DIM, 3 * RW_DIM + RW_W_LORA, 3 * RW_DIM + RW_W_LORA + RW_A_LORA)
RW_GN_EPS = 64e-5
SSM_DIM = 2 * D_MODEL
SSM_HD = 64
SSM_HEADS = SSM_DIM // SSM_HD
SSM_GROUPS = 4
SSM_RPG = SSM_HEADS // SSM_GROUPS
SSM_STATE = 128
SSM_CONV = 4
SSM_CHUNK = 128
SSM_XBC = SSM_DIM + 2 * SSM_GROUPS * SSM_STATE
SSM_IN = SSM_DIM + SSM_XBC + SSM_HEADS
NSA_HEADS = 16
NSA_HD = 64
NSA_KV_HEADS = 4
NSA_RPG = NSA_HEADS // NSA_KV_HEADS
NSA_DIM = NSA_HEADS * NSA_HD
NSA_KV_DIM = NSA_KV_HEADS * NSA_HD
CMP_STRIDE = 16
CMP_BLOCK = 2 * CMP_STRIDE
CMP_HIDDEN = 128
SEL_BLOCK = 64
N_SELECT = 16
WINDOW = 512
Q_BLOCK = 64
SEL_FORCE = 1e6
NSA_IN = NSA_DIM + 6 * NSA_KV_DIM + 3 * NSA_HEADS
N_BRANCH = 3
FFN_HIDDEN = ((8 * D_MODEL // 3 + 255) // 256) * 256
EPS = 1e-6

LANE = 128
VMEM_LIMIT = 56 * 1024 * 1024


def _cparams(*sem):
    return pltpu.CompilerParams(dimension_semantics=sem, vmem_limit_bytes=VMEM_LIMIT)


def _modnorm(x, g, sc, sh):
    y = x * lax.rsqrt(jnp.mean(x * x, axis=-1, keepdims=True) + EPS)
    return (y * g) * (1.0 + sc) + sh


def _mod_spec(per_batch, tm, tiles_per_batch):
    if per_batch:
        return pl.BlockSpec((None, 1, D_MODEL), lambda i: (i // tiles_per_batch, 0, 0))
    return pl.BlockSpec((tm, D_MODEL), lambda i: (i, 0))


def _full_spec(shape):
    return pl.BlockSpec(shape, lambda i: (0,) * len(shape))


def _ada_kernel(c_ref, w_ref, b_ref, o_ref):
    o_ref[...] = jnp.dot(c_ref[...].astype(BF16), w_ref[...], preferred_element_type=F32) + b_ref[...]


def _ada(c, w_bf, b):
    n, tn = c.shape[0], 1536
    return pl.pallas_call(
        _ada_kernel,
        grid=(6 * D_MODEL // tn,),
        in_specs=[pl.BlockSpec((n, D_MODEL), lambda j: (0, 0)),
                  pl.BlockSpec((D_MODEL, tn), lambda j: (0, j)),
                  pl.BlockSpec((1, tn), lambda j: (0, j))],
        out_specs=pl.BlockSpec((n, tn), lambda j: (0, j)),
        out_shape=jax.ShapeDtypeStruct((n, 6 * D_MODEL), F32),
        compiler_params=_cparams("arbitrary"),
        name="ada_mod",
    )(c, w_bf, b.reshape(1, -1))


def _norm_proj_kernel(x_ref, sc_ref, sh_ref, g_ref, w_ref, *o_refs, splits):
    h = _modnorm(x_ref[...], g_ref[...], sc_ref[...], sh_ref[...]).astype(BF16)
    for o_ref, (a, b) in zip(o_refs, splits):
        o_ref[...] = jnp.dot(h, w_ref[:, a:b], preferred_element_type=F32)


def _norm_proj(x2, sc, sh, g, w_bf, widths, per_batch, tokens_per_batch, name):
    n = x2.shape[0]
    tm = min(512, n)
    splits, a = [], 0
    for wd in widths:
        splits.append((a, a + wd))
        a += wd
    assert a == w_bf.shape[1] and n % tm == 0
    tpb = max(tokens_per_batch // tm, 1)
    return pl.pallas_call(
        functools.partial(_norm_proj_kernel, splits=tuple(splits)),
        grid=(n // tm,),
        in_specs=[pl.BlockSpec((tm, D_MODEL), lambda i: (i, 0)),
                  _mod_spec(per_batch, tm, tpb), _mod_spec(per_batch, tm, tpb),
                  _full_spec((1, D_MODEL)), _full_spec(w_bf.shape)],
        out_specs=[pl.BlockSpec((tm, wd), lambda i: (i, 0)) for wd in widths],
        out_shape=[jax.ShapeDtypeStruct((n, wd), F32) for wd in widths],
        compiler_params=_cparams("arbitrary"),
        name=name,
    )(x2, sc, sh, g.reshape(1, -1), w_bf)


def _merge_kernel(x_ref, gt_ref, orw_ref, ossm_ref, onsa_ref, gate_ref, wrw_ref, wssm_ref, wnsa_ref, wout_ref,
                  o_ref):
    gate = jax.nn.sigmoid(gate_ref[...])
    br = (jnp.dot(orw_ref[...].astype(BF16), wrw_ref[...], preferred_element_type=F32),
          jnp.dot(ossm_ref[...].astype(BF16), wssm_ref[...], preferred_element_type=F32),
          jnp.dot(onsa_ref[...].astype(BF16), wnsa_ref[...], preferred_element_type=F32))
    merged = sum(gate[:, i * D_MODEL:(i + 1) * D_MODEL] * br[i] for i in range(N_BRANCH))
    y = jnp.dot(merged.astype(BF16), wout_ref[...], preferred_element_type=F32)
    o_ref[...] = x_ref[...] + gt_ref[...] * y


def _merge(x2, gt, o_rw, o_ssm, o_nsa, gate, w_rw, w_ssm, w_nsa, w_out, per_batch, tokens_per_batch):
    n = x2.shape[0]
    tm = min(512, n)
    tpb = max(tokens_per_batch // tm, 1)
    row = lambda wd: pl.BlockSpec((tm, wd), lambda i: (i, 0))
    return pl.pallas_call(
        _merge_kernel,
        grid=(n // tm,),
        in_specs=[row(D_MODEL), _mod_spec(per_batch, tm, tpb), row(RW_DIM), row(SSM_DIM), row(NSA_DIM),
                  row(N_BRANCH * D_MODEL), _full_spec(w_rw.shape), _full_spec(w_ssm.shape),
                  _full_spec(w_nsa.shape), _full_spec(w_out.shape)],
        out_specs=row(D_MODEL),
        out_shape=jax.ShapeDtypeStruct((n, D_MODEL), F32),
        compiler_params=_cparams("arbitrary"),
        name="branch_merge",
    )(x2, gt, o_rw, o_ssm, o_nsa, gate, w_rw, w_ssm, w_nsa, w_out)


FFN_CHUNK = 256


def _ffn_kernel(x_ref, sc_ref, sh_ref, gt_ref, g_ref, win_ref, wout_ref, o_ref):
    x = x_ref[...]
    h = _modnorm(x, g_ref[...], sc_ref[...], sh_ref[...]).astype(BF16)
    acc = jnp.zeros(x.shape, F32)
    for c in range(FFN_HIDDEN // FFN_CHUNK):
        a = c * FFN_CHUNK
        up = jnp.dot(h, win_ref[:, a:a + FFN_CHUNK], preferred_element_type=F32)
        gf = jnp.dot(h, win_ref[:, FFN_HIDDEN + a:FFN_HIDDEN + a + FFN_CHUNK], preferred_element_type=F32)
        act = (jax.nn.silu(gf) * up).astype(BF16)
        acc = acc + jnp.dot(act, wout_ref[a:a + FFN_CHUNK, :], preferred_element_type=F32)
    o_ref[...] = x + gt_ref[...] * acc


def _ffn(x2, sc, sh, gt, g, w_in, w_out, per_batch, tokens_per_batch):
    n = x2.shape[0]
    tm = min(512, n)
    tpb = max(tokens_per_batch // tm, 1)
    row = pl.BlockSpec((tm, D_MODEL), lambda i: (i, 0))
    mod = _mod_spec(per_batch, tm, tpb)
    return pl.pallas_call(
        _ffn_kernel,
        grid=(n // tm,),
        in_specs=[row, mod, mod, mod, _full_spec((1, D_MODEL)), _full_spec(w_in.shape), _full_spec(w_out.shape)],
        out_specs=row,
        out_shape=jax.ShapeDtypeStruct((n, D_MODEL), F32),
        compiler_params=_cparams("arbitrary"),
        name="ffn",
    )(x2, sc, sh, gt, g.reshape(1, -1), w_in, w_out)


HIGHEST = lax.Precision.HIGHEST
NT_DIMS = (((1,), (1,)), ((), ()))
NEG_BIG = -1e30
F32_TINY = float(jnp.finfo(jnp.float32).tiny)


def _split3(x):
    hi = x.astype(BF16)
    r = x - hi.astype(F32)
    mid = r.astype(BF16)
    return hi, mid, (r - mid.astype(F32)).astype(BF16)


def _dot_f32_nt(a3, b3):
    pairs = ((2, 0), (0, 2), (1, 1), (1, 0), (0, 1), (0, 0))
    return sum(lax.dot_general(a3[i], b3[j], NT_DIMS, preferred_element_type=F32) for i, j in pairs)


def _dot_f32_exact_rhs(a, b_bf):
    return sum(jnp.dot(t, b_bf, preferred_element_type=F32) for t in reversed(_split3(a)))


def _seg_rmsnorm(x, seg, seg_t, gain):
    ss = jnp.dot(x * x, seg, precision=HIGHEST, preferred_element_type=F32)
    inv = lax.rsqrt(ss * (1.0 / NSA_HD) + EPS)
    return x * jnp.dot(inv, seg_t, precision=HIGHEST, preferred_element_type=F32) * gain


def _nsa_prep_kernel(q_ref, sel_ref, win_ref, seg_ref, segt_ref, qg_ref, kg_ref,
                     qh_ref, ksh_ref, vsh_ref, kwh_ref, vwh_ref, selo_ref, wino_ref):
    seg, segt = seg_ref[...], segt_ref[...]
    qn = _seg_rmsnorm(q_ref[...], seg, segt, qg_ref[...]) * (NSA_HD ** -0.5)
    for h in range(NSA_HEADS):
        qh_ref[h] = qn[:, h * NSA_HD:(h + 1) * NSA_HD]
    kg = kg_ref[...]
    for src, dst, kh_ref, vh_ref, row in ((sel_ref, selo_ref, ksh_ref, vsh_ref, 1), (win_ref, wino_ref, kwh_ref, vwh_ref, 2)):
        x = src[...]
        kn = _seg_rmsnorm(x[:, :NSA_KV_DIM], seg[:NSA_KV_DIM], segt[:, :NSA_KV_DIM], kg[row:row + 1])
        v = x[:, NSA_KV_DIM:]
        dst[:, :NSA_KV_DIM] = kn
        dst[:, NSA_KV_DIM:] = v
        ones_col = (lax.broadcasted_iota(jnp.int32, (x.shape[0], NSA_HD), 1) == 0).astype(BF16)
        for g in range(NSA_KV_HEADS):
            kh_ref[g] = kn[:, g * NSA_HD:(g + 1) * NSA_HD].astype(BF16)
            vh_ref[g] = jnp.concatenate([v[:, g * NSA_HD:(g + 1) * NSA_HD].astype(BF16), ones_col], axis=1)


def _seg_mats():
    c = jnp.arange(NSA_DIM)[:, None] // NSA_HD
    seg = (c == jnp.arange(LANE)[None, :]).astype(F32)
    return seg, seg.T


def _nsa_prep(q2, sel2, win2, q_norm, k_norm, B, T):
    tm = min(512, T)
    tpb = T // tm
    seg, segt = _seg_mats()
    row = lambda wd: pl.BlockSpec((tm, wd), lambda i: (i, 0))
    hm = lambda nh, wd=NSA_HD: pl.BlockSpec((None, nh, tm, wd), lambda i: (i // tpb, 0, i % tpb, 0))
    ks = jax.ShapeDtypeStruct((B, NSA_KV_HEADS, T, NSA_HD), BF16)
    vs = jax.ShapeDtypeStruct((B, NSA_KV_HEADS, T, 2 * NSA_HD), BF16)
    rows = jax.ShapeDtypeStruct((B * T, 2 * NSA_KV_DIM), F32)
    return pl.pallas_call(
        _nsa_prep_kernel,
        grid=(B * T // tm,),
        in_specs=[row(NSA_DIM), row(2 * NSA_KV_DIM), row(2 * NSA_KV_DIM), _full_spec(seg.shape), _full_spec(segt.shape),
                  _full_spec((1, NSA_DIM)), _full_spec((3, NSA_KV_DIM))],
        out_specs=[hm(NSA_HEADS), hm(NSA_KV_HEADS), hm(NSA_KV_HEADS, 2 * NSA_HD), hm(NSA_KV_HEADS),
                   hm(NSA_KV_HEADS, 2 * NSA_HD), row(2 * NSA_KV_DIM), row(2 * NSA_KV_DIM)],
        out_shape=[jax.ShapeDtypeStruct((B, NSA_HEADS, T, NSA_HD), F32), ks, vs, ks, vs, rows, rows],
        compiler_params=_cparams("arbitrary"),
        name="nsa_prep",
    )(q2, sel2, win2, seg, segt, jnp.tile(q_norm, NSA_HEADS).reshape(1, -1), jnp.tile(k_norm, (1, NSA_KV_HEADS)))


def _nsa_cmp_kernel(xk_ref, xv_ref, w1_ref, pos_ref, b1_ref, w2_ref, kn_ref, kc_ref, vc_ref, *, n_valid):
    half = CMP_STRIDE * NSA_HD
    for c, x_ref in ((0, xk_ref), (1, xv_ref)):
        x = x_ref[...]
        const = jnp.dot(pos_ref[c].astype(BF16), w1_ref[c], preferred_element_type=F32)[0:1] + b1_ref[c:c + 1]
        top = jnp.dot(x, w1_ref[c, :half, :], preferred_element_type=F32)
        bot = jnp.dot(x, w1_ref[c, half:, :], preferred_element_type=F32)
        bot_next = jnp.concatenate([bot[1:], jnp.zeros((1, CMP_HIDDEN), F32)], axis=0)
        hid = jax.nn.gelu(top + bot_next + const)
        kv = jnp.dot(hid.astype(BF16), w2_ref[c], preferred_element_type=F32)
        if c == 0:
            kv = kv * lax.rsqrt(jnp.mean(kv * kv, axis=-1, keepdims=True) + EPS) * kn_ref[...]
            for i, term in enumerate(_split3(kv)):
                kc_ref[i] = term
        else:
            vc_ref[...] = kv.astype(BF16)


def _nsa_compress(xh, w1_bf, pos, b1, w2_bf, k_norm0):
    B, _, nh, _ = xh.shape
    blk = lambda off: pl.BlockSpec((None, None, nh, CMP_STRIDE * NSA_HD), lambda b, g: (b, off + g, 0, 0))
    full = lambda shape: pl.BlockSpec(shape, lambda b, g: (0,) * len(shape))
    out = pl.BlockSpec((None, None, nh, NSA_HD), lambda b, g: (b, g, 0, 0))
    return pl.pallas_call(
        functools.partial(_nsa_cmp_kernel, n_valid=nh - 1),
        grid=(B, NSA_KV_HEADS),
        in_specs=[blk(0), blk(NSA_KV_HEADS), full(w1_bf.shape), full((2, 8, CMP_BLOCK * NSA_HD)), full((2, CMP_HIDDEN)),
                  full(w2_bf.shape), full((1, NSA_HD))],
        out_specs=[pl.BlockSpec((None, None, 3, nh, NSA_HD), lambda b, g: (b, g, 0, 0, 0)), out],
        out_shape=[jax.ShapeDtypeStruct((B, NSA_KV_HEADS, 3, nh, NSA_HD), BF16),
                   jax.ShapeDtypeStruct((B, NSA_KV_HEADS, nh, NSA_HD), BF16)],
        compiler_params=_cparams("arbitrary", "arbitrary"),
        name="nsa_compress",
    )(xh, xh, w1_bf, jnp.broadcast_to(pos.reshape(2, 1, -1), (2, 8, CMP_BLOCK * NSA_HD)), b1, w2_bf, k_norm0.reshape(1, -1))


NSA_TQ = 512
NSA_TK = 512


def _softmax_direct(s, mask):
    sm = jnp.where(mask[None], s, NEG_BIG)
    m = jnp.max(sm, axis=-1, keepdims=True)
    e = jnp.where(mask[None], jnp.exp(sm - m), 0.0)
    return e / jnp.maximum(jnp.sum(e, axis=-1, keepdims=True), F32_TINY)


def _top_blocks(score, n_top):
    ns = score.shape[-1]
    lane = lax.broadcasted_iota(jnp.int32, score.shape, 1).astype(F32)
    sel = jnp.zeros(score.shape, F32)
    for _ in range(n_top):
        m = jnp.max(score, axis=-1, keepdims=True)
        first = jnp.min(jnp.where(score == m, lane, float(ns)), axis=-1, keepdims=True)
        hit = lane == first
        sel = jnp.where(hit, 1.0, sel)
        score = jnp.where(hit, -jnp.inf, score)
    return sel


def _nsa_prompt_kernel(q_ref, kc_ref, vc_ref, c2s_ref, ks_ref, vs_ref, kw_ref, vw_ref, g_ref, o_ref, bias_ref):
    R, TQ, TK = NSA_RPG, NSA_TQ, NSA_TK
    T = ks_ref.shape[0]
    nc = kc_ref.shape[1]
    i = pl.program_id(2)
    q0 = i * TQ
    D = NSA_HD
    q = q_ref[...].reshape(R * TQ, D)
    qb = q.astype(BF16)
    t_row = q0 + lax.broadcasted_iota(jnp.int32, (TQ, 1), 0)

    wk = WINDOW + TQ
    ws = pl.multiple_of(jnp.maximum(q0 - WINDOW, 0), TQ)
    kw = kw_ref[pl.ds(ws, wk), :]
    vw = vw_ref[pl.ds(ws, wk), :]
    kpos = ws + lax.broadcasted_iota(jnp.int32, (1, wk), 1)
    sw = lax.dot_general(qb, kw, NT_DIMS, preferred_element_type=F32).reshape(R, TQ, wk)
    sw = sw + jnp.where((kpos <= t_row) & (kpos > t_row - WINDOW), 0.0, NEG_BIG)[None]
    ew = jnp.exp(sw - jnp.max(sw, axis=-1, keepdims=True))
    ow = jnp.dot(ew.reshape(R * TQ, wk).astype(BF16), vw, preferred_element_type=F32).reshape(R, TQ, 2 * D)
    o_win = ow[..., :D] / ow[..., D:D + 1]

    s = _dot_f32_nt(_split3(q), (kc_ref[0], kc_ref[1], kc_ref[2])).reshape(R, TQ, nc)
    cmp_end = lax.broadcasted_iota(jnp.int32, (1, nc), 1) * CMP_STRIDE + (CMP_BLOCK - 1)
    p = _softmax_direct(s, cmp_end <= t_row)
    o_cmp = jnp.dot(p.reshape(R * TQ, nc).astype(BF16), vc_ref[...], preferred_element_type=F32).reshape(R, TQ, D)
    imp = _dot_f32_exact_rhs(jnp.sum(p, axis=0), c2s_ref[...])
    gs = jax.nn.sigmoid(g_ref[...])
    o_cw = [gs[:, r:r + 1] * o_cmp[r] + gs[:, 2 * R + r:2 * R + r + 1] * o_win[r] for r in range(R)]
    ns = imp.shape[-1]
    blk = lax.broadcasted_iota(jnp.int32, (1, ns), 1)
    cur = t_row // SEL_BLOCK
    forced = (blk == 0) | (blk == cur) | (blk == cur - 1)
    score = jnp.where(forced, SEL_FORCE, jnp.where(blk * SEL_BLOCK <= t_row, imp, -SEL_FORCE))
    sel = _top_blocks(score, min(N_SELECT, ns)).astype(BF16)
    per_tile = TK // SEL_BLOCK
    key_blk = lax.broadcasted_iota(jnp.int32, (ns, TK), 1) // SEL_BLOCK
    row_blk = lax.broadcasted_iota(jnp.int32, (ns, TK), 0)
    for j in range(T // TK):
        @pl.when(j * TK < q0 + TQ)
        def _(j=j):
            expand = (key_blk + j * per_tile == row_blk).astype(BF16)
            member = jnp.dot(sel, expand, preferred_element_type=F32) > 0.5
            kpos_j = j * TK + lax.broadcasted_iota(jnp.int32, (1, TK), 1)
            bias_ref[j] = jnp.where(member & (kpos_j <= t_row), 0.0, NEG_BIG)

    def body(j, carry):
        m, acc = carry
        start = pl.multiple_of(j * TK, TK)
        k = ks_ref[pl.ds(start, TK), :]
        v = vs_ref[pl.ds(start, TK), :]
        sm = lax.dot_general(qb, k, NT_DIMS, preferred_element_type=F32).reshape(R, TQ, TK) + bias_ref[j][None]
        m_new = jnp.maximum(m, jnp.max(sm, axis=-1, keepdims=True))
        pj = jnp.exp(sm - m_new).astype(BF16)
        pv = jnp.dot(pj.reshape(R * TQ, TK), v, preferred_element_type=F32).reshape(R, TQ, 2 * D)
        return m_new, jnp.exp(m - m_new) * acc + pv

    carry0 = (jnp.full((R, TQ, 1), NEG_BIG, F32), jnp.zeros((R, TQ, 2 * D), F32))
    _, acc = lax.fori_loop(0, (q0 + TQ + TK - 1) // TK, body, carry0)
    o_sel = acc[..., :D] / acc[..., D:D + 1]
    for r in range(R):
        o_ref[:, r * D:(r + 1) * D] = o_cw[r] + gs[:, R + r:R + r + 1] * o_sel[r]


def _cmp_to_sel_padded(nh, ns):
    i = jnp.arange(nh)[:, None] * CMP_STRIDE
    j = jnp.arange(ns)[None, :] * SEL_BLOCK
    ov = jnp.minimum(i + CMP_BLOCK, j + SEL_BLOCK) - jnp.maximum(i, j)
    return (jnp.clip(ov, 0, None).astype(F32) / CMP_BLOCK).astype(BF16)


def _nsa_prompt_attn(qh, kc, vc, ksh, vsh, kwh, vwh, gates4):
    B, _, T, _ = qh.shape
    nh = kc.shape[3]
    ns = T // SEL_BLOCK
    c2s = _cmp_to_sel_padded(nh, ns)
    kv = lambda n, wd=NSA_HD: pl.BlockSpec((None, None, n, wd), lambda b, g, i: (b, g, 0, 0))
    return pl.pallas_call(
        _nsa_prompt_kernel,
        grid=(B, NSA_KV_HEADS, T // NSA_TQ),
        in_specs=[pl.BlockSpec((None, NSA_RPG, NSA_TQ, NSA_HD), lambda b, g, i: (b, g, i, 0)),
                  pl.BlockSpec((None, None, 3, nh, NSA_HD), lambda b, g, i: (b, g, 0, 0, 0)), kv(nh),
                  pl.BlockSpec(c2s.shape, lambda b, g, i: (0, 0)),
                  kv(T), kv(T, 2 * NSA_HD), kv(T), kv(T, 2 * NSA_HD),
                  pl.BlockSpec((None, None, NSA_TQ, 16), lambda b, g, i: (b, g, i, 0))],
        out_specs=pl.BlockSpec((None, NSA_TQ, NSA_RPG * NSA_HD), lambda b, g, i: (b, i, g)),
        out_shape=jax.ShapeDtypeStruct((B, T, NSA_DIM), F32),
        scratch_shapes=[pltpu.VMEM((T // NSA_TK, NSA_TQ, NSA_TK), F32)],
        compiler_params=_cparams("arbitrary", "arbitrary", "arbitrary"),
        name="nsa_prompt_attn",
    )(qh, kc, vc, c2s, ksh, vsh, kwh, vwh, gates4)


def _nsa_prompt_pallas(q2, cmp2, sel2, win2, gates2, p, wb, B, T):
    qh, ksh, vsh, kwh, vwh, sel_rows, win_rows = _nsa_prep(q2, sel2, win2, p['nsa_q_norm'], p['nsa_k_norm'], B, T)
    nh = T // CMP_STRIDE
    xh = cmp2.astype(BF16).reshape(B, nh, CMP_STRIDE, 2 * NSA_KV_HEADS, NSA_HD)
    xh = xh.transpose(0, 3, 1, 2, 4).reshape(B, 2 * NSA_KV_HEADS, nh, CMP_STRIDE * NSA_HD)
    kc, vc = _nsa_compress(xh, wb['cmp_w1'], p['nsa_cmp_pos'], p['nsa_cmp_b1'], wb['cmp_w2'], p['nsa_k_norm'][0])
    gates4 = gates2[:, :NSA_KV_HEADS * 16].reshape(B, T, NSA_KV_HEADS, 16).transpose(0, 2, 1, 3)
    o = _nsa_prompt_attn(qh, kc, vc, ksh, vsh, kwh, vwh, gates4)
    shp = (B, T, 2, NSA_KV_HEADS, NSA_HD)
    return o, (cmp2.reshape(shp), sel_rows.reshape(shp), win_rows.reshape(shp)[:, -WINDOW:])


HALF_COLS = CMP_STRIDE * 2 * NSA_KV_DIM
PAGE_HALVES = PAGE_SIZE // CMP_STRIDE
N_PAIR = 2 * NSA_KV_HEADS // 2
CMP_PAGES = 32
SEL_PAGES = 16


def _half_proj_body(x, w_ref, o_ref):
    for pr in range(N_PAIR):
        lhs = jnp.concatenate([x[:, s * 2 * NSA_KV_DIM + pr * LANE:s * 2 * NSA_KV_DIM + (pr + 1) * LANE]
                               for s in range(CMP_STRIDE)], axis=1)
        o_ref[:, pr * 4 * CMP_HIDDEN:(pr + 1) * 4 * CMP_HIDDEN] = jnp.dot(lhs, w_ref[pr], preferred_element_type=F32)


def _half_proj_paged_kernel(pt_ref, *refs):
    pages, (w_ref, o_ref) = refs[:CMP_PAGES], refs[CMP_PAGES:]
    kw = 2 * NSA_KV_DIM
    for pr in range(N_PAIR):
        lhs = jnp.concatenate(
            [jnp.concatenate([pg[pr, pl.ds(s, PAGE_HALVES, stride=CMP_STRIDE), :] for pg in pages], axis=0).astype(BF16)
             for s in range(CMP_STRIDE)], axis=1)
        o_ref[:, pr * kw:(pr + 1) * kw] = jnp.dot(lhs, w_ref[pr], preferred_element_type=F32)


def _half_proj_rows_kernel(x_ref, w_ref, o_ref):
    _half_proj_body(x_ref[...].astype(BF16), w_ref, o_ref)


def _pair_weights(w1):
    top, bot = w1[:, :CMP_STRIDE], w1[:, CMP_STRIDE:]
    tb = jnp.concatenate([top, bot], axis=-1)
    z = jnp.zeros_like(tb)
    par0 = jnp.concatenate([tb, z], axis=-1)
    par1 = jnp.concatenate([z, tb], axis=-1)
    w = jnp.stack([par0, par1], axis=2)
    w = w.reshape(2, CMP_STRIDE * 2 * NSA_HD, 4 * CMP_HIDDEN)
    return jnp.stack([w[0], w[0], w[1], w[1]], axis=0).astype(BF16)


def _half_proj_paged(pool4, layer, page_table, w_pair):
    B, n_pages = page_table.shape
    m = CMP_PAGES * PAGE_HALVES
    page_spec = lambda n: pl.BlockSpec((None, None, N_PAIR, PAGE_SIZE, LANE),
                                       lambda b, j, pt: (layer, pt[b, j * CMP_PAGES + n], 0, 0, 0))
    grid_spec = pltpu.PrefetchScalarGridSpec(
        num_scalar_prefetch=1, grid=(B, n_pages // CMP_PAGES),
        in_specs=[page_spec(n) for n in range(CMP_PAGES)] + [pl.BlockSpec(w_pair.shape, lambda b, j, pt: (0, 0, 0))],
        out_specs=pl.BlockSpec((None, m, 8 * 2 * CMP_HIDDEN), lambda b, j, pt: (b, j, 0)))
    return pl.pallas_call(
        _half_proj_paged_kernel, grid_spec=grid_spec,
        out_shape=jax.ShapeDtypeStruct((B, n_pages * PAGE_HALVES, 8 * 2 * CMP_HIDDEN), F32),
        compiler_params=_cparams("arbitrary", "arbitrary"),
        name="nsa_half_proj_paged",
    )(page_table, *([pool4] * CMP_PAGES), w_pair)


def _half_proj_rows(x, w_pair):
    m = x.shape[0]
    return pl.pallas_call(
        _half_proj_rows_kernel, grid=(1,),
        in_specs=[_full_spec(x.shape), _full_spec(w_pair.shape)],
        out_specs=_full_spec((m, 8 * 2 * CMP_HIDDEN)),
        out_shape=jax.ShapeDtypeStruct((m, 8 * 2 * CMP_HIDDEN), F32),
        compiler_params=_cparams("arbitrary"),
        name="nsa_half_proj_new",
    )(x, w_pair)


def _cmp_finish_kernel(tk_ref, tv_ref, nk_ref, nv_ref, w1_ref, pos_ref, b1_ref, w2_ref, kn_ref, kc_ref, vc_ref):
    for c, t_ref, n_ref in ((0, tk_ref, nk_ref), (1, tv_ref, nv_ref)):
        const = jnp.dot(pos_ref[c].astype(BF16), w1_ref[c], preferred_element_type=F32)[0:1] + b1_ref[c:c + 1]
        tb = t_ref[...]
        top = tb[:, :CMP_HIDDEN]
        bot_next = jnp.concatenate([tb[1:, CMP_HIDDEN:], n_ref[0:1, CMP_HIDDEN:]], axis=0)
        hid = jax.nn.gelu(top + bot_next + const)
        kv = jnp.dot(hid.astype(BF16), w2_ref[c], preferred_element_type=F32)
        if c == 0:
            kn = kv * lax.rsqrt(jnp.mean(kv * kv, axis=-1, keepdims=True) + EPS) * kn_ref[...]
            for i, term in enumerate(_split3(kn)):
                kc_ref[i] = term
        else:
            vc_ref[...] = kv.astype(BF16)


def _cmp_finish(tb_past, tb_new, w1_bf, pos, b1, w2_bf, k_norm0):
    B, nh, _ = tb_past.shape
    wd = 2 * CMP_HIDDEN
    full = lambda shape: pl.BlockSpec(shape, lambda b, g: (0,) * len(shape))
    past = lambda off: pl.BlockSpec((None, nh, wd), lambda b, g: (b, 0, off + g))
    new = lambda off: pl.BlockSpec((None, 8, wd), lambda b, g: (b, 0, off + g))
    out = pl.BlockSpec((None, None, nh, NSA_HD), lambda b, g: (b, g, 0, 0))
    return pl.pallas_call(
        _cmp_finish_kernel, grid=(B, NSA_KV_HEADS),
        in_specs=[past(0), past(NSA_KV_HEADS), new(0), new(NSA_KV_HEADS), full(w1_bf.shape),
                  full((2, 8, CMP_BLOCK * NSA_HD)), full((2, CMP_HIDDEN)), full(w2_bf.shape), full((1, NSA_HD))],
        out_specs=[pl.BlockSpec((None, None, 3, nh, NSA_HD), lambda b, g: (b, g, 0, 0, 0)), out],
        out_shape=[jax.ShapeDtypeStruct((B, NSA_KV_HEADS, 3, nh, NSA_HD), BF16),
                   jax.ShapeDtypeStruct((B, NSA_KV_HEADS, nh, NSA_HD), BF16)],
        compiler_params=_cparams("arbitrary", "arbitrary"),
        name="nsa_cmp_finish",
    )(tb_past, tb_past, tb_new, tb_new, w1_bf, jnp.broadcast_to(pos.reshape(2, 1, -1), (2, 8, CMP_BLOCK * NSA_HD)),
      b1, w2_bf, k_norm0.reshape(1, -1))


def _nsa_sample_kernel(pt_ref, *refs, T, ns_pad):
    pages = refs[:SEL_PAGES]
    (q_ref, kc_ref, vc_ref, c2s_ref, newsel_ref, wincache_ref, newwin_ref, gate_ref, o_ref,
     sel_scr, m_scr, l_scr, acc_scr, ocmp_scr, owin_scr) = refs[SEL_PAGES:]
    H, G, R, D = NSA_HEADS, NSA_KV_HEADS, NSA_RPG, NSA_HD
    rows, gr = H * T, R * T
    j = pl.program_id(1)
    n_steps = pl.num_programs(1)
    tk = SEL_PAGES * PAGE_SIZE
    row_t = PAST_LEN + lax.broadcasted_iota(jnp.int32, (gr, 1), 0) % T
    rs = lambda g: slice(g * gr, (g + 1) * gr)
    kT = lambda x, g: x[g * D:(g + 1) * D]
    vT = lambda x, g: x[G * D + g * D:G * D + (g + 1) * D]

    @pl.when(j == 0)
    def _():
        q = q_ref[...].reshape(rows, D)
        qb = q.astype(BF16)
        nc = kc_ref.shape[2]
        t_q = PAST_LEN + lax.broadcasted_iota(jnp.int32, (T, 1), 0)
        cmp_end = lax.broadcasted_iota(jnp.int32, (1, nc), 1) * CMP_STRIDE + (CMP_BLOCK - 1)
        blk = lax.broadcasted_iota(jnp.int32, (1, ns_pad), 1)
        cur = t_q // SEL_BLOCK
        forced = (blk == 0) | (blk == cur) | (blk == cur - 1)
        valid = blk * SEL_BLOCK <= t_q
        wc = wincache_ref[...].astype(BF16)
        nw = newwin_ref[...].astype(BF16)
        kpos = jnp.concatenate([PAST_LEN - WINDOW + lax.broadcasted_iota(jnp.int32, (1, WINDOW), 1),
                                PAST_LEN + lax.broadcasted_iota(jnp.int32, (1, T), 1)], axis=1)
        win_mask = (kpos <= row_t) & (kpos > row_t - WINDOW)
        s_cmp = [_dot_f32_nt(_split3(q[rs(g)]), (kc_ref[g, 0], kc_ref[g, 1], kc_ref[g, 2])).reshape(R, T, nc)
                 for g in range(G)]
        s_win = [jnp.concatenate([jnp.dot(qb[rs(g)], kT(wc, g), preferred_element_type=F32),
                                  lax.dot_general(qb[rs(g)], nw[:, g * D:(g + 1) * D], NT_DIMS,
                                                  preferred_element_type=F32)], axis=1) for g in range(G)]
        p_cmp = [_softmax_direct(s_cmp[g], cmp_end <= t_q) for g in range(G)]
        p_win = [_softmax_direct(s_win[g][None], win_mask)[0].astype(BF16) for g in range(G)]
        scores = []
        for g in range(G):
            ocmp_scr[rs(g), :] = jnp.dot(p_cmp[g].reshape(gr, nc).astype(BF16), vc_ref[g], preferred_element_type=F32)
            imp = _dot_f32_exact_rhs(jnp.sum(p_cmp[g], axis=0), c2s_ref[...])
            scores.append(jnp.where(forced, SEL_FORCE, jnp.where(valid, imp, -SEL_FORCE)))
            owin_scr[rs(g), :] = (lax.dot_general(p_win[g][:, :WINDOW], vT(wc, g), NT_DIMS, preferred_element_type=F32)
                                  + jnp.dot(p_win[g][:, WINDOW:], nw[:, G * D + g * D:G * D + (g + 1) * D],
                                            preferred_element_type=F32))
        sel = _top_blocks(jnp.concatenate(scores, axis=0), N_SELECT).reshape(G, 1, T, ns_pad)
        sel_scr[...] = jnp.broadcast_to(sel, (G, R, T, ns_pad)).reshape(rows, ns_pad).T
        m_scr[...] = jnp.full(m_scr.shape, NEG_BIG, F32)
        l_scr[...] = jnp.zeros(l_scr.shape, F32)
        acc_scr[...] = jnp.zeros(acc_scr.shape, F32)

    def flash(scores, v_dots, masks):
        stats = []
        for g in range(G):
            sm = jnp.where(masks[g], scores[g], NEG_BIG)
            m_old = m_scr[rs(g), :]
            m_new = jnp.maximum(m_old, jnp.max(sm, axis=-1, keepdims=True))
            p = jnp.where(masks[g], jnp.exp(sm - m_new), 0.0)
            stats.append((m_new, jnp.exp(m_old - m_new), p))
        pvs = [v_dots[g](stats[g][2].astype(BF16)) for g in range(G)]
        for g in range(G):
            m_new, alpha, p = stats[g]
            l_scr[rs(g), :] = alpha * l_scr[rs(g), :] + jnp.sum(p, axis=-1, keepdims=True)
            acc_scr[rs(g), :] = alpha * acc_scr[rs(g), :] + pvs[g]
            m_scr[rs(g), :] = m_new

    qb = q_ref[...].reshape(rows, D).astype(BF16)
    kv = jnp.concatenate([pg[...].astype(BF16) for pg in pages], axis=1)
    per_step = tk // SEL_BLOCK
    tn_dims = (((0,), (0,)), ((), ()))
    expand = (lax.broadcasted_iota(jnp.int32, (per_step, tk), 1) // SEL_BLOCK
              == lax.broadcasted_iota(jnp.int32, (per_step, tk), 0)).astype(BF16)
    sel_step = sel_scr[pl.ds(pl.multiple_of(j * per_step, per_step), per_step), :].astype(BF16)
    member = lax.dot_general(sel_step, expand, tn_dims, preferred_element_type=F32) > 0.5
    flash([jnp.dot(qb[rs(g)], kT(kv, g), preferred_element_type=F32) for g in range(G)],
          [lambda p, g=g: lax.dot_general(p, vT(kv, g), NT_DIMS, preferred_element_type=F32) for g in range(G)],
          [member[rs(g)] for g in range(G)])

    @pl.when(j == n_steps - 1)
    def _():
        new_blk = PAST_LEN // SEL_BLOCK
        kpos = PAST_LEN + lax.broadcasted_iota(jnp.int32, (1, T), 1)
        ns_rows = newsel_ref[...].astype(BF16)
        base = new_blk - new_blk % 8
        pick = (lax.broadcasted_iota(jnp.int32, (8, T), 0) == new_blk % 8).astype(BF16)
        new_member = lax.dot_general(sel_scr[base:base + 8, :].astype(BF16), pick, tn_dims,
                                     preferred_element_type=F32) > 0.5
        flash([lax.dot_general(qb[rs(g)], ns_rows[:, g * D:(g + 1) * D], NT_DIMS, preferred_element_type=F32)
               for g in range(G)],
              [lambda p, g=g: jnp.dot(p, ns_rows[:, G * D + g * D:G * D + (g + 1) * D], preferred_element_type=F32)
               for g in range(G)],
              [new_member[rs(g)] & (kpos <= row_t) for g in range(G)])
        o_sel = acc_scr[...] / jnp.maximum(l_scr[...], F32_TINY)
        gs = jax.nn.sigmoid(gate_ref[...])
        o = gs[:, 0:1] * ocmp_scr[...] + gs[:, 1:2] * o_sel + gs[:, 2:3] * owin_scr[...]
        for h in range(H):
            o_ref[:, h * D:(h + 1) * D] = o[h * T:(h + 1) * T]


def _nsa_sample_attn(qh, kc, vc, pool_sel, layer, page_table, new_sel, win_cache, new_win, gate_rows):
    B, H, T, D = qh.shape
    n_pages = page_table.shape[1]
    nc = kc.shape[3]
    ns = -(-(PAST_LEN + T) // SEL_BLOCK)
    ns_pad = -(-ns // LANE) * LANE
    c2s = _cmp_to_sel_padded(nc, ns_pad)
    rows = H * T
    cst = lambda shape: pl.BlockSpec(shape, lambda b, j, pt: (0,) * len(shape))
    per_b = lambda shape: pl.BlockSpec((None,) + shape, lambda b, j, pt: (b,) + (0,) * len(shape))
    page_spec = lambda n: pl.BlockSpec((None, None, 2 * NSA_KV_DIM, PAGE_SIZE),
                                       lambda b, j, pt: (layer, pt[b, j * SEL_PAGES + n], 0, 0))
    grid_spec = pltpu.PrefetchScalarGridSpec(
        num_scalar_prefetch=1, grid=(B, n_pages // SEL_PAGES),
        in_specs=[page_spec(n) for n in range(SEL_PAGES)] + [
            per_b((H, T, D)), per_b((NSA_KV_HEADS, 3, nc, D)), per_b((NSA_KV_HEADS, nc, D)), cst(c2s.shape),
            per_b((T, 2 * NSA_KV_DIM)), per_b((2 * NSA_KV_DIM, WINDOW)), per_b((T, 2 * NSA_KV_DIM)), per_b((rows, 3))],
        out_specs=per_b((T, NSA_DIM)),
        scratch_shapes=[pltpu.VMEM((ns_pad, rows), F32), pltpu.VMEM((rows, 1), F32), pltpu.VMEM((rows, 1), F32),
                        pltpu.VMEM((rows, D), F32), pltpu.VMEM((rows, D), F32), pltpu.VMEM((rows, D), F32)])
    return pl.pallas_call(
        functools.partial(_nsa_sample_kernel, T=T, ns_pad=ns_pad), grid_spec=grid_spec,
        out_shape=jax.ShapeDtypeStruct((B, T, NSA_DIM), F32),
        compiler_params=_cparams("arbitrary", "arbitrary"),
        name="nsa_sample_attn",
    )(page_table, *([pool_sel] * SEL_PAGES), qh, kc, vc, c2s, new_sel, win_cache, new_win, gate_rows)


def _nsa_sample_pallas(q2, cmp2, sel2, win2, gates2, p, wb, B, T, layer, cache_cmp_kv, cache_sel_kv, win_buf, page_table):
    qh, _, _, _, _, sel_rows, win_rows = _nsa_prep(q2, sel2, win2, p['nsa_q_norm'], p['nsa_k_norm'], B, T)
    depth, n_pool = cache_cmp_kv.shape[:2]
    pool_cmp = cache_cmp_kv.transpose(0, 1, 3, 4, 5, 2).reshape(depth, n_pool, N_PAIR, LANE, PAGE_SIZE)
    pool_cmp = jnp.swapaxes(pool_cmp, 3, 4)
    tb_past = _half_proj_paged(pool_cmp, layer, page_table, wb['cmp_pair'])
    new_half = jnp.pad(cmp2.reshape(B, 1, T * 2 * NSA_KV_DIM), ((0, 0), (0, 7), (0, HALF_COLS - T * 2 * NSA_KV_DIM)))
    tb_new = _half_proj_rows(new_half.reshape(B * 8, HALF_COLS), wb['cmp_pair']).reshape(B, 8, -1)
    kc, vc = _cmp_finish(tb_past, tb_new, wb['cmp_w1'], p['nsa_cmp_pos'], p['nsa_cmp_b1'], wb['cmp_w2'], p['nsa_k_norm'][0])
    pool_sel = cache_sel_kv.transpose(0, 1, 3, 4, 5, 2).reshape(depth, n_pool, 2 * NSA_KV_DIM, PAGE_SIZE)
    win_cache = win_buf.transpose(0, 2, 3, 4, 1).reshape(B, 2 * NSA_KV_DIM, WINDOW)
    g = gates2[:, :NSA_KV_HEADS * 16].reshape(B, T, NSA_KV_HEADS, 4, NSA_RPG)[:, :, :, :3]
    gate_rows = g.transpose(0, 2, 4, 1, 3).reshape(B, NSA_HEADS * T, 3)
    sel3, win3 = sel_rows.reshape(B, T, -1), win_rows.reshape(B, T, -1)
    o = _nsa_sample_attn(qh, kc, vc, pool_sel, layer, page_table, sel3, win_cache, win3, gate_rows)
    shp = (B, T, 2, NSA_KV_HEADS, NSA_HD)
    win_out = jnp.concatenate([win_buf, win_rows.reshape(shp)], axis=1)[:, -WINDOW:]
    return o, (cmp2.reshape(shp), sel_rows.reshape(shp), win_out)


RW_CHUNK = 64
RW_SEQS = 2


def _heads(x):
    return jnp.stack([x[:, h * RW_HD:(h + 1) * RW_HD] for h in range(x.shape[1] // RW_HD)], axis=0)


def _unheads(x):
    return jnp.concatenate([x[h] for h in range(x.shape[0])], axis=1)


def _bmm(a, b):
    return jnp.einsum('hlm,hmn->hln', a.astype(BF16), b.astype(BF16), preferred_element_type=F32)


def _bmm_nt(a, b):
    return jnp.einsum('hlk,hmk->hlm', a.astype(BF16), b.astype(BF16), preferred_element_type=F32)


def _bmm_tn(a, b):
    return jnp.einsum('hlv,hlk->hvk', a.astype(BF16), b.astype(BF16), preferred_element_type=F32)


def _rwkv_kernel(x_ref, shift_ref, s0_ref, mu_ref, w0_ref, a0_ref, w2_ref, a2_ref, g2_ref, kk_ref, ka_ref, rk_ref,
                 lng_ref, lnb_ref, o_ref, sfin_ref, prev_scr, s_scr):
    L = x_ref.shape[1]
    c = pl.program_id(1)

    @pl.when(c == 0)
    def _():
        prev_scr[...] = shift_ref[...]
        s_scr[...] = s0_ref[...]

    row = lax.broadcasted_iota(jnp.int32, (L, L), 0)
    col = lax.broadcasted_iota(jnp.int32, (L, L), 1)
    strict, incl = (col < row)[None], (col <= row)[None]
    n_sq, span = 0, 2
    while span < L:
        n_sq, span = n_sq + 1, span * 2
    for bi in range(x_ref.shape[0]):
        x = x_ref[bi]
        prev = jnp.concatenate([prev_scr[bi], x[:L - 1]], axis=0)
        prev_scr[bi] = x[L - 1:]
        xs = x + mu_ref[...] * (prev - x)
        r, k, v = xs[:, :RW_DIM], xs[:, RW_DIM:2 * RW_DIM], xs[:, 2 * RW_DIM:3 * RW_DIM]
        wl = xs[:, RW_SPLITS[2]:RW_SPLITS[3]]
        al = xs[:, RW_SPLITS[3]:RW_SPLITS[4]]
        gl = xs[:, RW_SPLITS[4]:]
        w = -jax.nn.softplus(-(w0_ref[...] + jnp.dot(jnp.tanh(wl).astype(BF16), w2_ref[...],
                                                     preferred_element_type=F32))) - 0.5
        logd = -jnp.exp(w)
        a = jax.nn.sigmoid(a0_ref[...] + jnp.dot(al.astype(BF16), a2_ref[...], preferred_element_type=F32))
        g = jnp.dot(jax.nn.sigmoid(gl).astype(BF16), g2_ref[...], preferred_element_type=F32)
        kk = k * kk_ref[...]
        k = k * (1.0 + (a - 1.0) * ka_ref[...])
        cs = jnp.dot((col <= row).astype(F32), logd, precision=HIGHEST, preferred_element_type=F32)
        g_incl, g_prev, g_inv = jnp.exp(cs), jnp.exp(cs - logd), jnp.exp(-cs)

        kk_h = _heads(kk)
        kk_h = kk_h * lax.rsqrt(jnp.sum(kk_h * kk_h, axis=-1, keepdims=True) + 1e-12)
        r_h, k_h, v_h = _heads(r), _heads(k), _heads(v)
        inv_h = _heads(g_inv)
        at = kk_h * _heads(g_prev)
        rt = r_h * _heads(g_incl)
        bt = -(kk_h * _heads(a)) * inv_h
        kt = k_h * inv_h
        ar = jnp.concatenate([at, rt], axis=1)
        bk = jnp.concatenate([bt, kt], axis=1)
        gram = _bmm_nt(ar, bk)
        a_ab = jnp.where(strict, gram[:, :L, :L], 0.0)
        a_ak = jnp.where(strict, gram[:, :L, L:], 0.0)
        a_rbk = jnp.concatenate([jnp.where(incl, gram[:, L:, :L], 0.0), jnp.where(incl, gram[:, L:, L:], 0.0)], axis=2)
        tinv = jnp.where((col == row)[None], 1.0, a_ab)
        pw = a_ab
        for _ in range(n_sq):
            pw = _bmm(pw, pw)
            tinv = tinv + _bmm(tinv, pw)

        s0 = s_scr[bi]
        ars = _bmm_nt(ar, s0)
        u = _bmm(tinv, ars[:, :L] + _bmm(a_ak, v_h))
        uv = jnp.concatenate([u, v_h], axis=1)
        o = ars[:, L:] + _bmm(a_rbk, uv)
        s_scr[bi] = (s0 + _bmm_tn(uv, bk)) * _heads(g_incl[L - 1:])

        mean = jnp.mean(o, axis=-1, keepdims=True)
        var = jnp.mean(jnp.square(o - mean), axis=-1, keepdims=True)
        on = _unheads((o - mean) * lax.rsqrt(var + RW_GN_EPS)) * lng_ref[...] + lnb_ref[...]
        bonus = _unheads(jnp.sum(r_h * k_h * _heads(rk_ref[...]), axis=-1, keepdims=True) * v_h)
        o_ref[bi] = (on + bonus) * g

    @pl.when(c == pl.num_programs(1) - 1)
    def _():
        sfin_ref[...] = s_scr[...]


def _rwkv7_pallas(p_rw, shift_state, wkv_state, p, wb):
    B, T, _ = p_rw.shape
    L = min(RW_CHUNK, T)
    nb = RW_SEQS
    assert T % L == 0 and B % nb == 0
    vec = lambda name: p[name].reshape(1, -1)
    full = lambda shape: pl.BlockSpec(shape, lambda b, c: (0,) * len(shape))
    st = pl.BlockSpec((nb, RW_HEADS, RW_HD, RW_HD), lambda b, c: (b, 0, 0, 0))
    o, s_fin = pl.pallas_call(
        _rwkv_kernel,
        grid=(B // nb, T // L),
        in_specs=[pl.BlockSpec((nb, L, RW_IN), lambda b, c: (b, c, 0)),
                  pl.BlockSpec((nb, 1, RW_IN), lambda b, c: (b, 0, 0)), st,
                  full((1, RW_IN)), full((1, RW_DIM)), full((1, RW_DIM)),
                  full((RW_W_LORA, RW_DIM)), full((RW_A_LORA, RW_DIM)), full((RW_G_LORA, RW_DIM)),
                  full((1, RW_DIM)), full((1, RW_DIM)), full((1, RW_DIM)), full((1, RW_DIM)), full((1, RW_DIM))],
        out_specs=[pl.BlockSpec((nb, L, RW_DIM), lambda b, c: (b, c, 0)), st],
        out_shape=[jax.ShapeDtypeStruct((B, T, RW_DIM), F32),
                   jax.ShapeDtypeStruct((B, RW_HEADS, RW_HD, RW_HD), F32)],
        scratch_shapes=[pltpu.VMEM((nb, 1, RW_IN), F32), pltpu.VMEM((nb, RW_HEADS, RW_HD, RW_HD), F32)],
        compiler_params=_cparams("arbitrary", "arbitrary"),
        name="rwkv7_mix",
    )(p_rw, shift_state.reshape(B, 1, RW_IN), wkv_state, vec('rw_mu'), vec('rw_w0'), vec('rw_a0'),
      wb['rw_w2'], wb['rw_a2'], wb['rw_g2'], vec('rw_k_k'), vec('rw_k_a'), vec('rw_r_k'), vec('rw_lnx_g'), vec('rw_lnx_b'))
    return o, p_rw[:, -1], s_fin


SSM_GW = SSM_RPG * SSM_HD


def _mamba_kernel(z_ref, xbc_ref, dt_ref, conv0_ref, s0_ref, cw_ref, cb_ref, dtb_ref, alog_ref, dvec_ref, ng_ref,
                  hexp_ref, o_ref, convf_ref, sfin_ref, tail_scr, s_scr):
    L = z_ref.shape[0]
    c = pl.program_id(1)
    last = pl.num_programs(1) - 1
    G, N, K = SSM_GROUPS, SSM_STATE, SSM_CONV

    @pl.when(c == 0)
    def _():
        tail_scr[...] = conv0_ref[...]
        s_scr[...] = s0_ref[...]

    ext = jnp.concatenate([tail_scr[...], xbc_ref[...]], axis=0)
    new_tail = ext[L:L + K - 1]
    tail_scr[...] = new_tail
    conv = cb_ref[...] + sum(ext[i:i + L] * cw_ref[i:i + 1] for i in range(K))
    act = jax.nn.silu(conv)
    xs, bm, cm = act[:, :SSM_DIM], act[:, SSM_DIM:SSM_DIM + G * N], act[:, SSM_DIM + G * N:]

    dt = jax.nn.softplus(dt_ref[...] + dtb_ref[...])
    da = dt * (-jnp.exp(alog_ref[...]))
    row = lax.broadcasted_iota(jnp.int32, (L, L), 0)
    col = lax.broadcasted_iota(jnp.int32, (L, L), 1)
    causal = col <= row
    a_cs = jnp.dot(causal.astype(F32), da, precision=HIGHEST, preferred_element_type=F32)
    a_last = a_cs[L - 1:]
    a_cs_t = a_cs.T
    per_head = jnp.concatenate([dt, dt * jnp.exp(a_last - a_cs), jnp.exp(a_cs)], axis=0)
    wide = jnp.dot(per_head, hexp_ref[...], precision=HIGHEST, preferred_element_type=F32)
    xdt_b = (xs * wide[:L]).astype(BF16)
    xds = (xs * wide[L:2 * L]).astype(BF16)
    dec_in = wide[2 * L:]
    st_head = lax.broadcasted_iota(jnp.int32, (SSM_GW, LANE), 0) // SSM_HD
    st_lane = lax.broadcasted_iota(jnp.int32, (SSM_GW, LANE), 1)
    dec_chunk = jnp.exp(a_last)

    ys = []
    for g in range(G):
        bg = bm[:, g * N:(g + 1) * N].astype(BF16)
        cg = cm[:, g * N:(g + 1) * N].astype(BF16)
        cb = lax.dot_general(cg, bg, NT_DIMS, preferred_element_type=F32)
        s_in = s_scr[g]
        y_off = lax.dot_general(cg, s_in.astype(BF16), NT_DIMS, preferred_element_type=F32)
        cols = slice(g * SSM_GW, (g + 1) * SSM_GW)
        y_diag = []
        for r in range(SSM_RPG):
            h = g * SSM_RPG + r
            seg = a_cs[:, h:h + 1] - a_cs_t[h:h + 1, :]
            w = (cb * jnp.exp(jnp.where(causal, seg, -jnp.inf))).astype(BF16)
            y_diag.append(jnp.dot(w, xdt_b[:, h * SSM_HD:(h + 1) * SSM_HD], preferred_element_type=F32))
        ys.append(jnp.concatenate(y_diag, axis=1) + y_off * dec_in[:, cols])
        states = lax.dot_general(xds[:, cols], bg, (((0,), (0,)), ((), ())), preferred_element_type=F32)
        dec_col = jnp.sum(jnp.where(st_head + g * SSM_RPG == st_lane, dec_chunk, 0.0), axis=-1, keepdims=True)
        s_scr[g] = s_in * dec_col + states
    y = jnp.concatenate(ys, axis=1) + dvec_ref[...] * xs
    y = y * jax.nn.silu(z_ref[...])
    o_ref[...] = y * lax.rsqrt(jnp.mean(y * y, axis=-1, keepdims=True) + EPS) * ng_ref[...]

    @pl.when(c == last)
    def _():
        convf_ref[...] = new_tail
        sfin_ref[...] = s_scr[...]


def _mamba2_pallas(z, xbc, dt, conv_state, ssm_state, p):
    B, T, _ = z.shape
    L = SSM_CHUNK if T % SSM_CHUNK == 0 else T
    pad = lambda v: jnp.pad(v, (0, LANE - SSM_HEADS)).reshape(1, LANE)
    hexp = (jnp.arange(LANE)[:, None] == jnp.arange(SSM_DIM)[None, :] // SSM_HD).astype(F32)
    full = lambda shape: pl.BlockSpec(shape, lambda b, c: (0,) * len(shape))
    tile = lambda wd: pl.BlockSpec((None, L, wd), lambda b, c: (b, c, 0))
    per_b = lambda shape: pl.BlockSpec((None,) + shape, lambda b, c: (b,) + (0,) * len(shape))
    st_shape = (SSM_GROUPS, SSM_GW, SSM_STATE)
    y, conv_f, s_fin = pl.pallas_call(
        _mamba_kernel,
        grid=(B, T // L),
        in_specs=[tile(SSM_DIM), tile(SSM_XBC), tile(LANE), per_b((SSM_CONV - 1, SSM_XBC)), per_b(st_shape),
                  full((SSM_CONV, SSM_XBC)), full((1, SSM_XBC)), full((1, LANE)), full((1, LANE)), full((1, SSM_DIM)),
                  full((1, SSM_DIM)), full((LANE, SSM_DIM))],
        out_specs=[tile(SSM_DIM), per_b((SSM_CONV - 1, SSM_XBC)), per_b(st_shape)],
        out_shape=[jax.ShapeDtypeStruct((B, T, SSM_DIM), F32),
                   jax.ShapeDtypeStruct((B, SSM_CONV - 1, SSM_XBC), F32),
                   jax.ShapeDtypeStruct((B,) + st_shape, F32)],
        scratch_shapes=[pltpu.VMEM((SSM_CONV - 1, SSM_XBC), F32), pltpu.VMEM(st_shape, F32)],
        compiler_params=_cparams("arbitrary", "arbitrary"),
        name="mamba2_mix",
    )(z, xbc, dt, conv_state, ssm_state.reshape((B,) + st_shape), p['ssm_conv_w'], p['ssm_conv_b'].reshape(1, -1),
      pad(p['ssm_dt_bias']), pad(p['ssm_a_log']), jnp.repeat(p['ssm_d'], SSM_HD).reshape(1, -1),
      p['ssm_norm_g'].reshape(1, -1), hexp)
    return y, conv_f, s_fin.reshape(B, SSM_HEADS, SSM_HD, SSM_STATE)


def _prep_weights(p):
    w_in = p['w_in']
    o_ssm = RW_IN
    o_nsa = RW_IN + SSM_IN
    o_gate = o_nsa + NSA_IN
    pad = lambda w, n: jnp.pad(w, ((0, 0), (0, n - w.shape[1])))
    w_rw = w_in[:, :RW_IN]
    w_ssm = jnp.concatenate([w_in[:, o_ssm:o_ssm + SSM_DIM + SSM_XBC],
                             pad(w_in[:, o_ssm + SSM_DIM + SSM_XBC:o_nsa], LANE)], axis=1)
    o_g = o_nsa + NSA_DIM + 6 * NSA_KV_DIM
    wg = w_in[:, o_g:o_gate].reshape(D_MODEL, 3, NSA_KV_HEADS, NSA_RPG).transpose(0, 2, 1, 3)
    wg = jnp.pad(wg.reshape(D_MODEL, NSA_KV_HEADS, 3 * NSA_RPG), ((0, 0), (0, 0), (0, 16 - 3 * NSA_RPG)))
    w_nsa = jnp.concatenate([w_in[:, o_nsa:o_g], pad(wg.reshape(D_MODEL, NSA_KV_HEADS * 16), LANE),
                             w_in[:, o_gate:]], axis=1)
    bf = lambda w: w.astype(BF16)
    return dict(w_ada=bf(p['w_ada']), w_rw=bf(w_rw), w_ssm=bf(w_ssm), w_nsa=bf(w_nsa),
                w_br_rw=bf(p['w_br_rw']), w_br_ssm=bf(p['w_br_ssm']), w_br_nsa=bf(p['w_br_nsa']),
                w_out=bf(p['w_out']), w_ffn_in=bf(p['w_ffn_in']), w_ffn_out=bf(p['w_ffn_out']),
                rw_w2=bf(p['rw_w2']), rw_a2=bf(p['rw_a2']), rw_g2=bf(p['rw_g2']),
                cmp_w1=bf(p['nsa_cmp_w1'].reshape(2, CMP_BLOCK * NSA_HD, CMP_HIDDEN)), cmp_w2=bf(p['nsa_cmp_w2']),
                cmp_pair=_pair_weights(p['nsa_cmp_w1']))


def _trunk_layer(x, c, p, wb, rw_shift, rw_wkv, conv_state, ssm_state, nsa_fn, per_batch):
    B, T, _ = x.shape
    n = B * T
    mod = _ada(c, wb['w_ada'], p['b_ada'])
    if per_batch:
        mods = [m.reshape(B, 1, D_MODEL) for m in jnp.split(mod, 6, axis=-1)]
    else:
        mods = [jnp.repeat(m, T, axis=0) for m in jnp.split(mod, 6, axis=-1)]
    sh1, sc1, gt1, sh2, sc2, gt2 = mods
    x2 = x.reshape(n, D_MODEL)
    (p_rw,) = _norm_proj(x2, sc1, sh1, p['ln1'], wb['w_rw'], (RW_IN,), per_batch, T, "proj_rw")
    z, xbc, dt = _norm_proj(x2, sc1, sh1, p['ln1'], wb['w_ssm'], (SSM_DIM, SSM_XBC, LANE), per_batch, T, "proj_ssm")
    kvw = 2 * NSA_KV_DIM
    q, cmp2, sel2, win2, gates, p_gate = _norm_proj(x2, sc1, sh1, p['ln1'], wb['w_nsa'],
                                                    (NSA_DIM, kvw, kvw, kvw, LANE, N_BRANCH * D_MODEL), per_batch, T,
                                                    "proj_nsa")
    r3 = lambda t: t.reshape(B, T, t.shape[-1])
    o_rw, rw_shift, rw_wkv = _rwkv7_pallas(r3(p_rw), rw_shift, rw_wkv, p, wb)
    o_ssm, conv_state, ssm_state = _mamba2_pallas(r3(z), r3(xbc), r3(dt), conv_state, ssm_state, p)
    o_nsa, (cmp_rows, sel_rows, win_buf) = nsa_fn(q, cmp2, sel2, win2, gates, p, wb, B, T)
    x2 = _merge(x2, gt1, o_rw.reshape(n, -1), o_ssm.reshape(n, -1), o_nsa.reshape(n, -1), p_gate,
                wb['w_br_rw'], wb['w_br_ssm'], wb['w_br_nsa'], wb['w_out'], per_batch, T)
    x2 = _ffn(x2, sc2, sh2, gt2, p['ln2'], wb['w_ffn_in'], wb['w_ffn_out'], per_batch, T)
    return x2.reshape(B, T, D_MODEL), (rw_shift, rw_wkv, conv_state, ssm_state, cmp_rows, sel_rows, win_buf)


def kernel(x_prompt, x_sample, cache_cmp_kv, cache_sel_kv, cache_win_kv, state_rwkv_shift, state_rwkv_wkv,
           state_ssm_conv, state_ssm, page_table, c_prompt, c_sample, w_ada, b_ada, ln1, ln2, w_in,
           rw_mu, rw_w0, rw_w2, rw_a0, rw_a2, rw_g2, rw_k_k, rw_k_a, rw_r_k, rw_lnx_g, rw_lnx_b,
           ssm_conv_w, ssm_conv_b, ssm_dt_bias, ssm_a_log, ssm_d, ssm_norm_g,
           nsa_q_norm, nsa_k_norm, nsa_cmp_pos, nsa_cmp_w1, nsa_cmp_b1, nsa_cmp_w2,
           w_br_rw, w_br_ssm, w_br_nsa, w_out, w_ffn_in, w_ffn_out):
    params = dict(w_ada=w_ada, b_ada=b_ada, ln1=ln1, ln2=ln2, w_in=w_in,
                  rw_mu=rw_mu, rw_w0=rw_w0, rw_w2=rw_w2, rw_a0=rw_a0, rw_a2=rw_a2, rw_g2=rw_g2,
                  rw_k_k=rw_k_k, rw_k_a=rw_k_a, rw_r_k=rw_r_k, rw_lnx_g=rw_lnx_g, rw_lnx_b=rw_lnx_b,
                  ssm_conv_w=ssm_conv_w, ssm_conv_b=ssm_conv_b, ssm_dt_bias=ssm_dt_bias, ssm_a_log=ssm_a_log,
                  ssm_d=ssm_d, ssm_norm_g=ssm_norm_g, nsa_q_norm=nsa_q_norm, nsa_k_norm=nsa_k_norm,
                  nsa_cmp_pos=nsa_cmp_pos, nsa_cmp_w1=nsa_cmp_w1, nsa_cmp_b1=nsa_cmp_b1, nsa_cmp_w2=nsa_cmp_w2,
                  w_br_rw=w_br_rw, w_br_ssm=w_br_ssm, w_br_nsa=w_br_nsa, w_out=w_out,
                  w_ffn_in=w_ffn_in, w_ffn_out=w_ffn_out)
    bp = x_prompt.shape[0]
    depth = w_in.shape[0]
    zero_shift = jnp.zeros((bp, RW_IN), F32)
    zero_wkv = jnp.zeros((bp, RW_HEADS, RW_HD, RW_HD), F32)
    zero_conv = jnp.zeros((bp, SSM_CONV - 1, SSM_XBC), F32)
    zero_ssm = jnp.zeros((bp, SSM_HEADS, SSM_HD, SSM_STATE), F32)
    xp, xs = x_prompt, x_sample
    st_p, st_s = [], []
    for l in range(depth):
        p = {name: arr[l] for name, arr in params.items()}
        wb = _prep_weights(p)
        xp, sp_l = _trunk_layer(xp, c_prompt, p, wb, zero_shift, zero_wkv, zero_conv, zero_ssm, _nsa_prompt_pallas, True)
        nsa_s = functools.partial(_nsa_sample_pallas, layer=l, cache_cmp_kv=cache_cmp_kv, cache_sel_kv=cache_sel_kv,
                                  win_buf=cache_win_kv[l], page_table=page_table)
        xs, ss_l = _trunk_layer(xs, c_sample, p, wb, state_rwkv_shift[l], state_rwkv_wkv[l], state_ssm_conv[l],
                                state_ssm[l], nsa_s, False)
        st_p.append(sp_l)
        st_s.append(ss_l)
    sp = [jnp.stack([s[i] for s in st_p]) for i in range(7)]
    ss = [jnp.stack([s[i] for s in st_s]) for i in range(7)]
    return (xp, xs, sp[4], ss[4], sp[5], ss[5], sp[6], ss[6], sp[0], ss[0], sp[1], ss[1], sp[2], ss[2], sp[3], ss[3])
```
